```python
import jax, jax.numpy as jnp
from jax import lax
import numpy as np

D_MODEL = 2048
BATCH = 4
SEQ = 2048
DEPTH = 2
DEC_BATCH = 8
DEC_SEQ = 1
PAST_LEN = 16384
PAGE_SIZE = 128

HEAD_DIM = 128
D_A = D_MODEL // 2
N_HEADS_A = D_A // HEAD_DIM
D_B = D_MODEL // 4
N_HEADS_B = D_B // HEAD_DIM
D_C = D_MODEL - D_A - D_B
N_GROUPS_C = D_C // HEAD_DIM
CHUNK = 128
QBLOCK = 128
CONV_W = 31
D_FF = 4 * D_MODEL
FORGET_BIAS = 8.0
OFF_K = D_A
OFF_V = 2 * D_A
OFF_F = 3 * D_A
OFF_UB = OFF_F + N_HEADS_A
OFF_VB = OFF_UB + D_B
OFF_AC = OFF_VB + D_B
OFF_GC = OFF_AC + D_C
IN_COLS = OFF_GC + D_C
EPS = 1e-6

kernel_name = 'hymba_fox_sgu_conformer_decode_step'


def rmsnorm(x, g):
    xf = x.astype(jnp.float32)
    y = xf * lax.rsqrt(jnp.mean(xf * xf, axis=-1, keepdims=True) + EPS)
    return (y * g.astype(jnp.float32)).astype(x.dtype)


def layernorm(x, g, b):
    xf = x.astype(jnp.float32)
    mu = jnp.mean(xf, axis=-1, keepdims=True)
    xc = xf - mu
    y = xc * lax.rsqrt(jnp.mean(xc * xc, axis=-1, keepdims=True) + EPS)
    return (y * g.astype(jnp.float32) + b.astype(jnp.float32)).astype(x.dtype)


def heads(t, n):
    return t.reshape(t.shape[:-1] + (n, HEAD_DIM))


def proj_in(x, g_pre, w_in, b_f, g_v):
    z = rmsnorm(x, g_pre) @ w_in
    q = heads(z[..., :OFF_K], N_HEADS_A)
    k = heads(z[..., OFF_K:OFF_V], N_HEADS_A)
    v = heads(z[..., OFF_V:OFF_F], N_HEADS_A)
    logf = jax.nn.log_sigmoid((z[..., OFF_F:OFF_UB] + b_f).astype(jnp.float32))
    u_b = heads(jax.nn.gelu(z[..., OFF_UB:OFF_VB]), N_HEADS_B)
    v_b = heads(rmsnorm(jax.nn.gelu(z[..., OFF_VB:OFF_AC]), g_v), N_HEADS_B)
    glu = z[..., OFF_AC:OFF_GC] * jax.nn.sigmoid(z[..., OFF_GC:])
    return q, k, v, logf, u_b, v_b, glu


def depthwise_conv(xpad, w, b):
    y = lax.conv_general_dilated(xpad, w.astype(xpad.dtype)[:, None, :], window_strides=(1,),
                                 padding='VALID', dimension_numbers=('NWC', 'WIO', 'NWC'),
                                 feature_group_count=D_C)
    return y + b.astype(y.dtype)


def conv_tail(y, ln_g, ln_b):
    return jax.nn.silu(layernorm(y, ln_g, ln_b))


def finish_layer(x, attn, sgu, conv, w_o, g_post_mix, g_pre_mlp, g_post_mlp, w_up, w_down):
    mixed = jnp.concatenate([attn.reshape(attn.shape[:2] + (D_A,)),
                             sgu.reshape(sgu.shape[:2] + (D_B,)), conv], axis=-1) @ w_o
    x = x + rmsnorm(mixed, g_post_mix)
    hid = jnp.square(jax.nn.relu(rmsnorm(x, g_pre_mlp) @ w_up))
    return x + rmsnorm(hid @ w_down, g_post_mlp)


def fox_prompt(q, k, v, logf):
    b, t = q.shape[0], q.shape[1]
    nb = t // QBLOCK
    scale = HEAD_DIM ** -0.5
    c = jnp.cumsum(logf, axis=1)
    c_keys = c.transpose(0, 2, 1)
    kpos = jnp.arange(t)
    qb = q.reshape(b, nb, QBLOCK, N_HEADS_A, HEAD_DIM).transpose(1, 0, 2, 3, 4)
    cb = c.reshape(b, nb, QBLOCK, N_HEADS_A).transpose(1, 0, 3, 2)

    def block(args):
        qi, ci, i = args
        s = jnp.einsum('bqhd,bkhd->bhqk', qi, k).astype(jnp.float32) * scale
        s = s + ci[..., :, None] - c_keys[:, :, None, :]
        qpos = i * QBLOCK + jnp.arange(QBLOCK)
        s = jnp.where(qpos[:, None] >= kpos[None, :], s, -jnp.inf)
        p = jax.nn.softmax(s, axis=-1).astype(v.dtype)
        return jnp.einsum('bhqk,bkhd->bqhd', p, v)

    out = lax.map(block, (qb, cb, jnp.arange(nb)))
    return out.transpose(1, 0, 2, 3, 4).reshape(b, t, N_HEADS_A, HEAD_DIM)


def fox_sample(q, k, v, logf, k_past, v_past, logf_past):
    scale = HEAD_DIM ** -0.5
    ds = q.shape[1]
    suffix = lax.cumsum(logf_past, axis=1, reverse=True) - logf_past
    cnew = jnp.cumsum(logf, axis=1)
    cn = cnew.transpose(0, 2, 1)
    s_past = jnp.einsum('bqhd,bkhd->bhqk', q, k_past).astype(jnp.float32) * scale
    s_past = s_past + suffix.transpose(0, 2, 1)[:, :, None, :] + cn[..., :, None]
    s_new = jnp.einsum('bqhd,bkhd->bhqk', q, k).astype(jnp.float32) * scale
    s_new = s_new + cn[..., :, None] - cn[..., None, :]
    causal = jnp.tril(jnp.ones((ds, ds), dtype=bool))
    s_new = jnp.where(causal, s_new, -jnp.inf)
    p = jax.nn.softmax(jnp.concatenate([s_past, s_new], axis=-1), axis=-1).astype(v.dtype)
    n_past = k_past.shape[1]
    return (jnp.einsum('bhqk,bkhd->bqhd', p[..., :n_past], v_past)
            + jnp.einsum('bhqk,bkhd->bqhd', p[..., n_past:], v))


def setup_inputs(seed: int = 0) -> dict:
    key = jax.random.key(seed)
    ks = jax.random.split(key, 24)
    n_pages = PAST_LEN // PAGE_SIZE
    n_pool = (DEC_BATCH * n_pages * 5) // 4
    f32 = jnp.float32
    nrm = lambda k, shape, s: jax.random.normal(k, shape, f32) * s
    page_table = jax.random.permutation(ks[0], n_pool)[:DEC_BATCH * n_pages]
    page_table = page_table.reshape(DEC_BATCH, n_pages).astype(jnp.int32)
    return {
        'x_prompt': nrm(ks[1], (BATCH, SEQ, D_MODEL), 1.0),
        'x_sample': nrm(ks[2], (DEC_BATCH, DEC_SEQ, D_MODEL), 1.0),
        'cache_k': nrm(ks[3], (n_pool, DEPTH, PAGE_SIZE, N_HEADS_A, HEAD_DIM), 1.0),
        'cache_v': nrm(ks[4], (n_pool, DEPTH, PAGE_SIZE, N_HEADS_A, HEAD_DIM), 1.0),
        'cache_logf': jax.nn.log_sigmoid(FORGET_BIAS + nrm(ks[5], (n_pool, DEPTH, PAGE_SIZE, N_HEADS_A), 1.0)),
        'state_conv': nrm(ks[6], (DEPTH, DEC_BATCH, CONV_W - 1, D_C), 0.5),
        'page_table': page_table,
        'w_in': nrm(ks[7], (DEPTH, D_MODEL, IN_COLS), D_MODEL ** -0.5),
        'b_f': FORGET_BIAS + nrm(ks[8], (DEPTH, N_HEADS_A), 0.1),
        'g_v': 1.0 + nrm(ks[9], (DEPTH, D_B), 0.01),
        'w_s': nrm(ks[10], (DEPTH, N_HEADS_B, CHUNK, CHUNK), CHUNK ** -0.5),
        'b_s': 1.0 + nrm(ks[11], (DEPTH, N_HEADS_B, CHUNK), 0.1),
        'conv_w': nrm(ks[12], (DEPTH, CONV_W, D_C), CONV_W ** -0.5),
        'conv_b': nrm(ks[13], (DEPTH, D_C), 0.01),
        'ln_c_g': 1.0 + nrm(ks[14], (DEPTH, D_C), 0.01),
        'ln_c_b': nrm(ks[15], (DEPTH, D_C), 0.01),
        'w_o': nrm(ks[16], (DEPTH, D_MODEL, D_MODEL), D_MODEL ** -0.5),
        'g_pre_mix': 1.0 + nrm(ks[17], (DEPTH, D_MODEL), 0.01),
        'g_post_mix': 1.0 + nrm(ks[18], (DEPTH, D_MODEL), 0.01),
        'g_pre_mlp': 1.0 + nrm(ks[19], (DEPTH, D_MODEL), 0.01),
        'g_post_mlp': 1.0 + nrm(ks[20], (DEPTH, D_MODEL), 0.01),
        'w_up': nrm(ks[21], (DEPTH, D_MODEL, D_FF), D_MODEL ** -0.5),
        'w_down': nrm(ks[22], (DEPTH, D_FF, D_MODEL), D_FF ** -0.5),
    }


def reference(x_prompt, x_sample, cache_k, cache_v, cache_logf, state_conv, page_table,
              w_in, b_f, g_v, w_s, b_s, conv_w, conv_b, ln_c_g, ln_c_b, w_o,
              g_pre_mix, g_post_mix, g_pre_mlp, g_post_mlp, w_up, w_down):
    bp, tp = x_prompt.shape[0], x_prompt.shape[1]
    bs, ds = x_sample.shape[0], x_sample.shape[1]
    n_past = page_table.shape[1] * PAGE_SIZE
    xp, xs = x_prompt, x_sample
    kp_l, vp_l, fp_l, cp_l = [], [], [], []
    ks_l, vs_l, fs_l, cs_l, us_l = [], [], [], [], []
    for l in range(DEPTH):
        w_mask = jnp.tril(w_s[l])
        bias_s = b_s[l].T[:, :, None]
        q, k, v, logf, u_b, v_b, glu = proj_in(xp, g_pre_mix[l], w_in[l], b_f[l], g_v[l])
        attn = fox_prompt(q, k, v, logf)
        vc = v_b.reshape(bp, tp // CHUNK, CHUNK, N_HEADS_B, HEAD_DIM)
        mix_b = jnp.einsum('hts,bcshd->bcthd', w_mask, vc) + bias_s
        sgu = u_b * mix_b.reshape(bp, tp, N_HEADS_B, HEAD_DIM)
        gpad = jnp.pad(glu, ((0, 0), (CONV_W - 1, 0), (0, 0)))
        conv = conv_tail(depthwise_conv(gpad, conv_w[l], conv_b[l]), ln_c_g[l], ln_c_b[l])
        xp = finish_layer(xp, attn, sgu, conv, w_o[l], g_post_mix[l], g_pre_mlp[l],
                          g_post_mlp[l], w_up[l], w_down[l])
        kp_l.append(k); vp_l.append(v); fp_l.append(logf)
        cp_l.append(gpad[:, -(CONV_W - 1):])
        q, k, v, logf, u_b, v_b, glu = proj_in(xs, g_pre_mix[l], w_in[l], b_f[l], g_v[l])
        k_past = cache_k[page_table, l].reshape(bs, n_past, N_HEADS_A, HEAD_DIM)
        v_past = cache_v[page_table, l].reshape(bs, n_past, N_HEADS_A, HEAD_DIM)
        f_past = cache_logf[page_table, l].reshape(bs, n_past, N_HEADS_A).astype(jnp.float32)
        attn = fox_sample(q, k, v, logf, k_past, v_past, f_past)
        mix_b = jnp.einsum('hts,bshd->bthd', w_mask[:, :ds, :ds], v_b) + bias_s[:ds]
        sgu = u_b * mix_b
        gcat = jnp.concatenate([state_conv[l].astype(glu.dtype), glu], axis=1)
        conv = conv_tail(depthwise_conv(gcat, conv_w[l], conv_b[l]), ln_c_g[l], ln_c_b[l])
        xs = finish_layer(xs, attn, sgu, conv, w_o[l], g_post_mix[l], g_pre_mlp[l],
                          g_post_mlp[l], w_up[l], w_down[l])
        ks_l.append(k); vs_l.append(v); fs_l.append(logf)
        cs_l.append(gcat[:, -(CONV_W - 1):])
        us_l.append(v_b.reshape(bs, ds, D_B))
    npp = tp // PAGE_SIZE
    k_prompt = jnp.stack(kp_l, axis=1).reshape(bp, DEPTH, npp, PAGE_SIZE, N_HEADS_A, HEAD_DIM).transpose(0, 2, 1, 3, 4, 5)
    v_prompt = jnp.stack(vp_l, axis=1).reshape(bp, DEPTH, npp, PAGE_SIZE, N_HEADS_A, HEAD_DIM).transpose(0, 2, 1, 3, 4, 5)
    logf_prompt = jnp.stack(fp_l, axis=1).reshape(bp, DEPTH, npp, PAGE_SIZE, N_HEADS_A).transpose(0, 2, 1, 3, 4)
    conv_prompt = jnp.stack(cp_l, axis=0)
    k_sample = jnp.stack(ks_l, axis=1)
    v_sample = jnp.stack(vs_l, axis=1)
    logf_sample = jnp.stack(fs_l, axis=1)
    conv_sample = jnp.stack(cs_l, axis=0)
    chunkv_sample = jnp.stack(us_l, axis=0)
    return (xp, xs, k_prompt, v_prompt, logf_prompt, conv_prompt,
            k_sample, v_sample, logf_sample, conv_sample, chunkv_sample)
```

```python
import functools

import numpy as np
import jax
import jax.numpy as jnp
from jax import lax
from jax.experimental import pallas as pl
from jax.experimental.pallas import tpu as pltpu

F32 = jnp.float32
BF16 = jnp.bfloat16

HEAD_DIM = 128
CHUNK = 128
EPS = 1e-6
MIB = 1024 * 1024
GELU_C = float(np.sqrt(2.0 / np.pi))

PROJ_TN = 768


def _params(semantics, vmem_mib):
    return pltpu.CompilerParams(dimension_semantics=semantics, vmem_limit_bytes=vmem_mib * MIB)


def _rms(x, g):
    return x * lax.rsqrt(jnp.mean(x * x, axis=-1, keepdims=True) + EPS) * g


def _gelu(x):
    return x * (0.5 * (1.0 + jnp.tanh(GELU_C * (x + 0.044715 * (x * x * x)))))


def _sigmoid(x):
    return 1.0 / (1.0 + jnp.exp(-x))


def _log_sigmoid(x):
    return jnp.minimum(x, 0.0) - jnp.log1p(jnp.exp(-jnp.abs(x)))


def _split3(x):
    hi = x.astype(BF16)
    r = x - hi.astype(F32)
    mid = r.astype(BF16)
    lo = (r - mid.astype(F32)).astype(BF16)
    return hi, mid, lo


def _dot(a, b):
    return jnp.dot(a, b, preferred_element_type=F32)


def _dot_nt(a, b):
    return lax.dot_general(a, b, (((1,), (1,)), ((), ())), preferred_element_type=F32)


def _norm_kernel(x_ref, g_ref, o_ref):
    o_ref[...] = _rms(x_ref[...], g_ref[...]).astype(o_ref.dtype)


def _norm_bf16(x, g, tm):
    m, d = x.shape
    return pl.pallas_call(
        _norm_kernel,
        grid=(m // tm,),
        in_specs=[pl.BlockSpec((tm, d), lambda i: (i, 0)), pl.BlockSpec((1, d), lambda i: (0, 0))],
        out_specs=pl.BlockSpec((tm, d), lambda i: (i, 0)),
        out_shape=jax.ShapeDtypeStruct((m, d), BF16),
        compiler_params=_params(("arbitrary",), 32),
        name="pre_norm",
    )(x, g)


def _mm_kernel(x_ref, w_ref, o_ref):
    o_ref[...] = _dot(x_ref[...], w_ref[...]).astype(o_ref.dtype)


def _mm_relu2_kernel(x_ref, w_ref, o_ref):
    h = jnp.maximum(_dot(x_ref[...], w_ref[...]), 0.0)
    o_ref[...] = (h * h).astype(o_ref.dtype)


def _matmul(body, x, w, tm, tn, out_dtype, name):
    m, k = x.shape
    n = w.shape[1]
    return pl.pallas_call(
        body,
        grid=(m // tm, n // tn),
        in_specs=[pl.BlockSpec((tm, k), lambda i, j: (i, 0)), pl.BlockSpec((k, tn), lambda i, j: (0, j))],
        out_specs=pl.BlockSpec((tm, tn), lambda i, j: (i, j)),
        out_shape=jax.ShapeDtypeStruct((m, n), out_dtype),
        compiler_params=_params(("arbitrary", "arbitrary"), 48),
        name=name,
    )(x, w)


def _wo_kernel(a_ref, s_ref, c_ref, w_ref, x_ref, g1_ref, g2_ref, x1_ref, xn_ref, *, d_a, d_b):
    mixed = _dot(a_ref[...], w_ref[0:d_a, :])
    mixed += _dot(s_ref[...], w_ref[d_a:d_a + d_b, :])
    mixed += _dot(c_ref[...], w_ref[d_a + d_b:, :])
    x1 = x_ref[...] + _rms(mixed, g1_ref[...])
    x1_ref[...] = x1
    xn_ref[...] = _rms(x1, g2_ref[...]).astype(xn_ref.dtype)


def _wo_block(attn, sgu, conv, w_o, x, g_post, g_pre_mlp, tm):
    m, d = x.shape
    d_a, d_b, d_c = attn.shape[1], sgu.shape[1], conv.shape[1]
    row = lambda width: pl.BlockSpec((tm, width), lambda i: (i, 0))
    const = lambda shape: pl.BlockSpec(shape, lambda i: (0, 0))
    return pl.pallas_call(
        functools.partial(_wo_kernel, d_a=d_a, d_b=d_b),
        grid=(m // tm,),
        in_specs=[row(d_a), row(d_b), row(d_c), const((d, d)), row(d), const((1, d)), const((1, d))],
        out_specs=[row(d), row(d)],
        out_shape=[jax.ShapeDtypeStruct((m, d), F32), jax.ShapeDtypeStruct((m, d), BF16)],
        compiler_params=_params(("arbitrary",), 48),
        name="wo_norm_residual",
    )(attn, sgu, conv, w_o, x, g_post, g_pre_mlp)


def _down_kernel(h_ref, w_ref, x1_ref, g_ref, *rest, emit_next):
    if emit_next:
        gn_ref, x2_ref, xn_ref, acc_ref = rest
    else:
        x2_ref, acc_ref = rest
    k = pl.program_id(1)

    @pl.when(k == 0)
    def _():
        acc_ref[...] = jnp.zeros_like(acc_ref)

    acc_ref[...] += _dot(h_ref[...], w_ref[...])

    @pl.when(k == pl.num_programs(1) - 1)
    def _():
        x2 = x1_ref[...] + _rms(acc_ref[...], g_ref[...])
        x2_ref[...] = x2
        if emit_next:
            xn_ref[...] = _rms(x2, gn_ref[...]).astype(xn_ref.dtype)


def _down_block(hid, w_down, x1, g_post, g_next, tm, tk):
    m, d = x1.shape
    kdim = hid.shape[1]
    emit_next = g_next is not None
    row = pl.BlockSpec((tm, d), lambda i, k: (i, 0))
    const = pl.BlockSpec((1, d), lambda i, k: (0, 0))
    in_specs = [pl.BlockSpec((tm, tk), lambda i, k: (i, k)), pl.BlockSpec((tk, d), lambda i, k: (k, 0)), row, const]
    args = [hid, w_down, x1, g_post]
    out_specs = [row]
    out_shape = [jax.ShapeDtypeStruct((m, d), F32)]
    if emit_next:
        in_specs.append(const)
        args.append(g_next)
        out_specs.append(row)
        out_shape.append(jax.ShapeDtypeStruct((m, d), BF16))
    out = pl.pallas_call(
        functools.partial(_down_kernel, emit_next=emit_next),
        grid=(m // tm, kdim // tk),
        in_specs=in_specs,
        out_specs=out_specs,
        out_shape=out_shape,
        scratch_shapes=[pltpu.VMEM((tm, d), F32)],
        compiler_params=_params(("arbitrary", "arbitrary"), 48),
        name="down_norm_residual",
    )(*args)
    return (out[0], out[1]) if emit_next else (out[0], None)


def _prefix_kernel(zf_ref, bf_ref, logf_ref, cq_ref, ct_ref, c_sc, *, t, n_heads):
    lf = _log_sigmoid(zf_ref[0] + bf_ref[...])
    logf_ref[0] = lf[:, 0:n_heads]
    r_i = lax.broadcasted_iota(jnp.int32, (CHUNK, CHUNK), 0)
    c_i = lax.broadcasted_iota(jnp.int32, (CHUNK, CHUNK), 1)
    tri = jnp.where(r_i >= c_i, 1.0, 0.0).astype(BF16)
    carry = jnp.zeros((1, 128), F32)
    for blk in range(t // CHUNK):
        hi, mid, lo = _split3(lf[blk * CHUNK:(blk + 1) * CHUNK, :])
        cb = (_dot(tri, hi) + _dot(tri, mid)) + _dot(tri, lo) + carry
        c_sc[blk * CHUNK:(blk + 1) * CHUNK, :] = cb
        carry = cb[CHUNK - 1:CHUNK, :]
    c = c_sc[...]
    ct_ref[0] = c.T[0:n_heads, :]
    for h in range(n_heads):
        cq_ref[0, h] = c[:, h:h + 1]


def _fox_prefix(z3, bf_pad, n_heads, f_blk):
    b, t, _ = z3.shape
    return pl.pallas_call(
        functools.partial(_prefix_kernel, t=t, n_heads=n_heads),
        grid=(b,),
        in_specs=[pl.BlockSpec((1, t, 128), lambda i: (i, 0, f_blk)), pl.BlockSpec((1, 128), lambda i: (0, 0))],
        out_specs=[
            pl.BlockSpec((1, t, n_heads), lambda i: (i, 0, 0)),
            pl.BlockSpec((1, n_heads, t, 1), lambda i: (i, 0, 0, 0)),
            pl.BlockSpec((1, n_heads, t), lambda i: (i, 0, 0)),
        ],
        out_shape=[
            jax.ShapeDtypeStruct((b, t, n_heads), F32),
            jax.ShapeDtypeStruct((b, n_heads, t, 1), F32),
            jax.ShapeDtypeStruct((b, n_heads, t), F32),
        ],
        scratch_shapes=[pltpu.VMEM((t, 128), F32)],
        compiler_params=_params(("arbitrary",), 48),
        name="fox_prefix",
    )(z3, bf_pad)


def _fox_attn_kernel(q_ref, k_ref, v_ref, cq_ref, ck_ref, o_ref, m_sc, l_sc, acc_sc, *, tq, tk, n_heads, scale):
    qi = pl.program_id(1)
    ki = pl.program_id(2)
    n_sub = tk // 128

    @pl.when(ki == 0)
    def _():
        m_sc[...] = jnp.full_like(m_sc, -jnp.inf)
        l_sc[...] = jnp.zeros_like(l_sc)
        acc_sc[...] = jnp.zeros_like(acc_sc)

    @pl.when(ki <= qi)
    def _():
        row = lax.broadcasted_iota(jnp.int32, (tq, tk), 0) + qi * tq
        col = lax.broadcasted_iota(jnp.int32, (tq, tk), 1) + ki * tk
        keep = row >= col
        for h in range(n_heads):
            hs = slice(h * HEAD_DIM, (h + 1) * HEAD_DIM)
            qh = q_ref[0, :, hs].astype(BF16)
            kh = k_ref[0, :, hs].astype(BF16)
            vh = v_ref[0, :, hs].astype(BF16)
            s = _dot_nt(qh, kh) * scale + cq_ref[0, h] - ck_ref[0, h:h + 1, :]
            s = jnp.where(keep, s, -jnp.inf)
            subs = [s[:, j * 128:(j + 1) * 128] for j in range(n_sub)]
            mc = subs[0]
            for x in subs[1:]:
                mc = jnp.maximum(mc, x)
            m_prev = m_sc[h]
            m_new = jnp.maximum(m_prev, jnp.max(mc, axis=1, keepdims=True))
            alpha = jnp.exp(m_prev - m_new)
            ps = [jnp.exp(x - m_new) for x in subs]
            lsum = ps[0]
            for x in ps[1:]:
                lsum = lsum + x
            l_sc[h] = alpha * l_sc[h] + jnp.sum(lsum, axis=1, keepdims=True)
            p = jnp.concatenate(ps, axis=1).astype(BF16)
            acc_sc[:, hs] = alpha * acc_sc[:, hs] + _dot(p, vh)
            m_sc[h] = m_new

    @pl.when(ki == qi)
    def _():
        for h in range(n_heads):
            hs = slice(h * HEAD_DIM, (h + 1) * HEAD_DIM)
            o_ref[0, :, hs] = (acc_sc[:, hs] / l_sc[h]).astype(o_ref.dtype)


def _fox_attn_prompt(z3, cq, ct, n_heads, tq, tk):
    b, t, _ = z3.shape
    d_a = n_heads * HEAD_DIM
    kv_blk = lambda col: pl.BlockSpec((1, tk, d_a), lambda bi, qi, ki: (bi, jnp.minimum(ki, qi), col))
    return pl.pallas_call(
        functools.partial(_fox_attn_kernel, tq=tq, tk=tk, n_heads=n_heads, scale=HEAD_DIM ** -0.5),
        grid=(b, t // tq, t // tk),
        in_specs=[
            pl.BlockSpec((1, tq, d_a), lambda bi, qi, ki: (bi, qi, 0)),
            kv_blk(1),
            kv_blk(2),
            pl.BlockSpec((1, n_heads, tq, 1), lambda bi, qi, ki: (bi, 0, qi, 0)),
            pl.BlockSpec((1, n_heads, tk), lambda bi, qi, ki: (bi, 0, jnp.minimum(ki, qi))),
        ],
        out_specs=pl.BlockSpec((1, tq, d_a), lambda bi, qi, ki: (bi, qi, 0)),
        out_shape=jax.ShapeDtypeStruct((b, t, d_a), BF16),
        scratch_shapes=[
            pltpu.VMEM((n_heads, tq, 128), F32),
            pltpu.VMEM((n_heads, tq, 128), F32),
            pltpu.VMEM((tq, d_a), F32),
        ],
        compiler_params=_params(("arbitrary", "arbitrary", "arbitrary"), 56),
        name="fox_attn_prompt",
    )(z3, z3, z3, cq, ct)


def _layernorm_silu(y, g, b):
    mu = jnp.mean(y, axis=-1, keepdims=True)
    yc = y - mu
    yn = yc * lax.rsqrt(jnp.mean(yc * yc, axis=-1, keepdims=True) + EPS) * g + b
    return yn * _sigmoid(yn)


def _mixer_kernel(ub_ref, vb_ref, ac_ref, gc_ref, gv_ref, ws_ref, bst_ref, cw_ref, cb_ref, lg_ref, lb_ref,
                  sgu_ref, conv_ref, tail_ref, g_sc, *, tm, n_heads_b, conv_w, halo):
    ti = pl.program_id(1)

    u = _gelu(ub_ref[0])
    vn = _rms(_gelu(vb_ref[0]), gv_ref[...])
    r_i = lax.broadcasted_iota(jnp.int32, (CHUNK, CHUNK), 0)
    c_i = lax.broadcasted_iota(jnp.int32, (CHUNK, CHUNK), 1)
    for h in range(n_heads_b):
        hs = slice(h * HEAD_DIM, (h + 1) * HEAD_DIM)
        w_h = jnp.where(r_i >= c_i, ws_ref[h], 0.0).astype(BF16)
        bias_h = bst_ref[:, h:h + 1]
        for c in range(tm // CHUNK):
            rs = slice(c * CHUNK, (c + 1) * CHUNK)
            mix = _dot(w_h, vn[rs, hs].astype(BF16)) + bias_h
            sgu_ref[0, rs, hs] = (u[rs, hs] * mix).astype(sgu_ref.dtype)

    @pl.when(ti == 0)
    def _():
        g_sc[0:halo, :] = jnp.zeros((halo, g_sc.shape[1]), F32)

    g_sc[halo:halo + tm, :] = ac_ref[0] * _sigmoid(gc_ref[0])
    base = halo - (conv_w - 1)
    y = g_sc[base:base + tm, :] * cw_ref[0:1, :]
    for k in range(1, conv_w):
        y += g_sc[base + k:base + k + tm, :] * cw_ref[k:k + 1, :]
    y += cb_ref[...]
    conv_ref[0] = _layernorm_silu(y, lg_ref[...], lb_ref[...]).astype(conv_ref.dtype)

    @pl.when(ti == pl.num_programs(1) - 1)
    def _():
        tail_ref[0] = g_sc[halo + tm - (conv_w - 1):halo + tm, :]

    g_sc[0:halo, :] = g_sc[tm:tm + halo, :]


def _mixer_prompt(z3, g_v, w_s, b_s_t, conv_w, conv_b, ln_g, ln_b, col0, tm):
    b, t, _ = z3.shape
    n_heads_b = w_s.shape[0]
    d_b = n_heads_b * HEAD_DIM
    d_c = conv_w.shape[1]
    kw = conv_w.shape[0]
    halo = 32
    zcol = lambda c: pl.BlockSpec((1, tm, d_b), lambda bi, ti: (bi, ti, c))
    const = lambda shape: pl.BlockSpec(shape, lambda bi, ti: (0,) * len(shape))
    return pl.pallas_call(
        functools.partial(_mixer_kernel, tm=tm, n_heads_b=n_heads_b, conv_w=kw, halo=halo),
        grid=(b, t // tm),
        in_specs=[zcol(col0), zcol(col0 + 1), zcol(col0 + 2), zcol(col0 + 3),
                  const((1, d_b)), const((n_heads_b, CHUNK, CHUNK)), const((CHUNK, n_heads_b)),
                  const((kw, d_c)), const((1, d_c)), const((1, d_c)), const((1, d_c))],
        out_specs=[
            pl.BlockSpec((1, tm, d_b), lambda bi, ti: (bi, ti, 0)),
            pl.BlockSpec((1, tm, d_c), lambda bi, ti: (bi, ti, 0)),
            pl.BlockSpec((1, kw - 1, d_c), lambda bi, ti: (bi, 0, 0)),
        ],
        out_shape=[
            jax.ShapeDtypeStruct((b, t, d_b), BF16),
            jax.ShapeDtypeStruct((b, t, d_c), BF16),
            jax.ShapeDtypeStruct((b, kw - 1, d_c), F32),
        ],
        scratch_shapes=[pltpu.VMEM((halo + tm, d_c), F32)],
        compiler_params=_params(("arbitrary", "arbitrary"), 32),
        name="mixer_prompt",
    )(z3, z3, z3, z3, g_v, w_s, b_s_t, conv_w, conv_b, ln_g, ln_b)


def _mixer_sample_kernel(zs_ref, bf_ref, gv_ref, w0_ref, b0_ref, st_ref, cw_ref, cb_ref, lg_ref, lb_ref,
                         sgu_ref, conv_ref, state_ref, vn_ref, logf_ref, y_sc, *, cols, d_b, conv_w):
    c_ub, c_f = cols
    n = zs_ref.shape[0]
    u = _gelu(zs_ref[:, c_ub:c_ub + d_b])
    vn = _rms(_gelu(zs_ref[:, c_ub + d_b:c_ub + 2 * d_b]), gv_ref[...])
    vn_ref[...] = vn
    sgu_ref[...] = (u * (w0_ref[...] * vn + b0_ref[...])).astype(sgu_ref.dtype)
    glu = zs_ref[:, c_ub + 2 * d_b:c_ub + 3 * d_b] * _sigmoid(zs_ref[:, c_ub + 3 * d_b:c_ub + 4 * d_b])
    kw = conv_w - 1
    for bi in range(n):
        g_new = glu[bi:bi + 1, :]
        y_sc[bi:bi + 1, :] = (jnp.sum(st_ref[bi] * cw_ref[0:kw, :], axis=0, keepdims=True)
                              + g_new * cw_ref[kw:kw + 1, :])
        state_ref[bi, 0:kw - 1, :] = st_ref[bi, 1:kw, :]
        state_ref[bi, kw - 1:kw, :] = g_new
    conv_ref[...] = _layernorm_silu(y_sc[...] + cb_ref[...], lg_ref[...], lb_ref[...]).astype(conv_ref.dtype)
    logf_ref[...] = _log_sigmoid(zs_ref[:, c_f:c_f + 128] + bf_ref[...])


def _mixer_sample(zs, bf_pad, g_v, w0_row, b0_row, state, conv_w, conv_b, ln_g, ln_b, c_ub, c_f):
    n = zs.shape[0]
    d_b = g_v.shape[1]
    kw, d_c = conv_w.shape
    full = lambda shape: pl.BlockSpec(shape, lambda i: (0,) * len(shape))
    args = (zs, bf_pad, g_v, w0_row, b0_row, state, conv_w, conv_b, ln_g, ln_b)
    out_shape = [
        jax.ShapeDtypeStruct((n, d_b), BF16),
        jax.ShapeDtypeStruct((n, d_c), BF16),
        jax.ShapeDtypeStruct((n, kw - 1, d_c), F32),
        jax.ShapeDtypeStruct((n, d_b), F32),
        jax.ShapeDtypeStruct((n, 128), F32),
    ]
    return pl.pallas_call(
        functools.partial(_mixer_sample_kernel, cols=(c_ub, c_f), d_b=d_b, conv_w=kw),
        grid=(1,),
        in_specs=[full(a.shape) for a in args],
        out_specs=[full(s.shape) for s in out_shape],
        out_shape=out_shape,
        scratch_shapes=[pltpu.VMEM((n, d_c), F32)],
        compiler_params=_params(("arbitrary",), 32),
        name="mixer_sample",
    )(*args)


def _decode_attn_kernel(pt_ref, zs_ref, cn_ref, k_ref, v_ref, lf_ref, o_ref,
                        q_sc, m_sc, l_sc, acc_sc, tail_sc, *, n_heads, scale):
    p = pl.program_id(1)
    d_a = n_heads * HEAD_DIM
    head_of_col = lax.broadcasted_iota(jnp.int32, (n_heads, d_a), 1) // HEAD_DIM
    own = lax.broadcasted_iota(jnp.int32, (n_heads, d_a), 0) == head_of_col
    widen = lambda x: jnp.concatenate([x] * n_heads, axis=1)

    @pl.when(p == 0)
    def _():
        q_sc[...] = jnp.where(own, jnp.broadcast_to(zs_ref[0, :, 0:d_a], (n_heads, d_a)), 0.0)
        m_sc[...] = jnp.full_like(m_sc, -jnp.inf)
        l_sc[...] = jnp.zeros_like(l_sc)
        acc_sc[...] = jnp.zeros_like(acc_sc)
        tail_sc[...] = jnp.zeros_like(tail_sc)

    lf = lf_ref[0, 0]
    r_i = lax.broadcasted_iota(jnp.int32, (CHUNK, CHUNK), 0)
    c_i = lax.broadcasted_iota(jnp.int32, (CHUNK, CHUNK), 1)
    later = jnp.where(r_i > c_i, 1.0, 0.0).astype(BF16)
    hi, mid, lo = _split3(lf)
    suffix = (_dot(hi, later) + _dot(mid, later)) + _dot(lo, later) + tail_sc[...]
    s = _dot_nt(q_sc[...].astype(BF16), k_ref[0, 0].astype(BF16)) * scale + suffix + cn_ref[0]
    m_prev = m_sc[...]
    m_new = jnp.maximum(m_prev, jnp.max(s, axis=1, keepdims=True))
    alpha = jnp.exp(m_prev - m_new)
    pr = jnp.exp(s - m_new)
    l_sc[...] = alpha * l_sc[...] + jnp.sum(pr, axis=1, keepdims=True)
    acc_sc[...] = widen(alpha) * acc_sc[...] + _dot(pr.astype(BF16), v_ref[0, 0].astype(BF16))
    m_sc[...] = m_new
    tail_sc[...] += jnp.sum(lf, axis=1, keepdims=True)

    @pl.when(p == pl.num_programs(1) - 1)
    def _():
        cn = cn_ref[0]
        s_new = jnp.sum(q_sc[...] * zs_ref[0, :, d_a:2 * d_a], axis=1, keepdims=True) * scale + cn - cn
        m_f = jnp.maximum(m_sc[...], s_new)
        a_f = jnp.exp(m_sc[...] - m_f)
        p_new = jnp.exp(s_new - m_f)
        l_f = a_f * l_sc[...] + p_new
        out = (widen(a_f) * acc_sc[...] + widen(p_new) * zs_ref[0, :, 2 * d_a:3 * d_a]) / widen(l_f)
        o_ref[0] = jnp.sum(jnp.where(own, out, 0.0), axis=0, keepdims=True)


def _decode_attn(page_table, zs3, cn, cache_k, cache_v, cache_lft, layer, n_heads):
    bsz, n_pages = page_table.shape
    page = cache_k.shape[2]
    d_a = n_heads * HEAD_DIM
    ncols = zs3.shape[2]
    rev = lambda b, p, pt: pt[b, n_pages - 1 - p]
    grid_spec = pltpu.PrefetchScalarGridSpec(
        num_scalar_prefetch=1,
        grid=(bsz, n_pages),
        in_specs=[
            pl.BlockSpec((1, 1, ncols), lambda b, p, pt: (b, 0, 0)),
            pl.BlockSpec((1, n_heads, 1), lambda b, p, pt: (b, 0, 0)),
            pl.BlockSpec((1, 1, page, d_a), lambda b, p, pt: (rev(b, p, pt), layer, 0, 0)),
            pl.BlockSpec((1, 1, page, d_a), lambda b, p, pt: (rev(b, p, pt), layer, 0, 0)),
            pl.BlockSpec((1, 1, n_heads, page), lambda b, p, pt: (rev(b, p, pt), layer, 0, 0)),
        ],
        out_specs=pl.BlockSpec((1, 1, d_a), lambda b, p, pt: (b, 0, 0)),
        scratch_shapes=[
            pltpu.VMEM((n_heads, d_a), F32),
            pltpu.VMEM((n_heads, 128), F32),
            pltpu.VMEM((n_heads, 128), F32),
            pltpu.VMEM((n_heads, d_a), F32),
            pltpu.VMEM((n_heads, 128), F32),
        ],
    )
    return pl.pallas_call(
        functools.partial(_decode_attn_kernel, n_heads=n_heads, scale=HEAD_DIM ** -0.5),
        grid_spec=grid_spec,
        out_shape=jax.ShapeDtypeStruct((bsz, 1, d_a), F32),
        compiler_params=_params(("arbitrary", "arbitrary"), 32),
        name="fox_attn_sample",
    )(page_table, zs3, cn, cache_k, cache_v, cache_lft)


def kernel(x_prompt, x_sample, cache_k, cache_v, cache_logf, state_conv, page_table, w_in, b_f, g_v, w_s, b_s,
           conv_w, conv_b, ln_c_g, ln_c_b, w_o, g_pre_mix, g_post_mix, g_pre_mlp, g_post_mlp, w_up, w_down):
    bp, tp, d = x_prompt.shape
    bs, ds, _ = x_sample.shape
    assert ds == 1, "the sample path handles exactly one new token per sequence"
    depth = w_in.shape[0]
    n_pool, _, page, n_heads, _ = cache_k.shape
    d_a = n_heads * HEAD_DIM
    n_heads_b = w_s.shape[1]
    d_b = n_heads_b * HEAD_DIM
    d_c = conv_w.shape[2]
    kw = conv_w.shape[1]
    off_f = 3 * d_a
    assert w_in.shape[2] == off_f + n_heads + 2 * d_b + 2 * d_c and d_b == d_c and d_a == 2 * d_b
    mp = bp * tp

    n_main = off_f + 2 * d_b + 2 * d_c
    n_proj = -(-(n_main + 128) // PROJ_TN) * PROJ_TN
    w_proj = jnp.concatenate(
        [w_in[:, :, :off_f], w_in[:, :, off_f + n_heads:], w_in[:, :, off_f:off_f + n_heads],
         jnp.zeros((depth, d, n_proj - n_main - n_heads), w_in.dtype)], axis=2).astype(BF16)
    w_o_b, w_up_b, w_down_b = w_o.astype(BF16), w_up.astype(BF16), w_down.astype(BF16)
    bf_pad = jnp.pad(b_f, ((0, 0), (0, 128 - n_heads)))
    f_blk = n_main // 128
    ub_blk = off_f // d_b
    row2 = lambda a, l: a[l][None, :]

    cache_k4 = cache_k.reshape(n_pool, depth, page, d_a)
    cache_v4 = cache_v.reshape(n_pool, depth, page, d_a)
    cache_lft = jnp.swapaxes(cache_logf, 2, 3)

    xp = x_prompt.reshape(mp, d)
    xs = x_sample.reshape(bs, d)
    xpn = _norm_bf16(xp, row2(g_pre_mix, 0), 512)
    xsn = _norm_bf16(xs, row2(g_pre_mix, 0), bs)
    kp_l, vp_l, fp_l, cp_l, ks_l, vs_l, fs_l, cs_l, us_l = ([] for _ in range(9))
    for l in range(depth):
        g_next = row2(g_pre_mix, l + 1) if l + 1 < depth else None
        z = _matmul(_mm_kernel, xpn, w_proj[l], 1024, PROJ_TN, F32, "proj_in")
        z3 = z.reshape(bp, tp, n_proj)
        logf, cq, ct = _fox_prefix(z3, bf_pad[l][None, :], n_heads, f_blk)
        attn = _fox_attn_prompt(z3, cq, ct, n_heads, 512, 512)
        sgu, conv, conv_tail = _mixer_prompt(z3, row2(g_v, l), w_s[l], b_s[l].T, conv_w[l], row2(conv_b, l),
                                             row2(ln_c_g, l), row2(ln_c_b, l), ub_blk, 256)
        x1, x1n = _wo_block(attn.reshape(mp, d_a), sgu.reshape(mp, d_b), conv.reshape(mp, d_c), w_o_b[l], xp,
                            row2(g_post_mix, l), row2(g_pre_mlp, l), 256)
        hid = _matmul(_mm_relu2_kernel, x1n, w_up_b[l], 512, 1024, BF16, "mlp_up")
        xp, xpn = _down_block(hid, w_down_b[l], x1, row2(g_post_mlp, l), g_next, 512, 1024)
        kp_l.append(z3[:, :, d_a:2 * d_a])
        vp_l.append(z3[:, :, 2 * d_a:3 * d_a])
        fp_l.append(logf)
        cp_l.append(conv_tail)
        zs = _matmul(_mm_kernel, xsn, w_proj[l], bs, PROJ_TN, F32, "proj_in_sample")
        w0_row = jnp.repeat(w_s[l][:, 0, 0], HEAD_DIM)[None, :]
        b0_row = jnp.repeat(b_s[l][:, 0], HEAD_DIM)[None, :]
        sgu_s, conv_s, state_s, vn_s, logf_s = _mixer_sample(
            zs, bf_pad[l][None, :], row2(g_v, l), w0_row, b0_row, state_conv[l], conv_w[l], row2(conv_b, l),
            row2(ln_c_g, l), row2(ln_c_b, l), off_f, n_main)
        logf_s = logf_s[:, :n_heads]
        attn_s = _decode_attn(page_table, zs.reshape(bs, 1, n_proj), logf_s[:, :, None], cache_k4, cache_v4,
                              cache_lft, l, n_heads)
        x1s, x1sn = _wo_block(attn_s.reshape(bs, d_a).astype(BF16), sgu_s, conv_s, w_o_b[l], xs,
                              row2(g_post_mix, l), row2(g_pre_mlp, l), bs)
        hid_s = _matmul(_mm_relu2_kernel, x1sn, w_up_b[l], bs, 1024, BF16, "mlp_up_sample")
        xs, xsn = _down_block(hid_s, w_down_b[l], x1s, row2(g_post_mlp, l), g_next, bs, 1024)
        ks_l.append(zs[:, d_a:2 * d_a])
        vs_l.append(zs[:, 2 * d_a:3 * d_a])
        fs_l.append(logf_s)
        cs_l.append(state_s)
        us_l.append(vn_s)

    npp = tp // page
    paged = lambda parts, tail: jnp.stack(parts, axis=1).reshape((bp, depth, npp, page) + tail).swapaxes(1, 2)
    return (xp.reshape(bp, tp, d), xs.reshape(bs, ds, d),
            paged(kp_l, (n_heads, HEAD_DIM)), paged(vp_l, (n_heads, HEAD_DIM)), paged(fp_l, (n_heads,)),
            jnp.stack(cp_l, axis=0),
            jnp.stack(ks_l, axis=1).reshape(bs, depth, ds, n_heads, HEAD_DIM),
            jnp.stack(vs_l, axis=1).reshape(bs, depth, ds, n_heads, HEAD_DIM),
            jnp.stack(fs_l, axis=1).reshape(bs, depth, ds, n_heads),
            jnp.stack(cs_l, axis=0),
            jnp.stack(us_l, axis=0).reshape(depth, bs, ds, d_b))
```

```python
import functools

import numpy as np
import jax
import jax.numpy as jnp
from jax import lax
from jax.experimental import pallas as pl
from jax.experimental.pallas import tpu as pltpu

F32 = jnp.float32
BF16 = jnp.bfloat16

HEAD_DIM = 128
CHUNK = 128
EPS = 1e-6
MIB = 1024 * 1024
GELU_C = float(np.sqrt(2.0 / np.pi))
LOG2E = float(np.log2(np.e))

REST_TN = 768
PAGES_PER_STEP = 8


def _params(semantics, vmem_mib):
    return pltpu.CompilerParams(dimension_semantics=semantics, vmem_limit_bytes=vmem_mib * MIB)


def _rms(x, g):
    return x * lax.rsqrt(jnp.mean(x * x, axis=-1, keepdims=True) + EPS) * g


def _gelu(x):
    return x * (0.5 * (1.0 + jnp.tanh(GELU_C * (x + 0.044715 * (x * x * x)))))


def _sigmoid(x):
    return 1.0 / (1.0 + jnp.exp(-x))


def _log_sigmoid(x):
    return jnp.minimum(x, 0.0) - jnp.log1p(jnp.exp(-jnp.abs(x)))


def _split3(x):
    hi = x.astype(BF16)
    r = x - hi.astype(F32)
    mid = r.astype(BF16)
    lo = (r - mid.astype(F32)).astype(BF16)
    return hi, mid, lo


def _dot(a, b):
    return jnp.dot(a, b, preferred_element_type=F32)


def _dot_nt(a, b):
    return lax.dot_general(a, b, (((1,), (1,)), ((), ())), preferred_element_type=F32)


def _dot3(x, w):
    hi, mid, lo = _split3(x)
    return (_dot(hi, w) + _dot(mid, w)) + _dot(lo, w)


def _dot3_left(w, x):
    hi, mid, lo = _split3(x)
    return (_dot(w, hi) + _dot(w, mid)) + _dot(w, lo)


def _ones_where(cond):
    return jnp.where(cond, 1.0, 0.0).astype(BF16)


def _norm_kernel(x_ref, g_ref, o_ref):
    o_ref[...] = _rms(x_ref[...], g_ref[...]).astype(o_ref.dtype)


def _norm_bf16(x, g, tm):
    m, d = x.shape
    return pl.pallas_call(
        _norm_kernel,
        grid=(m // tm,),
        in_specs=[pl.BlockSpec((tm, d), lambda i: (i, 0)), pl.BlockSpec((1, d), lambda i: (0, 0))],
        out_specs=pl.BlockSpec((tm, d), lambda i: (i, 0)),
        out_shape=jax.ShapeDtypeStruct((m, d), BF16),
        compiler_params=_params(("arbitrary",), 32),
        name="pre_norm",
    )(x, g)


def _mm_kernel(x_ref, w_ref, o_ref):
    o_ref[...] = _dot(x_ref[...], w_ref[...]).astype(o_ref.dtype)


def _mm_relu2_kernel(x_ref, w_ref, o_ref):
    h = jnp.maximum(_dot(x_ref[...], w_ref[...]), 0.0)
    o_ref[...] = (h * h).astype(o_ref.dtype)


def _matmul(body, x, w, tm, tn, out_dtype, name):
    m, k = x.shape
    n = w.shape[1]
    return pl.pallas_call(
        body,
        grid=(m // tm, n // tn),
        in_specs=[pl.BlockSpec((tm, k), lambda i, j: (i, 0)), pl.BlockSpec((k, tn), lambda i, j: (0, j))],
        out_specs=pl.BlockSpec((tm, tn), lambda i, j: (i, j)),
        out_shape=jax.ShapeDtypeStruct((m, n), out_dtype),
        compiler_params=_params(("arbitrary", "arbitrary"), 48),
        name=name,
    )(x, w)


def _proj_qkv_kernel(x_ref, w_ref, q_ref, kb_ref, vb_ref, kp_ref, vp_ref, *, tm, n_heads, page, q_scale):
    j = pl.program_id(1)
    acc = _dot(x_ref[...], w_ref[...])

    def to_pages(dst_ref):
        for pg in range(tm // page):
            for h in range(n_heads):
                dst_ref[0, pg, pl.ds(h, page, stride=n_heads), :] = (
                    acc[pg * page:(pg + 1) * page, h * HEAD_DIM:(h + 1) * HEAD_DIM])

    @pl.when(j == 0)
    def _():
        q_ref[...] = (acc * q_scale).astype(q_ref.dtype)

    @pl.when(j == 1)
    def _():
        kb_ref[...] = acc.astype(kb_ref.dtype)
        to_pages(kp_ref)

    @pl.when(j == 2)
    def _():
        vb_ref[...] = acc.astype(vb_ref.dtype)
        to_pages(vp_ref)


def _proj_qkv(xn, w_qkv, bsz, t, n_heads, page, tm):
    m, kdim = xn.shape
    d_a = n_heads * HEAD_DIM
    npp = t // page
    tiles_per_seq = t // tm
    row = pl.BlockSpec((tm, d_a), lambda i, j: (i, 0))
    pages = pl.BlockSpec((1, tm // page, page * n_heads, HEAD_DIM),
                         lambda i, j: (i // tiles_per_seq, i % tiles_per_seq, 0, 0))
    page_shape = jax.ShapeDtypeStruct((bsz, npp, page * n_heads, HEAD_DIM), F32)
    return pl.pallas_call(
        functools.partial(_proj_qkv_kernel, tm=tm, n_heads=n_heads, page=page, q_scale=HEAD_DIM ** -0.5 * LOG2E),
        grid=(m // tm, 3),
        in_specs=[pl.BlockSpec((tm, kdim), lambda i, j: (i, 0)), pl.BlockSpec((kdim, d_a), lambda i, j: (0, j))],
        out_specs=[row, row, row, pages, pages],
        out_shape=[jax.ShapeDtypeStruct((m, d_a), BF16)] * 3 + [page_shape, page_shape],
        compiler_params=_params(("arbitrary", "arbitrary"), 48),
        name="proj_qkv",
    )(xn, w_qkv)


def _wo_kernel(a_ref, s_ref, c_ref, w_ref, x_ref, g1_ref, g2_ref, x1_ref, xn_ref, *, d_a, d_b):
    mixed = _dot(a_ref[...], w_ref[0:d_a, :])
    mixed += _dot(s_ref[...], w_ref[d_a:d_a + d_b, :])
    mixed += _dot(c_ref[...], w_ref[d_a + d_b:, :])
    x1 = x_ref[...] + _rms(mixed, g1_ref[...])
    x1_ref[...] = x1
    xn_ref[...] = _rms(x1, g2_ref[...]).astype(xn_ref.dtype)


def _wo_block(attn, sgu, conv, w_o, x, g_post, g_pre_mlp, tm):
    m, d = x.shape
    d_a, d_b, d_c = attn.shape[1], sgu.shape[1], conv.shape[1]
    row = lambda width: pl.BlockSpec((tm, width), lambda i: (i, 0))
    const = lambda shape: pl.BlockSpec(shape, lambda i: (0, 0))
    return pl.pallas_call(
        functools.partial(_wo_kernel, d_a=d_a, d_b=d_b),
        grid=(m // tm,),
        in_specs=[row(d_a), row(d_b), row(d_c), const((d, d)), row(d), const((1, d)), const((1, d))],
        out_specs=[row(d), row(d)],
        out_shape=[jax.ShapeDtypeStruct((m, d), F32), jax.ShapeDtypeStruct((m, d), BF16)],
        compiler_params=_params(("arbitrary",), 48),
        name="wo_norm_residual",
    )(attn, sgu, conv, w_o, x, g_post, g_pre_mlp)


def _down_kernel(h_ref, w_ref, x1_ref, g_ref, *rest, emit_next):
    if emit_next:
        gn_ref, x2_ref, xn_ref, acc_ref = rest
    else:
        x2_ref, acc_ref = rest
    k = pl.program_id(1)

    @pl.when(k == 0)
    def _():
        acc_ref[...] = jnp.zeros_like(acc_ref)

    acc_ref[...] += _dot(h_ref[...], w_ref[...])

    @pl.when(k == pl.num_programs(1) - 1)
    def _():
        x2 = x1_ref[...] + _rms(acc_ref[...], g_ref[...])
        x2_ref[...] = x2
        if emit_next:
            xn_ref[...] = _rms(x2, gn_ref[...]).astype(xn_ref.dtype)


def _down_block(hid, w_down, x1, g_post, g_next, tm, tk):
    m, d = x1.shape
    kdim = hid.shape[1]
    emit_next = g_next is not None
    row = pl.BlockSpec((tm, d), lambda i, k: (i, 0))
    const = pl.BlockSpec((1, d), lambda i, k: (0, 0))
    in_specs = [pl.BlockSpec((tm, tk), lambda i, k: (i, k)), pl.BlockSpec((tk, d), lambda i, k: (k, 0)), row, const]
    args = [hid, w_down, x1, g_post]
    out_specs = [row]
    out_shape = [jax.ShapeDtypeStruct((m, d), F32)]
    if emit_next:
        in_specs.append(const)
        args.append(g_next)
        out_specs.append(row)
        out_shape.append(jax.ShapeDtypeStruct((m, d), BF16))
    out = pl.pallas_call(
        functools.partial(_down_kernel, emit_next=emit_next),
        grid=(m // tm, kdim // tk),
        in_specs=in_specs,
        out_specs=out_specs,
        out_shape=out_shape,
        scratch_shapes=[pltpu.VMEM((tm, d), F32)],
        compiler_params=_params(("arbitrary", "arbitrary"), 48),
        name="down_norm_residual",
    )(*args)
    return (out[0], out[1]) if emit_next else (out[0], None)


def _prefix_kernel(zf_ref, bf_ref, logf_ref, cq_ref, ct_ref, c_sc, *, t, n_heads):
    lf = _log_sigmoid(zf_ref[0] + bf_ref[...])
    logf_ref[0] = lf[:, 0:n_heads]
    r_i = lax.broadcasted_iota(jnp.int32, (CHUNK, CHUNK), 0)
    c_i = lax.broadcasted_iota(jnp.int32, (CHUNK, CHUNK), 1)
    tri = _ones_where(r_i >= c_i)
    carry = jnp.zeros((1, 128), F32)
    for blk in range(t // CHUNK):
        cb = _dot3_left(tri, lf[blk * CHUNK:(blk + 1) * CHUNK, :]) + carry
        c_sc[blk * CHUNK:(blk + 1) * CHUNK, :] = cb * LOG2E
        carry = cb[CHUNK - 1:CHUNK, :]
    c = c_sc[...]
    ct_ref[0] = c.T[0:n_heads, :]
    for h in range(n_heads):
        cq_ref[0, h] = c[:, h:h + 1]


def _fox_prefix(z3, bf_pad, n_heads, f_blk):
    b, t, _ = z3.shape
    return pl.pallas_call(
        functools.partial(_prefix_kernel, t=t, n_heads=n_heads),
        grid=(b,),
        in_specs=[pl.BlockSpec((1, t, 128), lambda i: (i, 0, f_blk)), pl.BlockSpec((1, 128), lambda i: (0, 0))],
        out_specs=[
            pl.BlockSpec((1, t, n_heads), lambda i: (i, 0, 0)),
            pl.BlockSpec((1, n_heads, t, 1), lambda i: (i, 0, 0, 0)),
            pl.BlockSpec((1, n_heads, t), lambda i: (i, 0, 0)),
        ],
        out_shape=[
            jax.ShapeDtypeStruct((b, t, n_heads), F32),
            jax.ShapeDtypeStruct((b, n_heads, t, 1), F32),
            jax.ShapeDtypeStruct((b, n_heads, t), F32),
        ],
        scratch_shapes=[pltpu.VMEM((t, 128), F32)],
        compiler_params=_params(("arbitrary",), 48),
        name="fox_prefix",
    )(z3, bf_pad)


def _fox_attn_kernel(qi_ref, ki_ref, q_ref, k_ref, v_ref, cq_ref, ck_ref, o_ref, m_sc, l_sc, acc_sc, cq_sc,
                     *, tq, n_heads):
    step = pl.program_id(1)
    qi = qi_ref[step]
    ki = ki_ref[step]
    n_sub = tq // 128

    @pl.when(ki == 0)
    def _():
        m_sc[...] = jnp.full_like(m_sc, -jnp.inf)
        l_sc[...] = jnp.zeros_like(l_sc)
        acc_sc[...] = jnp.zeros_like(acc_sc)
        for h in range(n_heads):
            cq_sc[h] = jnp.broadcast_to(cq_ref[0, h], (tq, 128))

    def block(diagonal):
        if diagonal:
            keep = (lax.broadcasted_iota(jnp.int32, (tq, tq), 0) >= lax.broadcasted_iota(jnp.int32, (tq, tq), 1))
        for h in range(n_heads):
            hs = slice(h * HEAD_DIM, (h + 1) * HEAD_DIM)
            s = _dot_nt(q_ref[0, :, hs], k_ref[0, :, hs]) - ck_ref[0, h:h + 1, :]
            if diagonal:
                s = jnp.where(keep, s, -jnp.inf)
            subs = [s[:, j * 128:(j + 1) * 128] for j in range(n_sub)]
            mc = subs[0]
            for x in subs[1:]:
                mc = jnp.maximum(mc, x)
            cq = cq_sc[h]
            m_prev = m_sc[h]
            m_new = jnp.maximum(m_prev, jnp.max(mc, axis=1, keepdims=True) + cq)
            alpha = jnp.exp2(m_prev - m_new)
            shift = m_new - cq
            ps = [jnp.exp2(x - shift) for x in subs]
            lsum = ps[0]
            for x in ps[1:]:
                lsum = lsum + x
            l_sc[h] = alpha * l_sc[h] + jnp.sum(lsum, axis=1, keepdims=True)
            p = jnp.concatenate(ps, axis=1).astype(BF16)
            acc_sc[:, hs] = alpha * acc_sc[:, hs] + _dot(p, v_ref[0, :, hs])
            m_sc[h] = m_new

    @pl.when(ki < qi)
    def _():
        block(False)

    @pl.when(ki == qi)
    def _():
        block(True)
        for h in range(n_heads):
            hs = slice(h * HEAD_DIM, (h + 1) * HEAD_DIM)
            o_ref[0, :, hs] = (acc_sc[:, hs] / l_sc[h]).astype(o_ref.dtype)


def _fox_attn_prompt(q, k, v, cq, ct, n_heads, tq):
    b, t, d_a = q.shape
    nq = t // tq
    pairs = [(i, j) for i in range(nq) for j in range(i + 1)]
    qi_tab = jnp.asarray([pr[0] for pr in pairs], jnp.int32)
    ki_tab = jnp.asarray([pr[1] for pr in pairs], jnp.int32)
    q_blk = pl.BlockSpec((1, tq, d_a), lambda bi, s, qt, kt: (bi, qt[s], 0))
    kv_blk = pl.BlockSpec((1, tq, d_a), lambda bi, s, qt, kt: (bi, kt[s], 0))
    grid_spec = pltpu.PrefetchScalarGridSpec(
        num_scalar_prefetch=2,
        grid=(b, len(pairs)),
        in_specs=[
            q_blk, kv_blk, kv_blk,
            pl.BlockSpec((1, n_heads, tq, 1), lambda bi, s, qt, kt: (bi, 0, qt[s], 0)),
            pl.BlockSpec((1, n_heads, tq), lambda bi, s, qt, kt: (bi, 0, kt[s])),
        ],
        out_specs=q_blk,
        scratch_shapes=[
            pltpu.VMEM((n_heads, tq, 128), F32),
            pltpu.VMEM((n_heads, tq, 128), F32),
            pltpu.VMEM((tq, d_a), F32),
            pltpu.VMEM((n_heads, tq, 128), F32),
        ],
    )
    return pl.pallas_call(
        functools.partial(_fox_attn_kernel, tq=tq, n_heads=n_heads),
        grid_spec=grid_spec,
        out_shape=jax.ShapeDtypeStruct((b, t, d_a), BF16),
        compiler_params=_params(("arbitrary", "arbitrary"), 48),
        name="fox_attn_prompt",
    )(qi_tab, ki_tab, q, k, v, cq, ct)


def _layernorm_silu(y, g, b):
    mu = jnp.mean(y, axis=-1, keepdims=True)
    yc = y - mu
    yn = yc * lax.rsqrt(jnp.mean(yc * yc, axis=-1, keepdims=True) + EPS) * g + b
    return yn * _sigmoid(yn)


def _mixer_kernel(ub_ref, vb_ref, ac_ref, gc_ref, gv_ref, ws_ref, bst_ref, cw_ref, cb_ref, lg_ref, lb_ref,
                  sgu_ref, conv_ref, tail_ref, g_sc, *, tm, n_heads_b, conv_w, halo):
    ti = pl.program_id(1)

    u = _gelu(ub_ref[0])
    vn = _rms(_gelu(vb_ref[0]), gv_ref[...])
    r_i = lax.broadcasted_iota(jnp.int32, (CHUNK, CHUNK), 0)
    c_i = lax.broadcasted_iota(jnp.int32, (CHUNK, CHUNK), 1)
    for h in range(n_heads_b):
        hs = slice(h * HEAD_DIM, (h + 1) * HEAD_DIM)
        w_h = jnp.where(r_i >= c_i, ws_ref[h], 0.0).astype(BF16)
        bias_h = bst_ref[:, h:h + 1]
        for c in range(tm // CHUNK):
            rs = slice(c * CHUNK, (c + 1) * CHUNK)
            mix = _dot(w_h, vn[rs, hs].astype(BF16)) + bias_h
            sgu_ref[0, rs, hs] = (u[rs, hs] * mix).astype(sgu_ref.dtype)

    @pl.when(ti == 0)
    def _():
        g_sc[0:halo, :] = jnp.zeros((halo, g_sc.shape[1]), F32)

    g_sc[halo:halo + tm, :] = ac_ref[0] * _sigmoid(gc_ref[0])
    base = halo - (conv_w - 1)
    y = g_sc[base:base + tm, :] * cw_ref[0:1, :]
    for k in range(1, conv_w):
        y += g_sc[base + k:base + k + tm, :] * cw_ref[k:k + 1, :]
    y += cb_ref[...]
    conv_ref[0] = _layernorm_silu(y, lg_ref[...], lb_ref[...]).astype(conv_ref.dtype)

    @pl.when(ti == pl.num_programs(1) - 1)
    def _():
        tail_ref[0] = g_sc[halo + tm - (conv_w - 1):halo + tm, :]

    g_sc[0:halo, :] = g_sc[tm:tm + halo, :]


def _mixer_prompt(z3, g_v, w_s, b_s_t, conv_w, conv_b, ln_g, ln_b, tm):
    b, t, _ = z3.shape
    n_heads_b = w_s.shape[0]
    d_b = n_heads_b * HEAD_DIM
    d_c = conv_w.shape[1]
    kw = conv_w.shape[0]
    halo = 32
    zcol = lambda c: pl.BlockSpec((1, tm, d_b), lambda bi, ti: (bi, ti, c))
    const = lambda shape: pl.BlockSpec(shape, lambda bi, ti: (0,) * len(shape))
    return pl.pallas_call(
        functools.partial(_mixer_kernel, tm=tm, n_heads_b=n_heads_b, conv_w=kw, halo=halo),
        grid=(b, t // tm),
        in_specs=[zcol(0), zcol(1), zcol(2), zcol(3),
                  const((1, d_b)), const((n_heads_b, CHUNK, CHUNK)), const((CHUNK, n_heads_b)),
                  const((kw, d_c)), const((1, d_c)), const((1, d_c)), const((1, d_c))],
        out_specs=[
            pl.BlockSpec((1, tm, d_b), lambda bi, ti: (bi, ti, 0)),
            pl.BlockSpec((1, tm, d_c), lambda bi, ti: (bi, ti, 0)),
            pl.BlockSpec((1, kw - 1, d_c), lambda bi, ti: (bi, 0, 0)),
        ],
        out_shape=[
            jax.ShapeDtypeStruct((b, t, d_b), BF16),
            jax.ShapeDtypeStruct((b, t, d_c), BF16),
            jax.ShapeDtypeStruct((b, kw - 1, d_c), F32),
        ],
        scratch_shapes=[pltpu.VMEM((halo + tm, d_c), F32)],
        compiler_params=_params(("arbitrary", "arbitrary"), 32),
        name="mixer_prompt",
    )(z3, z3, z3, z3, g_v, w_s, b_s_t, conv_w, conv_b, ln_g, ln_b)


def _mixer_sample_kernel(zs_ref, bf_ref, gv_ref, w0_ref, b0_ref, st_ref, cw_ref, cb_ref, lg_ref, lb_ref,
                         sgu_ref, conv_ref, state_ref, vn_ref, logf_ref, y_sc, *, d_b, conv_w):
    n = zs_ref.shape[0]
    u = _gelu(zs_ref[:, 0:d_b])
    vn = _rms(_gelu(zs_ref[:, d_b:2 * d_b]), gv_ref[...])
    vn_ref[...] = vn
    sgu_ref[...] = (u * (w0_ref[...] * vn + b0_ref[...])).astype(sgu_ref.dtype)
    glu = zs_ref[:, 2 * d_b:3 * d_b] * _sigmoid(zs_ref[:, 3 * d_b:4 * d_b])
    kw = conv_w - 1
    for bi in range(n):
        g_new = glu[bi:bi + 1, :]
        y_sc[bi:bi + 1, :] = (jnp.sum(st_ref[bi] * cw_ref[0:kw, :], axis=0, keepdims=True)
                              + g_new * cw_ref[kw:kw + 1, :])
        state_ref[bi, 0:kw - 1, :] = st_ref[bi, 1:kw, :]
        state_ref[bi, kw - 1:kw, :] = g_new
    conv_ref[...] = _layernorm_silu(y_sc[...] + cb_ref[...], lg_ref[...], lb_ref[...]).astype(conv_ref.dtype)
    logf_ref[...] = _log_sigmoid(zs_ref[:, 4 * d_b:4 * d_b + 128] + bf_ref[...])


def _mixer_sample(zs, bf_pad, g_v, w0_row, b0_row, state, conv_w, conv_b, ln_g, ln_b):
    n = zs.shape[0]
    d_b = g_v.shape[1]
    kw, d_c = conv_w.shape
    full = lambda shape: pl.BlockSpec(shape, lambda i: (0,) * len(shape))
    args = (zs, bf_pad, g_v, w0_row, b0_row, state, conv_w, conv_b, ln_g, ln_b)
    out_shape = [
        jax.ShapeDtypeStruct((n, d_b), BF16),
        jax.ShapeDtypeStruct((n, d_c), BF16),
        jax.ShapeDtypeStruct((n, kw - 1, d_c), F32),
        jax.ShapeDtypeStruct((n, d_b), F32),
        jax.ShapeDtypeStruct((n, 128), F32),
    ]
    return pl.pallas_call(
        functools.partial(_mixer_sample_kernel, d_b=d_b, conv_w=kw),
        grid=(1,),
        in_specs=[full(a.shape) for a in args],
        out_specs=[full(s.shape) for s in out_shape],
        out_shape=out_shape,
        scratch_shapes=[pltpu.VMEM((n, d_c), F32)],
        compiler_params=_params(("arbitrary",), 32),
        name="mixer_sample",
    )(*args)


def _decode_attn_kernel(pt_ref, zs_ref, cn_ref, *refs, n_grp, n_heads, scale):
    k_refs, v_refs, lf_refs = refs[:n_grp], refs[n_grp:2 * n_grp], refs[2 * n_grp:3 * n_grp]
    o_ref = refs[3 * n_grp]
    cn_sc, m_sc, l_sc, acc_sc, tail_sc = refs[3 * n_grp + 1:]
    p = pl.program_id(1)
    rows_pp = k_refs[0].shape[2]
    n_flat = rows_pp // 128
    lane = lax.broadcasted_iota(jnp.int32, (n_heads, 128), 1)
    sub = lax.broadcasted_iota(jnp.int32, (n_heads, 128), 0)
    cls_mask = n_heads - 1
    own = sub == (lane & cls_mask)
    diag = sub == lane
    r_i = lax.broadcasted_iota(jnp.int32, (128, 128), 0)
    c_i = lax.broadcasted_iota(jnp.int32, (128, 128), 1)
    same = (r_i & cls_mask) == (c_i & cls_mask)

    @pl.when(p == 0)
    def _():
        spread = _ones_where((r_i < n_heads) & ((c_i & cls_mask) == r_i))
        cn_sc[...] = _dot3(jnp.broadcast_to(cn_ref[0], (n_heads, 128)), spread)
        m_sc[...] = jnp.full_like(m_sc, -jnp.inf)
        l_sc[...] = jnp.zeros_like(l_sc)
        acc_sc[...] = jnp.zeros_like(acc_sc)
        tail_sc[...] = jnp.zeros_like(tail_sc)

    lf = jnp.concatenate([lf_refs[g][0, 0] for g in range(n_grp)], axis=0)
    n_rows = n_grp * n_flat
    row_tot = _dot3(lf, _ones_where(same))
    within = _dot3(lf, _ones_where(same & (r_i > c_i)))
    rr = lax.broadcasted_iota(jnp.int32, (n_rows, n_rows), 0)
    cc = lax.broadcasted_iota(jnp.int32, (n_rows, n_rows), 1)
    suffix = within + _dot3_left(_ones_where(cc > rr), row_tot) + tail_sc[0:1, :]

    q8 = zs_ref[0, 0:n_heads, :].astype(BF16)
    scores = []
    for g in range(n_grp):
        s_t = _dot_nt(q8, k_refs[g][0, 0].astype(BF16))
        flat = [jnp.sum(jnp.where(own, s_t[:, a * 128:(a + 1) * 128], 0.0), axis=0, keepdims=True)
                for a in range(n_flat)]
        scores.append(jnp.concatenate(flat, axis=0) * scale
                      + suffix[g * n_flat:(g + 1) * n_flat, :] + cn_sc[...])
    m_step = scores[0]
    for x in scores[1:]:
        m_step = jnp.maximum(m_step, x)
    shift = 1
    while shift < n_flat:
        m_step = jnp.maximum(m_step, pltpu.roll(m_step, shift, 0))
        shift *= 2
    shift = n_heads
    while shift < 128:
        m_step = jnp.maximum(m_step, pltpu.roll(m_step, shift, 1))
        shift *= 2
    m_prev = m_sc[...]
    m_new = jnp.maximum(m_prev, m_step)
    alpha = jnp.exp(m_prev - m_new)
    probs = [jnp.exp(x - m_new) for x in scores]
    l_step = probs[0]
    for x in probs[1:]:
        l_step = l_step + x
    l_sc[...] = alpha * l_sc[...] + l_step
    pv = jnp.zeros((n_heads, HEAD_DIM), F32)
    for g in range(n_grp):
        blocks = [jnp.where(own, jnp.broadcast_to(probs[g][a:a + 1, :], (n_heads, 128)), 0.0)
                  for a in range(n_flat)]
        pv += _dot(jnp.concatenate(blocks, axis=1).astype(BF16), v_refs[g][0, 0].astype(BF16))
    alpha_col = jnp.sum(jnp.where(diag, alpha, 0.0), axis=1, keepdims=True)
    acc_sc[...] = alpha_col * acc_sc[...] + pv
    m_sc[...] = m_new
    tail_sc[...] += jnp.sum(row_tot, axis=0, keepdims=True)

    @pl.when(p == pl.num_programs(1) - 1)
    def _():
        m_col = jnp.max(jnp.where(diag, m_sc[...], -jnp.inf), axis=1, keepdims=True)
        l_row = jnp.sum(l_sc[...], axis=0, keepdims=True)
        l_col = jnp.sum(jnp.where(own, jnp.broadcast_to(l_row, (n_heads, 128)), 0.0), axis=1, keepdims=True)
        cn_col = jnp.sum(jnp.where(diag, cn_sc[...], 0.0), axis=1, keepdims=True)
        q_new = zs_ref[0, 0:n_heads, :]
        k_new = zs_ref[0, n_heads:2 * n_heads, :]
        v_new = zs_ref[0, 2 * n_heads:3 * n_heads, :]
        s_new = jnp.sum(q_new * k_new, axis=1, keepdims=True) * scale + cn_col - cn_col
        m_f = jnp.maximum(m_col, s_new)
        a_f = jnp.exp(m_col - m_f)
        p_new = jnp.exp(s_new - m_f)
        o_ref[0] = (a_f * acc_sc[...] + p_new * v_new) / (a_f * l_col + p_new)


def _decode_attn(page_table, zs3, cn, cache_k, cache_v, cache_lf, layer, n_heads):
    bsz, n_pages = page_table.shape
    n_grp = PAGES_PER_STEP
    assert n_pages % n_grp == 0 and cache_k.shape[2] == 128 * n_heads and cache_lf.shape[2] == n_heads
    assert n_heads & (n_heads - 1) == 0 and n_heads <= 8
    rows_pp = cache_k.shape[2]

    def page_spec(block, g):
        return pl.BlockSpec(block, lambda b, p, pt: (pt[b, n_pages - (p + 1) * n_grp + g], layer, 0, 0))

    kv_specs = [page_spec((1, 1, rows_pp, HEAD_DIM), g) for g in range(n_grp)]
    lf_specs = [page_spec((1, 1, n_heads, 128), g) for g in range(n_grp)]
    grid_spec = pltpu.PrefetchScalarGridSpec(
        num_scalar_prefetch=1,
        grid=(bsz, n_pages // n_grp),
        in_specs=[pl.BlockSpec((1, 3 * n_heads, HEAD_DIM), lambda b, p, pt: (b, 0, 0)),
                  pl.BlockSpec((1, 1, 128), lambda b, p, pt: (b, 0, 0))] + kv_specs + kv_specs + lf_specs,
        out_specs=pl.BlockSpec((1, n_heads, HEAD_DIM), lambda b, p, pt: (b, 0, 0)),
        scratch_shapes=[pltpu.VMEM((n_heads, 128), F32)] * 3 + [pltpu.VMEM((n_heads, HEAD_DIM), F32),
                                                                 pltpu.VMEM((n_heads, 128), F32)],
    )
    return pl.pallas_call(
        functools.partial(_decode_attn_kernel, n_grp=n_grp, n_heads=n_heads, scale=HEAD_DIM ** -0.5),
        grid_spec=grid_spec,
        out_shape=jax.ShapeDtypeStruct((bsz, n_heads, HEAD_DIM), F32),
        compiler_params=_params(("arbitrary", "arbitrary"), 48),
        name="fox_attn_sample",
    )(page_table, zs3, cn, *([cache_k] * n_grp), *([cache_v] * n_grp), *([cache_lf] * n_grp))


def kernel(x_prompt, x_sample, cache_k, cache_v, cache_logf, state_conv, page_table, w_in, b_f, g_v, w_s, b_s,
           conv_w, conv_b, ln_c_g, ln_c_b, w_o, g_pre_mix, g_post_mix, g_pre_mlp, g_post_mlp, w_up, w_down):
    bp, tp, d = x_prompt.shape
    bs, ds, _ = x_sample.shape
    assert ds == 1, "the sample path handles exactly one new token per sequence"
    depth = w_in.shape[0]
    n_pool, _, page, n_heads, _ = cache_k.shape
    d_a = n_heads * HEAD_DIM
    n_heads_b = w_s.shape[1]
    d_b = n_heads_b * HEAD_DIM
    d_c = conv_w.shape[2]
    off_f = 3 * d_a
    assert w_in.shape[2] == off_f + n_heads + 2 * d_b + 2 * d_c and d_b == d_c and page == CHUNK
    mp = bp * tp

    n_rest = 2 * d_b + 2 * d_c
    n_rest_pad = -(-(n_rest + 128) // REST_TN) * REST_TN
    w_qkv = w_in[:, :, :off_f].astype(BF16)
    w_rest = jnp.concatenate(
        [w_in[:, :, off_f + n_heads:], w_in[:, :, off_f:off_f + n_heads],
         jnp.zeros((depth, d, n_rest_pad - n_rest - n_heads), w_in.dtype)], axis=2).astype(BF16)
    w_o_b, w_up_b, w_down_b = w_o.astype(BF16), w_up.astype(BF16), w_down.astype(BF16)
    bf_pad = jnp.pad(b_f, ((0, 0), (0, 128 - n_heads)))
    f_blk = n_rest // 128
    row2 = lambda a, l: a[l][None, :]

    cache_k2 = cache_k.reshape(n_pool, depth, page * n_heads, HEAD_DIM)
    cache_v2 = cache_v.reshape(n_pool, depth, page * n_heads, HEAD_DIM)
    cache_lf = cache_logf.reshape(n_pool, depth, page * n_heads // 128, 128)

    xp = x_prompt.reshape(mp, d)
    xs = x_sample.reshape(bs, d)
    xpn = _norm_bf16(xp, row2(g_pre_mix, 0), 512)
    xsn = _norm_bf16(xs, row2(g_pre_mix, 0), bs)
    kp_l, vp_l, fp_l, cp_l, ks_l, vs_l, fs_l, cs_l, us_l = ([] for _ in range(9))
    for l in range(depth):
        g_next = row2(g_pre_mix, l + 1) if l + 1 < depth else None
        q, kb, vb, k_pages, v_pages = _proj_qkv(xpn, w_qkv[l], bp, tp, n_heads, page, 512)
        z3 = _matmul(_mm_kernel, xpn, w_rest[l], 1024, REST_TN, F32, "proj_rest").reshape(bp, tp, n_rest_pad)
        logf, cq, ct = _fox_prefix(z3, bf_pad[l][None, :], n_heads, f_blk)
        attn = _fox_attn_prompt(q.reshape(bp, tp, d_a), kb.reshape(bp, tp, d_a), vb.reshape(bp, tp, d_a),
                                cq, ct, n_heads, 512)
        sgu, conv, conv_tail = _mixer_prompt(z3, row2(g_v, l), w_s[l], b_s[l].T, conv_w[l], row2(conv_b, l),
                                             row2(ln_c_g, l), row2(ln_c_b, l), 256)
        x1, x1n = _wo_block(attn.reshape(mp, d_a), sgu.reshape(mp, d_b), conv.reshape(mp, d_c), w_o_b[l], xp,
                            row2(g_post_mix, l), row2(g_pre_mlp, l), 256)
        hid = _matmul(_mm_relu2_kernel, x1n, w_up_b[l], 512, 1024, BF16, "mlp_up")
        xp, xpn = _down_block(hid, w_down_b[l], x1, row2(g_post_mlp, l), g_next, 512, 1024)
        kp_l.append(k_pages)
        vp_l.append(v_pages)
        fp_l.append(logf)
        cp_l.append(conv_tail)
        zs_qkv = _matmul(_mm_kernel, xsn, w_qkv[l], bs, 1024, F32, "proj_qkv_sample")
        zs_rest = _matmul(_mm_kernel, xsn, w_rest[l], bs, REST_TN, F32, "proj_rest_sample")
        w0_row = jnp.repeat(w_s[l][:, 0, 0], HEAD_DIM)[None, :]
        b0_row = jnp.repeat(b_s[l][:, 0], HEAD_DIM)[None, :]
        sgu_s, conv_s, state_s, vn_s, logf_s = _mixer_sample(
            zs_rest, bf_pad[l][None, :], row2(g_v, l), w0_row, b0_row, state_conv[l], conv_w[l], row2(conv_b, l),
            row2(ln_c_g, l), row2(ln_c_b, l))
        attn_s = _decode_attn(page_table, zs_qkv.reshape(bs, 3 * n_heads, HEAD_DIM), logf_s[:, None, :],
                              cache_k2, cache_v2, cache_lf, l, n_heads)
        x1s, x1sn = _wo_block(attn_s.reshape(bs, d_a).astype(BF16), sgu_s, conv_s, w_o_b[l], xs,
                              row2(g_post_mix, l), row2(g_pre_mlp, l), bs)
        hid_s = _matmul(_mm_relu2_kernel, x1sn, w_up_b[l], bs, 1024, BF16, "mlp_up_sample")
        xs, xsn = _down_block(hid_s, w_down_b[l], x1s, row2(g_post_mlp, l), g_next, bs, 1024)
        ks_l.append(zs_qkv[:, d_a:2 * d_a])
        vs_l.append(zs_qkv[:, 2 * d_a:3 * d_a])
        fs_l.append(logf_s[:, :n_heads])
        cs_l.append(state_s)
        us_l.append(vn_s)

    npp = tp // page
    kv_out = lambda parts: jnp.stack(parts, axis=2).reshape(bp, npp, depth, page, n_heads, HEAD_DIM)
    logf_prompt = jnp.stack(fp_l, axis=1).reshape(bp, depth, npp, page, n_heads).swapaxes(1, 2)
    return (xp.reshape(bp, tp, d), xs.reshape(bs, ds, d),
            kv_out(kp_l), kv_out(vp_l), logf_prompt,
            jnp.stack(cp_l, axis=0),
            jnp.stack(ks_l, axis=1).reshape(bs, depth, ds, n_heads, HEAD_DIM),
            jnp.stack(vs_l, axis=1).reshape(bs, depth, ds, n_heads, HEAD_DIM),
            jnp.stack(fs_l, axis=1).reshape(bs, depth, ds, n_heads),
            jnp.stack(cs_l, axis=0),
            jnp.stack(us_l, axis=0).reshape(depth, bs, ds, d_b))
```

```python
import functools

import numpy as np
import jax
import jax.numpy as jnp
from jax import lax
from jax.experimental import pallas as pl
from jax.experimental.pallas import tpu as pltpu

F32 = jnp.float32
BF16 = jnp.bfloat16

HEAD_DIM = 128
CHUNK = 128
EPS = 1e-6
MIB = 1024 * 1024
GELU_C = float(np.sqrt(2.0 / np.pi))
LOG2E = float(np.log2(np.e))

REST_TN = 768
PAGES_PER_STEP = 8


def _params(semantics, vmem_mib):
    return pltpu.CompilerParams(dimension_semantics=semantics, vmem_limit_bytes=vmem_mib * MIB)


def _rms(x, g):
    return x * lax.rsqrt(jnp.mean(x * x, axis=-1, keepdims=True) + EPS) * g


def _gelu(x):
    return x * (0.5 * (1.0 + jnp.tanh(GELU_C * (x + 0.044715 * (x * x * x)))))


def _sigmoid(x):
    return 1.0 / (1.0 + jnp.exp(-x))


def _log_sigmoid(x):
    return jnp.minimum(x, 0.0) - jnp.log1p(jnp.exp(-jnp.abs(x)))


def _split3(x):
    hi = x.astype(BF16)
    r = x - hi.astype(F32)
    mid = r.astype(BF16)
    lo = (r - mid.astype(F32)).astype(BF16)
    return hi, mid, lo


def _dot(a, b):
    return jnp.dot(a, b, preferred_element_type=F32)


def _dot_nt(a, b):
    return lax.dot_general(a, b, (((1,), (1,)), ((), ())), preferred_element_type=F32)


def _dot3(x, w):
    hi, mid, lo = _split3(x)
    return (_dot(hi, w) + _dot(mid, w)) + _dot(lo, w)


def _dot3_left(w, x):
    hi, mid, lo = _split3(x)
    return (_dot(w, hi) + _dot(w, mid)) + _dot(w, lo)


def _ones_where(cond):
    return jnp.where(cond, 1.0, 0.0).astype(BF16)


def _norm_kernel(x_ref, g_ref, o_ref):
    o_ref[...] = _rms(x_ref[...], g_ref[...]).astype(o_ref.dtype)


def _norm_bf16(x, g, tm):
    m, d = x.shape
    return pl.pallas_call(
        _norm_kernel,
        grid=(m // tm,),
        in_specs=[pl.BlockSpec((tm, d), lambda i: (i, 0)), pl.BlockSpec((1, d), lambda i: (0, 0))],
        out_specs=pl.BlockSpec((tm, d), lambda i: (i, 0)),
        out_shape=jax.ShapeDtypeStruct((m, d), BF16),
        compiler_params=_params(("arbitrary",), 32),
        name="pre_norm",
    )(x, g)


def _mm_kernel(x_ref, w_ref, o_ref):
    o_ref[...] = _dot(x_ref[...], w_ref[...]).astype(o_ref.dtype)


def _matmul(x, w, layer, tm, tn, out_dtype, name):
    m, k = x.shape
    n = w.shape[2]
    return pl.pallas_call(
        _mm_kernel,
        grid=(m // tm, n // tn),
        in_specs=[pl.BlockSpec((tm, k), lambda i, j: (i, 0)),
                  pl.BlockSpec((None, k, tn), lambda i, j: (layer, 0, j))],
        out_specs=pl.BlockSpec((tm, tn), lambda i, j: (i, j)),
        out_shape=jax.ShapeDtypeStruct((m, n), out_dtype),
        compiler_params=_params(("arbitrary", "arbitrary"), 48),
        name=name,
    )(x, w)


def _ws_kernel(x_ref, w_ref, *rest, mode, has_prev, n_heads, page, q_scale):
    if has_prev:
        rest = rest[1:]
    outs, w_sc = rest[:-1], rest[-1]

    @pl.when(pl.program_id(1) == 0)
    def _():
        w_sc[...] = w_ref[...].astype(BF16)

    acc = _dot(x_ref[...], w_sc[...])
    if mode == "f32":
        outs[0][...] = acc
    elif mode == "q":
        outs[0][...] = (acc * q_scale).astype(BF16)
    elif mode == "relu2":
        h = jnp.maximum(acc, 0.0)
        outs[0][...] = (h * h).astype(BF16)
    else:
        outs[0][...] = acc.astype(BF16)
        dst_ref = outs[1]
        for pg in range(acc.shape[0] // page):
            for h in range(n_heads):
                dst_ref[0, pg, pl.ds(h, page, stride=n_heads), :] = (
                    acc[pg * page:(pg + 1) * page, h * HEAD_DIM:(h + 1) * HEAD_DIM])


def _ws_matmul(x, w, layer, col0, n_cols, tm, tn, mode, name, pages=None):
    m, k = x.shape
    assert col0 % tn == 0 and n_cols % tn == 0
    jb = col0 // tn
    in_specs = [pl.BlockSpec((tm, k), lambda j, i: (i, 0)),
                pl.BlockSpec((None, k, tn), lambda j, i: (layer, 0, jb + j))]
    args = [x, w]
    out_specs = [pl.BlockSpec((tm, tn), lambda j, i: (i, j))]
    out_shape = [jax.ShapeDtypeStruct((m, n_cols), F32 if mode == "f32" else BF16)]
    aliases = {}
    n_heads = page = 0
    has_prev = False
    if mode == "kv":
        prev, bsz, t, depth, n_heads, page = pages
        assert n_cols == tn == n_heads * HEAD_DIM and t % tm == 0 and tm % page == 0
        tps = t // tm
        out_specs.append(pl.BlockSpec((1, tm // page, None, page * n_heads, HEAD_DIM),
                                      lambda j, i: (i // tps, i % tps, layer, 0, 0)))
        out_shape.append(jax.ShapeDtypeStruct((bsz, t // page, depth, page * n_heads, HEAD_DIM), F32))
        if prev is not None:
            has_prev = True
            in_specs.append(pl.BlockSpec(memory_space=pl.ANY))
            args.append(prev)
            aliases = {2: 1}
    out = pl.pallas_call(
        functools.partial(_ws_kernel, mode=mode, has_prev=has_prev, n_heads=n_heads, page=page,
                          q_scale=HEAD_DIM ** -0.5 * LOG2E),
        grid=(n_cols // tn, m // tm),
        in_specs=in_specs,
        out_specs=out_specs,
        out_shape=out_shape,
        scratch_shapes=[pltpu.VMEM((k, tn), BF16)],
        input_output_aliases=aliases,
        compiler_params=_params(("arbitrary", "arbitrary"), 56),
        name=name,
    )(*args)
    return out if mode == "kv" else out[0]


def _wo_kernel(a_ref, s_ref, c_ref, w_ref, x_ref, g1_ref, g2_ref, x1_ref, xn_ref, *, d_a, d_b):
    mixed = _dot(a_ref[...], w_ref[0:d_a, :])
    mixed += _dot(s_ref[...], w_ref[d_a:d_a + d_b, :])
    mixed += _dot(c_ref[...], w_ref[d_a + d_b:, :])
    x1 = x_ref[...] + _rms(mixed, g1_ref[...])
    x1_ref[...] = x1
    xn_ref[...] = _rms(x1, g2_ref[...]).astype(xn_ref.dtype)


def _wo_block(attn, sgu, conv, w_o, layer, x, g_post, g_pre_mlp, tm):
    m, d = x.shape
    d_a, d_b, d_c = attn.shape[1], sgu.shape[1], conv.shape[1]
    row = lambda width: pl.BlockSpec((tm, width), lambda i: (i, 0))
    const = lambda shape: pl.BlockSpec(shape, lambda i: (0, 0))
    w_spec = pl.BlockSpec((None, d, d), lambda i: (layer, 0, 0))
    return pl.pallas_call(
        functools.partial(_wo_kernel, d_a=d_a, d_b=d_b),
        grid=(m // tm,),
        in_specs=[row(d_a), row(d_b), row(d_c), w_spec, row(d), const((1, d)), const((1, d))],
        out_specs=[row(d), row(d)],
        out_shape=[jax.ShapeDtypeStruct((m, d), F32), jax.ShapeDtypeStruct((m, d), BF16)],
        compiler_params=_params(("arbitrary",), 48),
        name="wo_norm_residual",
    )(attn, sgu, conv, w_o, x, g_post, g_pre_mlp)


def _down_kernel(h_ref, w_ref, x1_ref, g_ref, *rest, emit_next):
    if emit_next:
        gn_ref, x2_ref, xn_ref, acc_ref = rest
    else:
        x2_ref, acc_ref = rest
    k = pl.program_id(1)

    @pl.when(k == 0)
    def _():
        acc_ref[...] = jnp.zeros_like(acc_ref)

    acc_ref[...] += _dot(h_ref[...], w_ref[...])

    @pl.when(k == pl.num_programs(1) - 1)
    def _():
        x2 = x1_ref[...] + _rms(acc_ref[...], g_ref[...])
        x2_ref[...] = x2
        if emit_next:
            xn_ref[...] = _rms(x2, gn_ref[...]).astype(xn_ref.dtype)


def _down_block(hid, w_down, layer, x1, g_post, g_next, tm, tk):
    m, d = x1.shape
    kdim = hid.shape[1]
    emit_next = g_next is not None
    row = pl.BlockSpec((tm, d), lambda i, k: (i, 0))
    const = pl.BlockSpec((1, d), lambda i, k: (0, 0))
    in_specs = [pl.BlockSpec((tm, tk), lambda i, k: (i, k)),
                pl.BlockSpec((None, tk, d), lambda i, k: (layer, k, 0)), row, const]
    args = [hid, w_down, x1, g_post]
    out_specs = [row]
    out_shape = [jax.ShapeDtypeStruct((m, d), F32)]
    if emit_next:
        in_specs.append(const)
        args.append(g_next)
        out_specs.append(row)
        out_shape.append(jax.ShapeDtypeStruct((m, d), BF16))
    out = pl.pallas_call(
        functools.partial(_down_kernel, emit_next=emit_next),
        grid=(m // tm, kdim // tk),
        in_specs=in_specs,
        out_specs=out_specs,
        out_shape=out_shape,
        scratch_shapes=[pltpu.VMEM((tm, d), F32)],
        compiler_params=_params(("arbitrary", "arbitrary"), 56),
        name="down_norm_residual",
    )(*args)
    return (out[0], out[1]) if emit_next else (out[0], None)


def _prefix_kernel(zf_ref, bf_ref, logf_ref, cq_ref, ct_ref, c_sc, *, t, n_heads):
    lf = _log_sigmoid(zf_ref[0] + bf_ref[...])
    logf_ref[0] = lf[:, 0:n_heads]
    r_i = lax.broadcasted_iota(jnp.int32, (CHUNK, CHUNK), 0)
    c_i = lax.broadcasted_iota(jnp.int32, (CHUNK, CHUNK), 1)
    tri = _ones_where(r_i >= c_i)
    carry = jnp.zeros((1, 128), F32)
    for blk in range(t // CHUNK):
        cb = _dot3_left(tri, lf[blk * CHUNK:(blk + 1) * CHUNK, :]) + carry
        c_sc[blk * CHUNK:(blk + 1) * CHUNK, :] = cb * LOG2E
        carry = cb[CHUNK - 1:CHUNK, :]
    c = c_sc[...]
    ct_ref[0] = c.T[0:n_heads, :]
    for h in range(n_heads):
        cq_ref[0, h] = c[:, h:h + 1]


def _fox_prefix(z3, bf_pad, n_heads, f_blk):
    b, t, _ = z3.shape
    return pl.pallas_call(
        functools.partial(_prefix_kernel, t=t, n_heads=n_heads),
        grid=(b,),
        in_specs=[pl.BlockSpec((1, t, 128), lambda i: (i, 0, f_blk)), pl.BlockSpec((1, 128), lambda i: (0, 0))],
        out_specs=[
            pl.BlockSpec((1, t, n_heads), lambda i: (i, 0, 0)),
            pl.BlockSpec((1, n_heads, t, 1), lambda i: (i, 0, 0, 0)),
            pl.BlockSpec((1, n_heads, t), lambda i: (i, 0, 0)),
        ],
        out_shape=[
            jax.ShapeDtypeStruct((b, t, n_heads), F32),
            jax.ShapeDtypeStruct((b, n_heads, t, 1), F32),
            jax.ShapeDtypeStruct((b, n_heads, t), F32),
        ],
        scratch_shapes=[pltpu.VMEM((t, 128), F32)],
        compiler_params=_params(("arbitrary",), 48),
        name="fox_prefix",
    )(z3, bf_pad)


def _fox_attn_kernel(qi_ref, ki_ref, q_ref, k_ref, v_ref, cq_ref, ck_ref, o_ref, m_sc, l_sc, acc_sc, cq_sc,
                     *, tq, n_heads):
    step = pl.program_id(1)
    qi = qi_ref[step]
    ki = ki_ref[step]
    n_sub = tq // 128

    @pl.when(ki == 0)
    def _():
        m_sc[...] = jnp.full_like(m_sc, -jnp.inf)
        l_sc[...] = jnp.zeros_like(l_sc)
        acc_sc[...] = jnp.zeros_like(acc_sc)
        for h in range(n_heads):
            cq_sc[h] = jnp.broadcast_to(cq_ref[0, h], (tq, 128))

    def block(diagonal):
        if diagonal:
            keep = (lax.broadcasted_iota(jnp.int32, (tq, tq), 0) >= lax.broadcasted_iota(jnp.int32, (tq, tq), 1))
        for h in range(n_heads):
            hs = slice(h * HEAD_DIM, (h + 1) * HEAD_DIM)
            s = _dot_nt(q_ref[0, :, hs], k_ref[0, :, hs]) - ck_ref[0, h:h + 1, :]
            if diagonal:
                s = jnp.where(keep, s, -jnp.inf)
            subs = [s[:, j * 128:(j + 1) * 128] for j in range(n_sub)]
            mc = subs[0]
            for x in subs[1:]:
                mc = jnp.maximum(mc, x)
            cq = cq_sc[h]
            m_prev = m_sc[h]
            m_new = jnp.maximum(m_prev, jnp.max(mc, axis=1, keepdims=True) + cq)
            alpha = jnp.exp2(m_prev - m_new)
            shift = m_new - cq
            ps = [jnp.exp2(x - shift) for x in subs]
            lsum = ps[0]
            for x in ps[1:]:
                lsum = lsum + x
            l_sc[h] = alpha * l_sc[h] + jnp.sum(lsum, axis=1, keepdims=True)
            p = jnp.concatenate(ps, axis=1).astype(BF16)
            acc_sc[:, hs] = alpha * acc_sc[:, hs] + _dot(p, v_ref[0, :, hs])
            m_sc[h] = m_new

    @pl.when(ki < qi)
    def _():
        block(False)

    @pl.when(ki == qi)
    def _():
        block(True)
        for h in range(n_heads):
            hs = slice(h * HEAD_DIM, (h + 1) * HEAD_DIM)
            o_ref[0, :, hs] = (acc_sc[:, hs] / l_sc[h]).astype(o_ref.dtype)


def _fox_attn_prompt(q, k, v, cq, ct, n_heads, tq):
    b, t, d_a = q.shape
    nq = t // tq
    pairs = [(i, j) for i in range(nq) for j in range(i + 1)]
    qi_tab = jnp.asarray([pr[0] for pr in pairs], jnp.int32)
    ki_tab = jnp.asarray([pr[1] for pr in pairs], jnp.int32)
    q_blk = pl.BlockSpec((1, tq, d_a), lambda bi, s, qt, kt: (bi, qt[s], 0))
    kv_blk = pl.BlockSpec((1, tq, d_a), lambda bi, s, qt, kt: (bi, kt[s], 0))
    grid_spec = pltpu.PrefetchScalarGridSpec(
        num_scalar_prefetch=2,
        grid=(b, len(pairs)),
        in_specs=[
            q_blk, kv_blk, kv_blk,
            pl.BlockSpec((1, n_heads, tq, 1), lambda bi, s, qt, kt: (bi, 0, qt[s], 0)),
            pl.BlockSpec((1, n_heads, tq), lambda bi, s, qt, kt: (bi, 0, kt[s])),
        ],
        out_specs=q_blk,
        scratch_shapes=[
            pltpu.VMEM((n_heads, tq, 128), F32),
            pltpu.VMEM((n_heads, tq, 128), F32),
            pltpu.VMEM((tq, d_a), F32),
            pltpu.VMEM((n_heads, tq, 128), F32),
        ],
    )
    return pl.pallas_call(
        functools.partial(_fox_attn_kernel, tq=tq, n_heads=n_heads),
        grid_spec=grid_spec,
        out_shape=jax.ShapeDtypeStruct((b, t, d_a), BF16),
        compiler_params=_params(("arbitrary", "arbitrary"), 48),
        name="fox_attn_prompt",
    )(qi_tab, ki_tab, q, k, v, cq, ct)


def _layernorm_silu(y, g, b):
    mu = jnp.mean(y, axis=-1, keepdims=True)
    yc = y - mu
    yn = yc * lax.rsqrt(jnp.mean(yc * yc, axis=-1, keepdims=True) + EPS) * g + b
    return yn * _sigmoid(yn)


def _mixer_kernel(ub_ref, vb_ref, ac_ref, gc_ref, gv_ref, ws_ref, bst_ref, cw_ref, cb_ref, lg_ref, lb_ref,
                  sgu_ref, conv_ref, tail_ref, g_sc, *, tm, n_heads_b, conv_w, halo):
    ti = pl.program_id(1)

    u = _gelu(ub_ref[0])
    vn = _rms(_gelu(vb_ref[0]), gv_ref[...])
    r_i = lax.broadcasted_iota(jnp.int32, (CHUNK, CHUNK), 0)
    c_i = lax.broadcasted_iota(jnp.int32, (CHUNK, CHUNK), 1)
    for h in range(n_heads_b):
        hs = slice(h * HEAD_DIM, (h + 1) * HEAD_DIM)
        w_h = jnp.where(r_i >= c_i, ws_ref[h], 0.0).astype(BF16)
        bias_h = bst_ref[:, h:h + 1]
        for c in range(tm // CHUNK):
            rs = slice(c * CHUNK, (c + 1) * CHUNK)
            mix = _dot(w_h, vn[rs, hs].astype(BF16)) + bias_h
            sgu_ref[0, rs, hs] = (u[rs, hs] * mix).astype(sgu_ref.dtype)

    @pl.when(ti == 0)
    def _():
        g_sc[0:halo, :] = jnp.zeros((halo, g_sc.shape[1]), F32)

    g_sc[halo:halo + tm, :] = ac_ref[0] * _sigmoid(gc_ref[0])
    base = halo - (conv_w - 1)
    y = g_sc[base:base + tm, :] * cw_ref[0:1, :]
    for k in range(1, conv_w):
        y += g_sc[base + k:base + k + tm, :] * cw_ref[k:k + 1, :]
    y += cb_ref[...]
    conv_ref[0] = _layernorm_silu(y, lg_ref[...], lb_ref[...]).astype(conv_ref.dtype)

    @pl.when(ti == pl.num_programs(1) - 1)
    def _():
        tail_ref[0] = g_sc[halo + tm - (conv_w - 1):halo + tm, :]

    g_sc[0:halo, :] = g_sc[tm:tm + halo, :]


def _mixer_prompt(z3, g_v, w_s, b_s_t, conv_w, conv_b, ln_g, ln_b, tm):
    b, t, _ = z3.shape
    n_heads_b = w_s.shape[0]
    d_b = n_heads_b * HEAD_DIM
    d_c = conv_w.shape[1]
    kw = conv_w.shape[0]
    halo = 32
    zcol = lambda c: pl.BlockSpec((1, tm, d_b), lambda bi, ti: (bi, ti, c))
    const = lambda shape: pl.BlockSpec(shape, lambda bi, ti: (0,) * len(shape))
    return pl.pallas_call(
        functools.partial(_mixer_kernel, tm=tm, n_heads_b=n_heads_b, conv_w=kw, halo=halo),
        grid=(b, t // tm),
        in_specs=[zcol(0), zcol(1), zcol(2), zcol(3),
                  const((1, d_b)), const((n_heads_b, CHUNK, CHUNK)), const((CHUNK, n_heads_b)),
                  const((kw, d_c)), const((1, d_c)), const((1, d_c)), const((1, d_c))],
        out_specs=[
            pl.BlockSpec((1, tm, d_b), lambda bi, ti: (bi, ti, 0)),
            pl.BlockSpec((1, tm, d_c), lambda bi, ti: (bi, ti, 0)),
            pl.BlockSpec((1, kw - 1, d_c), lambda bi, ti: (bi, 0, 0)),
        ],
        out_shape=[
            jax.ShapeDtypeStruct((b, t, d_b), BF16),
            jax.ShapeDtypeStruct((b, t, d_c), BF16),
            jax.ShapeDtypeStruct((b, kw - 1, d_c), F32),
        ],
        scratch_shapes=[pltpu.VMEM((halo + tm, d_c), F32)],
        compiler_params=_params(("arbitrary", "arbitrary"), 32),
        name="mixer_prompt",
    )(z3, z3, z3, z3, g_v, w_s, b_s_t, conv_w, conv_b, ln_g, ln_b)


def _mixer_sample_kernel(zs_ref, bf_ref, gv_ref, w0_ref, b0_ref, st_ref, cw_ref, cb_ref, lg_ref, lb_ref,
                         sgu_ref, conv_ref, state_ref, vn_ref, logf_ref, y_sc, *, d_b, conv_w):
    n = zs_ref.shape[0]
    u = _gelu(zs_ref[:, 0:d_b])
    vn = _rms(_gelu(zs_ref[:, d_b:2 * d_b]), gv_ref[...])
    vn_ref[...] = vn
    sgu_ref[...] = (u * (w0_ref[...] * vn + b0_ref[...])).astype(sgu_ref.dtype)
    glu = zs_ref[:, 2 * d_b:3 * d_b] * _sigmoid(zs_ref[:, 3 * d_b:4 * d_b])
    kw = conv_w - 1
    for bi in range(n):
        g_new = glu[bi:bi + 1, :]
        y_sc[bi:bi + 1, :] = (jnp.sum(st_ref[bi] * cw_ref[0:kw, :], axis=0, keepdims=True)
                              + g_new * cw_ref[kw:kw + 1, :])
        state_ref[bi, 0:kw - 1, :] = st_ref[bi, 1:kw, :]
        state_ref[bi, kw - 1:kw, :] = g_new
    conv_ref[...] = _layernorm_silu(y_sc[...] + cb_ref[...], lg_ref[...], lb_ref[...]).astype(conv_ref.dtype)
    logf_ref[...] = _log_sigmoid(zs_ref[:, 4 * d_b:4 * d_b + 128] + bf_ref[...])


def _mixer_sample(zs, bf_pad, g_v, w0_row, b0_row, state, conv_w, conv_b, ln_g, ln_b):
    n = zs.shape[0]
    d_b = g_v.shape[1]
    kw, d_c = conv_w.shape
    full = lambda shape: pl.BlockSpec(shape, lambda i: (0,) * len(shape))
    args = (zs, bf_pad, g_v, w0_row, b0_row, state, conv_w, conv_b, ln_g, ln_b)
    out_shape = [
        jax.ShapeDtypeStruct((n, d_b), BF16),
        jax.ShapeDtypeStruct((n, d_c), BF16),
        jax.ShapeDtypeStruct((n, kw - 1, d_c), F32),
        jax.ShapeDtypeStruct((n, d_b), F32),
        jax.ShapeDtypeStruct((n, 128), F32),
    ]
    return pl.pallas_call(
        functools.partial(_mixer_sample_kernel, d_b=d_b, conv_w=kw),
        grid=(1,),
        in_specs=[full(a.shape) for a in args],
        out_specs=[full(s.shape) for s in out_shape],
        out_shape=out_shape,
        scratch_shapes=[pltpu.VMEM((n, d_c), F32)],
        compiler_params=_params(("arbitrary",), 32),
        name="mixer_sample",
    )(*args)


def _decode_attn_kernel(pt_ref, zs_ref, cn_ref, *refs, n_grp, n_heads, scale):
    k_refs, v_refs, lf_refs = refs[:n_grp], refs[n_grp:2 * n_grp], refs[2 * n_grp:3 * n_grp]
    o_ref = refs[3 * n_grp]
    cn_sc, m_sc, l_sc, acc_sc, tail_sc = refs[3 * n_grp + 1:]
    p = pl.program_id(1)
    rows_pp = k_refs[0].shape[2]
    n_flat = rows_pp // 128
    lane = lax.broadcasted_iota(jnp.int32, (n_heads, 128), 1)
    sub = lax.broadcasted_iota(jnp.int32, (n_heads, 128), 0)
    cls_mask = n_heads - 1
    own = sub == (lane & cls_mask)
    diag = sub == lane
    r_i = lax.broadcasted_iota(jnp.int32, (128, 128), 0)
    c_i = lax.broadcasted_iota(jnp.int32, (128, 128), 1)
    same = (r_i & cls_mask) == (c_i & cls_mask)

    @pl.when(p == 0)
    def _():
        spread = _ones_where((r_i < n_heads) & ((c_i & cls_mask) == r_i))
        cn_sc[...] = _dot3(jnp.broadcast_to(cn_ref[0], (n_heads, 128)), spread)
        m_sc[...] = jnp.full_like(m_sc, -jnp.inf)
        l_sc[...] = jnp.zeros_like(l_sc)
        acc_sc[...] = jnp.zeros_like(acc_sc)
        tail_sc[...] = jnp.zeros_like(tail_sc)

    lf = jnp.concatenate([lf_refs[g][0, 0] for g in range(n_grp)], axis=0)
    n_rows = n_grp * n_flat
    row_tot = _dot3(lf, _ones_where(same))
    within = _dot3(lf, _ones_where(same & (r_i > c_i)))
    rr = lax.broadcasted_iota(jnp.int32, (n_rows, n_rows), 0)
    cc = lax.broadcasted_iota(jnp.int32, (n_rows, n_rows), 1)
    suffix = within + _dot3_left(_ones_where(cc > rr), row_tot) + tail_sc[0:1, :]

    q8 = zs_ref[0, 0:n_heads, :].astype(BF16)
    scores = []
    for g in range(n_grp):
        s_t = _dot_nt(q8, k_refs[g][0, 0].astype(BF16))
        flat = [jnp.sum(jnp.where(own, s_t[:, a * 128:(a + 1) * 128], 0.0), axis=0, keepdims=True)
                for a in range(n_flat)]
        scores.append(jnp.concatenate(flat, axis=0) * scale
                      + suffix[g * n_flat:(g + 1) * n_flat, :] + cn_sc[...])
    m_step = scores[0]
    for x in scores[1:]:
        m_step = jnp.maximum(m_step, x)
    shift = 1
    while shift < n_flat:
        m_step = jnp.maximum(m_step, pltpu.roll(m_step, shift, 0))
        shift *= 2
    shift = n_heads
    while shift < 128:
        m_step = jnp.maximum(m_step, pltpu.roll(m_step, shift, 1))
        shift *= 2
    m_prev = m_sc[...]
    m_new = jnp.maximum(m_prev, m_step)
    alpha = jnp.exp(m_prev - m_new)
    probs = [jnp.exp(x - m_new) for x in scores]
    l_step = probs[0]
    for x in probs[1:]:
        l_step = l_step + x
    l_sc[...] = alpha * l_sc[...] + l_step
    pv = jnp.zeros((n_heads, HEAD_DIM), F32)
    for g in range(n_grp):
        blocks = [jnp.where(own, jnp.broadcast_to(probs[g][a:a + 1, :], (n_heads, 128)), 0.0)
                  for a in range(n_flat)]
        pv += _dot(jnp.concatenate(blocks, axis=1).astype(BF16), v_refs[g][0, 0].astype(BF16))
    alpha_col = jnp.sum(jnp.where(diag, alpha, 0.0), axis=1, keepdims=True)
    acc_sc[...] = alpha_col * acc_sc[...] + pv
    m_sc[...] = m_new
    tail_sc[...] += jnp.sum(row_tot, axis=0, keepdims=True)

    @pl.when(p == pl.num_programs(1) - 1)
    def _():
        m_col = jnp.max(jnp.where(diag, m_sc[...], -jnp.inf), axis=1, keepdims=True)
        l_row = jnp.sum(l_sc[...], axis=0, keepdims=True)
        l_col = jnp.sum(jnp.where(own, jnp.broadcast_to(l_row, (n_heads, 128)), 0.0), axis=1, keepdims=True)
        cn_col = jnp.sum(jnp.where(diag, cn_sc[...], 0.0), axis=1, keepdims=True)
        q_new = zs_ref[0, 0:n_heads, :]
        k_new = zs_ref[0, n_heads:2 * n_heads, :]
        v_new = zs_ref[0, 2 * n_heads:3 * n_heads, :]
        s_new = jnp.sum(q_new * k_new, axis=1, keepdims=True) * scale + cn_col - cn_col
        m_f = jnp.maximum(m_col, s_new)
        a_f = jnp.exp(m_col - m_f)
        p_new = jnp.exp(s_new - m_f)
        o_ref[0] = (a_f * acc_sc[...] + p_new * v_new) / (a_f * l_col + p_new)


def _decode_attn(page_table, zs3, cn, cache_k, cache_v, cache_lf, layer, n_heads):
    bsz, n_pages = page_table.shape
    n_grp = PAGES_PER_STEP
    assert n_pages % n_grp == 0 and cache_k.shape[2] == 128 * n_heads and cache_lf.shape[2] == n_heads
    assert n_heads & (n_heads - 1) == 0 and n_heads <= 8
    rows_pp = cache_k.shape[2]

    def page_spec(block, g):
        return pl.BlockSpec(block, lambda b, p, pt: (pt[b, n_pages - (p + 1) * n_grp + g], layer, 0, 0))

    kv_specs = [page_spec((1, 1, rows_pp, HEAD_DIM), g) for g in range(n_grp)]
    lf_specs = [page_spec((1, 1, n_heads, 128), g) for g in range(n_grp)]
    grid_spec = pltpu.PrefetchScalarGridSpec(
        num_scalar_prefetch=1,
        grid=(bsz, n_pages // n_grp),
        in_specs=[pl.BlockSpec((1, 3 * n_heads, HEAD_DIM), lambda b, p, pt: (b, 0, 0)),
                  pl.BlockSpec((1, 1, 128), lambda b, p, pt: (b, 0, 0))] + kv_specs + kv_specs + lf_specs,
        out_specs=pl.BlockSpec((1, n_heads, HEAD_DIM), lambda b, p, pt: (b, 0, 0)),
        scratch_shapes=[pltpu.VMEM((n_heads, 128), F32)] * 3 + [pltpu.VMEM((n_heads, HEAD_DIM), F32),
                                                                 pltpu.VMEM((n_heads, 128), F32)],
    )
    return pl.pallas_call(
        functools.partial(_decode_attn_kernel, n_grp=n_grp, n_heads=n_heads, scale=HEAD_DIM ** -0.5),
        grid_spec=grid_spec,
        out_shape=jax.ShapeDtypeStruct((bsz, n_heads, HEAD_DIM), F32),
        compiler_params=_params(("arbitrary", "arbitrary"), 48),
        name="fox_attn_sample",
    )(page_table, zs3, cn, *([cache_k] * n_grp), *([cache_v] * n_grp), *([cache_lf] * n_grp))


def kernel(x_prompt, x_sample, cache_k, cache_v, cache_logf, state_conv, page_table, w_in, b_f, g_v, w_s, b_s,
           conv_w, conv_b, ln_c_g, ln_c_b, w_o, g_pre_mix, g_post_mix, g_pre_mlp, g_post_mlp, w_up, w_down):
    bp, tp, d = x_prompt.shape
    bs, ds, _ = x_sample.shape
    assert ds == 1, "the sample path handles exactly one new token per sequence"
    depth = w_in.shape[0]
    n_pool, _, page, n_heads, _ = cache_k.shape
    d_a = n_heads * HEAD_DIM
    n_heads_b = w_s.shape[1]
    d_b = n_heads_b * HEAD_DIM
    d_c = conv_w.shape[2]
    off_f = 3 * d_a
    assert w_in.shape[2] == off_f + n_heads + 2 * d_b + 2 * d_c and d_b == d_c and page == CHUNK
    mp = bp * tp

    n_rest = 2 * d_b + 2 * d_c
    n_rest_pad = -(-(n_rest + 128) // REST_TN) * REST_TN
    w_rest = jnp.concatenate(
        [w_in[:, :, off_f + n_heads:].astype(BF16), w_in[:, :, off_f:off_f + n_heads].astype(BF16),
         jnp.zeros((depth, d, n_rest_pad - n_rest - n_heads), BF16)], axis=2)
    w_o_b, w_down_b = w_o.astype(BF16), w_down.astype(BF16)
    d_ff = w_up.shape[2]
    bf_pad = jnp.pad(b_f, ((0, 0), (0, 128 - n_heads)))
    f_blk = n_rest // 128
    row2 = lambda a, l: a[l][None, :]

    cache_k2 = cache_k.reshape(n_pool, depth, page * n_heads, HEAD_DIM)
    cache_v2 = cache_v.reshape(n_pool, depth, page * n_heads, HEAD_DIM)
    cache_lf = cache_logf.reshape(n_pool, depth, page * n_heads // 128, 128)

    xp = x_prompt.reshape(mp, d)
    xs = x_sample.reshape(bs, d)
    xpn = _norm_bf16(xp, row2(g_pre_mix, 0), 512)
    xsn = _norm_bf16(xs, row2(g_pre_mix, 0), bs)
    fp_l, cp_l, ks_l, vs_l, fs_l, cs_l, us_l = ([] for _ in range(7))
    k_pages = v_pages = None
    for l in range(depth):
        g_next = row2(g_pre_mix, l + 1) if l + 1 < depth else None
        q = _ws_matmul(xpn, w_in, l, 0, d_a, 1024, d_a, "q", "proj_q")
        kb, k_pages = _ws_matmul(xpn, w_in, l, d_a, d_a, 1024, d_a, "kv", "proj_k",
                                 pages=(k_pages, bp, tp, depth, n_heads, page))
        vb, v_pages = _ws_matmul(xpn, w_in, l, 2 * d_a, d_a, 1024, d_a, "kv", "proj_v",
                                 pages=(v_pages, bp, tp, depth, n_heads, page))
        z3 = _matmul(xpn, w_rest, l, 1024, REST_TN, F32, "proj_rest").reshape(bp, tp, n_rest_pad)
        logf, cq, ct = _fox_prefix(z3, bf_pad[l][None, :], n_heads, f_blk)
        attn = _fox_attn_prompt(q.reshape(bp, tp, d_a), kb.reshape(bp, tp, d_a), vb.reshape(bp, tp, d_a),
                                cq, ct, n_heads, 512)
        sgu, conv, conv_tail = _mixer_prompt(z3, row2(g_v, l), w_s[l], b_s[l].T, conv_w[l], row2(conv_b, l),
                                             row2(ln_c_g, l), row2(ln_c_b, l), 256)
        x1, x1n = _wo_block(attn.reshape(mp, d_a), sgu.reshape(mp, d_b), conv.reshape(mp, d_c), w_o_b, l, xp,
                            row2(g_post_mix, l), row2(g_pre_mlp, l), 256)
        hid = _ws_matmul(x1n, w_up, l, 0, d_ff, 1024, 1024, "relu2", "mlp_up")
        xp, xpn = _down_block(hid, w_down_b, l, x1, row2(g_post_mlp, l), g_next, 512, 2048)
        fp_l.append(logf)
        cp_l.append(conv_tail)
        zs_qkv = _ws_matmul(xsn, w_in, l, 0, 3 * d_a, bs, d_a, "f32", "proj_qkv_sample")
        zs_rest = _matmul(xsn, w_rest, l, bs, REST_TN, F32, "proj_rest_sample")
        w0_row = jnp.repeat(w_s[l][:, 0, 0], HEAD_DIM)[None, :]
        b0_row = jnp.repeat(b_s[l][:, 0], HEAD_DIM)[None, :]
        sgu_s, conv_s, state_s, vn_s, logf_s = _mixer_sample(
            zs_rest, bf_pad[l][None, :], row2(g_v, l), w0_row, b0_row, state_conv[l], conv_w[l], row2(conv_b, l),
            row2(ln_c_g, l), row2(ln_c_b, l))
        attn_s = _decode_attn(page_table, zs_qkv.reshape(bs, 3 * n_heads, HEAD_DIM), logf_s[:, None, :],
                              cache_k2, cache_v2, cache_lf, l, n_heads)
        x1s, x1sn = _wo_block(attn_s.reshape(bs, d_a).astype(BF16), sgu_s, conv_s, w_o_b, l, xs,
                              row2(g_post_mix, l), row2(g_pre_mlp, l), bs)
        hid_s = _ws_matmul(x1sn, w_up, l, 0, d_ff, bs, 1024, "relu2", "mlp_up_sample")
        xs, xsn = _down_block(hid_s, w_down_b, l, x1s, row2(g_post_mlp, l), g_next, bs, 2048)
        ks_l.append(zs_qkv[:, d_a:2 * d_a])
        vs_l.append(zs_qkv[:, 2 * d_a:3 * d_a])
        fs_l.append(logf_s[:, :n_heads])
        cs_l.append(state_s)
        us_l.append(vn_s)

    npp = tp // page
    kv_out = lambda pages: pages.reshape(bp, npp, depth, page, n_heads, HEAD_DIM)
    logf_prompt = jnp.stack(fp_l, axis=1).reshape(bp, depth, npp, page, n_heads).swapaxes(1, 2)
    return (xp.reshape(bp, tp, d), xs.reshape(bs, ds, d),
            kv_out(k_pages), kv_out(v_pages), logf_prompt,
            jnp.stack(cp_l, axis=0),
            jnp.stack(ks_l, axis=1).reshape(bs, depth, ds, n_heads, HEAD_DIM),
            jnp.stack(vs_l, axis=1).reshape(bs, depth, ds, n_heads, HEAD_DIM),
            jnp.stack(fs_l, axis=1).reshape(bs, depth, ds, n_heads),
            jnp.stack(cs_l, axis=0),
            jnp.stack(us_l, axis=0).reshape(depth, bs, ds, d_b))
```

```python
import functools
from typing import NamedTuple

import numpy as np
import jax
import jax.numpy as jnp
from jax import lax
from jax.experimental import pallas as pl
from jax.experimental.pallas import tpu as pltpu

F32 = jnp.float32
BF16 = jnp.bfloat16

HEAD_DIM = 128
CHUNK = 128
EPS = 1e-6
MIB = 1024 * 1024
GELU_C = float(np.sqrt(2.0 / np.pi))
LOG2E = float(np.log2(np.e))

UP_TILE = (1024, 1024)
DOWN_TILE = (512, 1024)


def _params(semantics, vmem_mib):
    return pltpu.CompilerParams(dimension_semantics=semantics, vmem_limit_bytes=vmem_mib * MIB)


def _rms(x, g):
    return x * lax.rsqrt(jnp.mean(x * x, axis=-1, keepdims=True) + EPS) * g


def _gelu(x):
    return x * (0.5 * (1.0 + jnp.tanh(GELU_C * (x + 0.044715 * (x * x * x)))))


def _sigmoid(x):
    return 1.0 / (1.0 + jnp.exp(-x))


def _log_sigmoid(x):
    return jnp.minimum(x, 0.0) - jnp.log1p(jnp.exp(-jnp.abs(x)))


def _split3(x):
    hi = x.astype(BF16)
    r = x - hi.astype(F32)
    mid = r.astype(BF16)
    lo = (r - mid.astype(F32)).astype(BF16)
    return hi, mid, lo


def _dot(a, b):
    return jnp.dot(a, b, preferred_element_type=F32)


def _dot_nt(a, b):
    return lax.dot_general(a, b, (((1,), (1,)), ((), ())), preferred_element_type=F32)


def _dot3(x, w):
    hi, mid, lo = _split3(x)
    return (_dot(hi, w) + _dot(mid, w)) + _dot(lo, w)


def _dot3_left(w, x):
    hi, mid, lo = _split3(x)
    return (_dot(w, hi) + _dot(w, mid)) + _dot(w, lo)


def _ones_where(cond):
    return jnp.where(cond, 1.0, 0.0).astype(BF16)


def _norm_kernel(x_ref, g_ref, o_ref):
    o_ref[...] = _rms(x_ref[...], g_ref[...]).astype(o_ref.dtype)


def _norm_bf16(x, g, tm):
    m, d = x.shape
    return pl.pallas_call(
        _norm_kernel,
        grid=(m // tm,),
        in_specs=[pl.BlockSpec((tm, d), lambda i: (i, 0)), pl.BlockSpec((1, d), lambda i: (0, 0))],
        out_specs=pl.BlockSpec((tm, d), lambda i: (i, 0)),
        out_shape=jax.ShapeDtypeStruct((m, d), BF16),
        compiler_params=_params(("arbitrary",), 32),
        name="pre_norm",
    )(x, g)


def _ws_kernel(*refs, mode, transposed, with_f, has_prev, dec, n_heads, page, q_scale):
    refs = list(refs)
    if dec is not None:
        refs = refs[1:]
    x_ref, w_ref = refs[0], refs[1]
    pos = 2
    if with_f:
        wf_ref = refs[pos]
        pos += 1
    if has_prev:
        pos += 1
    if dec is not None:
        dec_in = refs[pos:pos + 2 + 3 * dec.n_grp]
        pos += 2 + 3 * dec.n_grp
    n_out = 2 if (mode == "kv" or with_f) else 1
    outs = refs[pos:pos + n_out]
    pos += n_out
    if dec is not None:
        dec_out = refs[pos]
        pos += 1
    w_sc = refs[pos]
    pos += 1
    if with_f:
        wf_sc = refs[pos]
        pos += 1
    dec_state = refs[pos:]

    @pl.when(pl.program_id(1) == 0)
    def _():
        w_sc[...] = (w_ref[0] if transposed else w_ref[...]).astype(BF16)
        if with_f:
            pad = jnp.zeros((wf_sc.shape[0] - wf_ref.shape[1], wf_sc.shape[1]), F32)
            wf_sc[...] = jnp.concatenate([wf_ref[0], pad], axis=0).astype(BF16)

    if dec is not None:
        dec_p = lax.rem(pl.program_id(0) * pl.num_programs(1) + pl.program_id(1), dec.n_steps)
        run_decode = functools.partial(_decode_step, p=dec_p, dec=dec, in_refs=dec_in, o_ref=dec_out,
                                       state=dec_state)
        run_decode("init")
        dec_carry = run_decode("scores")
    x = x_ref[...]
    acc = _dot_nt(x, w_sc[...]) if transposed else _dot(x, w_sc[...])
    if with_f:
        outs[1][...] = _dot_nt(x, wf_sc[...])
    if mode == "f32":
        outs[0][...] = acc
    elif mode == "q":
        outs[0][...] = (acc * q_scale).astype(BF16)
    elif mode == "relu2":
        h = jnp.maximum(acc, 0.0)
        outs[0][...] = (h * h).astype(BF16)
    else:
        outs[0][...] = acc.astype(BF16)
        dst_ref = outs[1]
        for pg in range(acc.shape[0] // page):
            for h in range(n_heads):
                dst_ref[0, pg, pl.ds(h, page, stride=n_heads), :] = (
                    acc[pg * page:(pg + 1) * page, h * HEAD_DIM:(h + 1) * HEAD_DIM])
    if dec is not None:
        run_decode("values", carry=dec_carry)
        run_decode("finish")


def _ws_matmul(x, w, layer, col0, n_cols, tm, tn, mode, name, *, transposed=False, f_row=None, pages=None,
               dec=None):
    m, k = x.shape
    assert n_cols % tn == 0 and (f_row is None or (n_cols == tn and transposed))
    n_j, n_i = n_cols // tn, m // tm
    if transposed:
        w_spec = pl.BlockSpec((pl.Element(1), pl.Element(tn), pl.Element(k)),
                              lambda j, i, *_: (layer, pl.multiple_of(col0 + j * tn, 8), 0))
    else:
        assert col0 % tn == 0
        w_spec = pl.BlockSpec((None, k, tn), lambda j, i, *_: (layer, 0, col0 // tn + j))
    in_specs = [pl.BlockSpec((tm, k), lambda j, i, *_: (i, 0)), w_spec]
    args = [x, w]
    out_specs = [pl.BlockSpec((tm, tn), lambda j, i, *_: (i, j))]
    out_shape = [jax.ShapeDtypeStruct((m, n_cols), F32 if mode == "f32" else BF16)]
    scratch = [pltpu.VMEM((tn, k) if transposed else (k, tn), BF16)]
    if f_row is not None:
        in_specs.append(pl.BlockSpec((pl.Element(1), pl.Element(8), pl.Element(k)),
                                     lambda j, i, *_: (layer, f_row, 0)))
        args.append(w)
        out_specs.append(pl.BlockSpec((tm, 128), lambda j, i, *_: (i, 0)))
        out_shape.append(jax.ShapeDtypeStruct((m, 128), F32))
        scratch.append(pltpu.VMEM((128, k), BF16))
    aliases = {}
    n_heads = page = 0
    has_prev = False
    if mode == "kv":
        prev, bsz, t, depth, n_heads, page = pages
        assert n_cols == tn == n_heads * HEAD_DIM and t % tm == 0 and tm % page == 0
        tps = t // tm
        out_specs.append(pl.BlockSpec((1, tm // page, None, page * n_heads, HEAD_DIM),
                                      lambda j, i, *_: (i // tps, i % tps, layer, 0, 0)))
        out_shape.append(jax.ShapeDtypeStruct((bsz, t // page, depth, page * n_heads, HEAD_DIM), F32))
        if prev is not None:
            has_prev = True
            in_specs.append(pl.BlockSpec(memory_space=pl.ANY))
            args.append(prev)
            aliases = {2: 1}
    n_prefetch = 0
    if dec is not None:
        assert dec.n_items == n_j * n_i
        dec_specs, dec_args, dec_out_spec, dec_out_shape, dec_scratch = dec.operands(lambda j, i: j * n_i + i)
        in_specs += dec_specs
        args = [dec.page_table] + args + dec_args
        out_specs.append(dec_out_spec)
        out_shape.append(dec_out_shape)
        scratch += dec_scratch
        n_prefetch = 1
    grid_spec = pltpu.PrefetchScalarGridSpec(
        num_scalar_prefetch=n_prefetch, grid=(n_j, n_i), in_specs=in_specs, out_specs=out_specs,
        scratch_shapes=scratch)
    out = pl.pallas_call(
        functools.partial(_ws_kernel, mode=mode, transposed=transposed, with_f=f_row is not None,
                          has_prev=has_prev, dec=None if dec is None else dec.static, n_heads=n_heads, page=page,
                          q_scale=HEAD_DIM ** -0.5 * LOG2E),
        grid_spec=grid_spec,
        out_shape=out_shape,
        input_output_aliases=aliases,
        compiler_params=_params(("arbitrary", "arbitrary"), 58),
        name=name,
    )(*args)
    return out if len(out) > 1 else out[0]


def _wo_kernel(a_ref, s_ref, c_ref, w_ref, x_ref, g1_ref, g2_ref, x1_ref, xn_ref, *, d_a, d_b):
    mixed = _dot(a_ref[...], w_ref[0:d_a, :])
    mixed += _dot(s_ref[...], w_ref[d_a:d_a + d_b, :])
    mixed += _dot(c_ref[...], w_ref[d_a + d_b:, :])
    x1 = x_ref[...] + _rms(mixed, g1_ref[...])
    x1_ref[...] = x1
    xn_ref[...] = _rms(x1, g2_ref[...]).astype(xn_ref.dtype)


def _wo_block(attn, sgu, conv, w_o, layer, x, g_post, g_pre_mlp, tm):
    m, d = x.shape
    d_a, d_b, d_c = attn.shape[1], sgu.shape[1], conv.shape[1]
    row = lambda width: pl.BlockSpec((tm, width), lambda i: (i, 0))
    const = lambda shape: pl.BlockSpec(shape, lambda i: (0, 0))
    w_spec = pl.BlockSpec((None, d, d), lambda i: (layer, 0, 0))
    return pl.pallas_call(
        functools.partial(_wo_kernel, d_a=d_a, d_b=d_b),
        grid=(m // tm,),
        in_specs=[row(d_a), row(d_b), row(d_c), w_spec, row(d), const((1, d)), const((1, d))],
        out_specs=[row(d), row(d)],
        out_shape=[jax.ShapeDtypeStruct((m, d), F32), jax.ShapeDtypeStruct((m, d), BF16)],
        compiler_params=_params(("arbitrary",), 48),
        name="wo_norm_residual",
    )(attn, sgu, conv, w_o, x, g_post, g_pre_mlp)


def _down_kernel(*refs, emit_next, dec):
    refs = list(refs)
    if dec is not None:
        refs = refs[1:]
    h_ref, w_ref, x1_ref, g_ref = refs[:4]
    pos = 4
    if emit_next:
        gn_ref = refs[pos]
        pos += 1
    if dec is not None:
        dec_in = refs[pos:pos + 2 + 3 * dec.n_grp]
        pos += 2 + 3 * dec.n_grp
    x2_ref = refs[pos]
    pos += 1
    if emit_next:
        xn_ref = refs[pos]
        pos += 1
    if dec is not None:
        dec_out = refs[pos]
        pos += 1
    acc_ref = refs[pos]
    dec_state = refs[pos + 1:]
    k = pl.program_id(1)

    @pl.when(k == 0)
    def _():
        acc_ref[...] = jnp.zeros_like(acc_ref)

    if dec is not None:
        dec_p = lax.rem(pl.program_id(0) * pl.num_programs(1) + k, dec.n_steps)
        run_decode = functools.partial(_decode_step, p=dec_p, dec=dec, in_refs=dec_in, o_ref=dec_out,
                                       state=dec_state)
        run_decode("init")
        dec_carry = run_decode("scores")
    acc_ref[...] += _dot(h_ref[...], w_ref[...])
    if dec is not None:
        run_decode("values", carry=dec_carry)

    @pl.when(k == pl.num_programs(1) - 1)
    def _():
        x2 = x1_ref[...] + _rms(acc_ref[...], g_ref[...])
        x2_ref[...] = x2
        if emit_next:
            xn_ref[...] = _rms(x2, gn_ref[...]).astype(xn_ref.dtype)

    if dec is not None:
        run_decode("finish")


def _down_block(hid, w_down, layer, x1, g_post, g_next, tm, tk, dec=None):
    m, d = x1.shape
    kdim = hid.shape[1]
    n_i, n_k = m // tm, kdim // tk
    emit_next = g_next is not None
    row = pl.BlockSpec((tm, d), lambda i, k, *_: (i, 0))
    const = pl.BlockSpec((1, d), lambda i, k, *_: (0, 0))
    in_specs = [pl.BlockSpec((tm, tk), lambda i, k, *_: (i, k)),
                pl.BlockSpec((None, tk, d), lambda i, k, *_: (layer, k, 0)), row, const]
    args = [hid, w_down, x1, g_post]
    out_specs = [row]
    out_shape = [jax.ShapeDtypeStruct((m, d), F32)]
    scratch = [pltpu.VMEM((tm, d), F32)]
    if emit_next:
        in_specs.append(const)
        args.append(g_next)
        out_specs.append(row)
        out_shape.append(jax.ShapeDtypeStruct((m, d), BF16))
    n_prefetch = 0
    if dec is not None:
        assert dec.n_items == n_i * n_k
        dec_specs, dec_args, dec_out_spec, dec_out_shape, dec_scratch = dec.operands(lambda i, k: i * n_k + k)
        in_specs += dec_specs
        args = [dec.page_table] + args + dec_args
        out_specs.append(dec_out_spec)
        out_shape.append(dec_out_shape)
        scratch += dec_scratch
        n_prefetch = 1
    grid_spec = pltpu.PrefetchScalarGridSpec(
        num_scalar_prefetch=n_prefetch, grid=(n_i, n_k), in_specs=in_specs, out_specs=out_specs,
        scratch_shapes=scratch)
    out = pl.pallas_call(
        functools.partial(_down_kernel, emit_next=emit_next, dec=None if dec is None else dec.static),
        grid_spec=grid_spec,
        out_shape=out_shape,
        compiler_params=_params(("arbitrary", "arbitrary"), 58),
        name="down_norm_residual",
    )(*args)
    return out[0], (out[1] if emit_next else None), (out[-1] if dec is not None else None)


def _prefix_kernel(zf_ref, bf_ref, logf_ref, cq_ref, ct_ref, c_sc, *, t, n_heads):
    lf = _log_sigmoid(zf_ref[0] + bf_ref[...])
    logf_ref[0] = lf[:, 0:n_heads]
    r_i = lax.broadcasted_iota(jnp.int32, (CHUNK, CHUNK), 0)
    c_i = lax.broadcasted_iota(jnp.int32, (CHUNK, CHUNK), 1)
    tri = _ones_where(r_i >= c_i)
    carry = jnp.zeros((1, 128), F32)
    for blk in range(t // CHUNK):
        cb = _dot3_left(tri, lf[blk * CHUNK:(blk + 1) * CHUNK, :]) + carry
        c_sc[blk * CHUNK:(blk + 1) * CHUNK, :] = cb * LOG2E
        carry = cb[CHUNK - 1:CHUNK, :]
    c = c_sc[...]
    ct_ref[0] = c.T[0:n_heads, :]
    for h in range(n_heads):
        cq_ref[0, h] = c[:, h:h + 1]


def _fox_prefix(z3, bf_pad, n_heads, f_blk):
    b, t, _ = z3.shape
    return pl.pallas_call(
        functools.partial(_prefix_kernel, t=t, n_heads=n_heads),
        grid=(b,),
        in_specs=[pl.BlockSpec((1, t, 128), lambda i: (i, 0, f_blk)), pl.BlockSpec((1, 128), lambda i: (0, 0))],
        out_specs=[
            pl.BlockSpec((1, t, n_heads), lambda i: (i, 0, 0)),
            pl.BlockSpec((1, n_heads, t, 1), lambda i: (i, 0, 0, 0)),
            pl.BlockSpec((1, n_heads, t), lambda i: (i, 0, 0)),
        ],
        out_shape=[
            jax.ShapeDtypeStruct((b, t, n_heads), F32),
            jax.ShapeDtypeStruct((b, n_heads, t, 1), F32),
            jax.ShapeDtypeStruct((b, n_heads, t), F32),
        ],
        scratch_shapes=[pltpu.VMEM((t, 128), F32)],
        compiler_params=_params(("arbitrary",), 48),
        name="fox_prefix",
    )(z3, bf_pad)


def _fox_attn_kernel(qi_ref, ki_ref, q_ref, k_ref, v_ref, cq_ref, ck_ref, o_ref, m_sc, l_sc, acc_sc, cq_sc,
                     *, tq, n_heads):
    step = pl.program_id(1)
    qi = qi_ref[step]
    ki = ki_ref[step]
    n_sub = tq // 128

    @pl.when(ki == 0)
    def _():
        m_sc[...] = jnp.full_like(m_sc, -jnp.inf)
        l_sc[...] = jnp.zeros_like(l_sc)
        acc_sc[...] = jnp.zeros_like(acc_sc)
        for h in range(n_heads):
            cq_sc[h] = jnp.broadcast_to(cq_ref[0, h], (tq, 128))

    def block(diagonal):
        if diagonal:
            keep = (lax.broadcasted_iota(jnp.int32, (tq, tq), 0) >= lax.broadcasted_iota(jnp.int32, (tq, tq), 1))
        for h in range(n_heads):
            hs = slice(h * HEAD_DIM, (h + 1) * HEAD_DIM)
            s = _dot_nt(q_ref[0, :, hs], k_ref[0, :, hs]) - ck_ref[0, h:h + 1, :]
            if diagonal:
                s = jnp.where(keep, s, -jnp.inf)
            subs = [s[:, j * 128:(j + 1) * 128] for j in range(n_sub)]
            mc = subs[0]
            for x in subs[1:]:
                mc = jnp.maximum(mc, x)
            cq = cq_sc[h]
            m_prev = m_sc[h]
            m_new = jnp.maximum(m_prev, jnp.max(mc, axis=1, keepdims=True) + cq)
            alpha = jnp.exp2(m_prev - m_new)
            shift = m_new - cq
            ps = [jnp.exp2(x - shift) for x in subs]
            lsum = ps[0]
            for x in ps[1:]:
                lsum = lsum + x
            l_sc[h] = alpha * l_sc[h] + jnp.sum(lsum, axis=1, keepdims=True)
            p = jnp.concatenate(ps, axis=1).astype(BF16)
            acc_sc[:, hs] = alpha * acc_sc[:, hs] + _dot(p, v_ref[0, :, hs])
            m_sc[h] = m_new

    @pl.when(ki < qi)
    def _():
        block(False)

    @pl.when(ki == qi)
    def _():
        block(True)
        for h in range(n_heads):
            hs = slice(h * HEAD_DIM, (h + 1) * HEAD_DIM)
            o_ref[0, :, hs] = (acc_sc[:, hs] / l_sc[h]).astype(o_ref.dtype)


def _fox_attn_prompt(q, k, v, cq, ct, n_heads, tq):
    b, t, d_a = q.shape
    nq = t // tq
    pairs = [(i, j) for i in range(nq) for j in range(i + 1)]
    qi_tab = jnp.asarray([pr[0] for pr in pairs], jnp.int32)
    ki_tab = jnp.asarray([pr[1] for pr in pairs], jnp.int32)
    q_blk = pl.BlockSpec((1, tq, d_a), lambda bi, s, qt, kt: (bi, qt[s], 0))
    kv_blk = pl.BlockSpec((1, tq, d_a), lambda bi, s, qt, kt: (bi, kt[s], 0))
    grid_spec = pltpu.PrefetchScalarGridSpec(
        num_scalar_prefetch=2,
        grid=(b, len(pairs)),
        in_specs=[
            q_blk, kv_blk, kv_blk,
            pl.BlockSpec((1, n_heads, tq, 1), lambda bi, s, qt, kt: (bi, 0, qt[s], 0)),
            pl.BlockSpec((1, n_heads, tq), lambda bi, s, qt, kt: (bi, 0, kt[s])),
        ],
        out_specs=q_blk,
        scratch_shapes=[
            pltpu.VMEM((n_heads, tq, 128), F32),
            pltpu.VMEM((n_heads, tq, 128), F32),
            pltpu.VMEM((tq, d_a), F32),
            pltpu.VMEM((n_heads, tq, 128), F32),
        ],
    )
    return pl.pallas_call(
        functools.partial(_fox_attn_kernel, tq=tq, n_heads=n_heads),
        grid_spec=grid_spec,
        out_shape=jax.ShapeDtypeStruct((b, t, d_a), BF16),
        compiler_params=_params(("arbitrary", "arbitrary"), 48),
        name="fox_attn_prompt",
    )(qi_tab, ki_tab, q, k, v, cq, ct)


def _layernorm_silu(y, g, b):
    mu = jnp.mean(y, axis=-1, keepdims=True)
    yc = y - mu
    yn = yc * lax.rsqrt(jnp.mean(yc * yc, axis=-1, keepdims=True) + EPS) * g + b
    return yn * _sigmoid(yn)


def _mixer_kernel(ub_ref, vb_ref, ac_ref, gc_ref, gv_ref, ws_ref, bst_ref, cw_ref, cb_ref, lg_ref, lb_ref,
                  sgu_ref, conv_ref, tail_ref, g_sc, *, tm, n_heads_b, conv_w, halo):
    ti = pl.program_id(1)

    u = _gelu(ub_ref[0])
    vn = _rms(_gelu(vb_ref[0]), gv_ref[...])
    r_i = lax.broadcasted_iota(jnp.int32, (CHUNK, CHUNK), 0)
    c_i = lax.broadcasted_iota(jnp.int32, (CHUNK, CHUNK), 1)
    for h in range(n_heads_b):
        hs = slice(h * HEAD_DIM, (h + 1) * HEAD_DIM)
        w_h = jnp.where(r_i >= c_i, ws_ref[h], 0.0).astype(BF16)
        bias_h = bst_ref[:, h:h + 1]
        for c in range(tm // CHUNK):
            rs = slice(c * CHUNK, (c + 1) * CHUNK)
            mix = _dot(w_h, vn[rs, hs].astype(BF16)) + bias_h
            sgu_ref[0, rs, hs] = (u[rs, hs] * mix).astype(sgu_ref.dtype)

    @pl.when(ti == 0)
    def _():
        g_sc[0:halo, :] = jnp.zeros((halo, g_sc.shape[1]), F32)

    g_sc[halo:halo + tm, :] = ac_ref[0] * _sigmoid(gc_ref[0])
    base = halo - (conv_w - 1)
    y = g_sc[base:base + tm, :] * cw_ref[0:1, :]
    for k in range(1, conv_w):
        y += g_sc[base + k:base + k + tm, :] * cw_ref[k:k + 1, :]
    y += cb_ref[...]
    conv_ref[0] = _layernorm_silu(y, lg_ref[...], lb_ref[...]).astype(conv_ref.dtype)

    @pl.when(ti == pl.num_programs(1) - 1)
    def _():
        tail_ref[0] = g_sc[halo + tm - (conv_w - 1):halo + tm, :]

    g_sc[0:halo, :] = g_sc[tm:tm + halo, :]


def _mixer_prompt(z3, g_v, w_s, b_s_t, conv_w, conv_b, ln_g, ln_b, tm):
    b, t, _ = z3.shape
    n_heads_b = w_s.shape[0]
    d_b = n_heads_b * HEAD_DIM
    d_c = conv_w.shape[1]
    kw = conv_w.shape[0]
    halo = 32
    zcol = lambda c: pl.BlockSpec((1, tm, d_b), lambda bi, ti: (bi, ti, c))
    const = lambda shape: pl.BlockSpec(shape, lambda bi, ti: (0,) * len(shape))
    return pl.pallas_call(
        functools.partial(_mixer_kernel, tm=tm, n_heads_b=n_heads_b, conv_w=kw, halo=halo),
        grid=(b, t // tm),
        in_specs=[zcol(0), zcol(1), zcol(2), zcol(3),
                  const((1, d_b)), const((n_heads_b, CHUNK, CHUNK)), const((CHUNK, n_heads_b)),
                  const((kw, d_c)), const((1, d_c)), const((1, d_c)), const((1, d_c))],
        out_specs=[
            pl.BlockSpec((1, tm, d_b), lambda bi, ti: (bi, ti, 0)),
            pl.BlockSpec((1, tm, d_c), lambda bi, ti: (bi, ti, 0)),
            pl.BlockSpec((1, kw - 1, d_c), lambda bi, ti: (bi, 0, 0)),
        ],
        out_shape=[
            jax.ShapeDtypeStruct((b, t, d_b), BF16),
            jax.ShapeDtypeStruct((b, t, d_c), BF16),
            jax.ShapeDtypeStruct((b, kw - 1, d_c), F32),
        ],
        scratch_shapes=[pltpu.VMEM((halo + tm, d_c), F32)],
        compiler_params=_params(("arbitrary", "arbitrary"), 32),
        name="mixer_prompt",
    )(z3, z3, z3, z3, g_v, w_s, b_s_t, conv_w, conv_b, ln_g, ln_b)


def _mixer_sample_kernel(zs_ref, zf_ref, bf_ref, gv_ref, w0_ref, b0_ref, st_ref, cw_ref, cb_ref, lg_ref, lb_ref,
                         sgu_ref, conv_ref, state_ref, vn_ref, logf_ref, y_sc, *, d_b, conv_w):
    n = zs_ref.shape[0]
    u = _gelu(zs_ref[:, 0:d_b])
    vn = _rms(_gelu(zs_ref[:, d_b:2 * d_b]), gv_ref[...])
    vn_ref[...] = vn
    sgu_ref[...] = (u * (w0_ref[...] * vn + b0_ref[...])).astype(sgu_ref.dtype)
    glu = zs_ref[:, 2 * d_b:3 * d_b] * _sigmoid(zs_ref[:, 3 * d_b:4 * d_b])
    kw = conv_w - 1
    for bi in range(n):
        g_new = glu[bi:bi + 1, :]
        y_sc[bi:bi + 1, :] = (jnp.sum(st_ref[bi] * cw_ref[0:kw, :], axis=0, keepdims=True)
                              + g_new * cw_ref[kw:kw + 1, :])
        state_ref[bi, 0:kw - 1, :] = st_ref[bi, 1:kw, :]
        state_ref[bi, kw - 1:kw, :] = g_new
    conv_ref[...] = _layernorm_silu(y_sc[...] + cb_ref[...], lg_ref[...], lb_ref[...]).astype(conv_ref.dtype)
    logf_ref[...] = _log_sigmoid(zf_ref[...] + bf_ref[...])


def _mixer_sample(zs, zf, bf_pad, g_v, w0_row, b0_row, state, conv_w, conv_b, ln_g, ln_b):
    n = zs.shape[0]
    d_b = g_v.shape[1]
    kw, d_c = conv_w.shape
    full = lambda shape: pl.BlockSpec(shape, lambda i: (0,) * len(shape))
    args = (zs, zf, bf_pad, g_v, w0_row, b0_row, state, conv_w, conv_b, ln_g, ln_b)
    out_shape = [
        jax.ShapeDtypeStruct((n, d_b), BF16),
        jax.ShapeDtypeStruct((n, d_c), BF16),
        jax.ShapeDtypeStruct((n, kw - 1, d_c), F32),
        jax.ShapeDtypeStruct((n, d_b), F32),
        jax.ShapeDtypeStruct((n, 128), F32),
    ]
    return pl.pallas_call(
        functools.partial(_mixer_sample_kernel, d_b=d_b, conv_w=kw),
        grid=(1,),
        in_specs=[full(a.shape) for a in args],
        out_specs=[full(s.shape) for s in out_shape],
        out_shape=out_shape,
        scratch_shapes=[pltpu.VMEM((n, d_c), F32)],
        compiler_params=_params(("arbitrary",), 32),
        name="mixer_sample",
    )(*args)


def _decode_step(phase, p, dec, in_refs, o_ref, state, carry=None):
    n_grp, n_heads, scale = dec.n_grp, dec.n_heads, dec.scale
    zs_ref, cn_ref = in_refs[0], in_refs[1]
    refs = in_refs[2:]
    k_refs, v_refs, lf_refs = refs[:n_grp], refs[n_grp:2 * n_grp], refs[2 * n_grp:3 * n_grp]
    cn_sc, m_sc, l_sc, acc_sc, tail_sc = state
    rows_pp = k_refs[0].shape[2]
    n_flat = rows_pp // 128
    lane = lax.broadcasted_iota(jnp.int32, (n_heads, 128), 1)
    sub = lax.broadcasted_iota(jnp.int32, (n_heads, 128), 0)
    cls_mask = n_heads - 1
    own = sub == (lane & cls_mask)
    diag = sub == lane
    r_i = lax.broadcasted_iota(jnp.int32, (128, 128), 0)
    c_i = lax.broadcasted_iota(jnp.int32, (128, 128), 1)
    same = (r_i & cls_mask) == (c_i & cls_mask)

    if phase == "init":
        @pl.when(p == 0)
        def _():
            spread = _ones_where((r_i < n_heads) & ((c_i & cls_mask) == r_i))
            cn_sc[...] = _dot3(jnp.broadcast_to(cn_ref[0], (n_heads, 128)), spread)
            m_sc[...] = jnp.full_like(m_sc, -jnp.inf)
            l_sc[...] = jnp.zeros_like(l_sc)
            acc_sc[...] = jnp.zeros_like(acc_sc)
            tail_sc[...] = jnp.zeros_like(tail_sc)
        return

    if phase == "finish":
        @pl.when(p == dec.n_steps - 1)
        def _():
            m_col = jnp.max(jnp.where(diag, m_sc[...], -jnp.inf), axis=1, keepdims=True)
            l_row = jnp.sum(l_sc[...], axis=0, keepdims=True)
            l_col = jnp.sum(jnp.where(own, jnp.broadcast_to(l_row, (n_heads, 128)), 0.0), axis=1, keepdims=True)
            cn_col = jnp.sum(jnp.where(diag, cn_sc[...], 0.0), axis=1, keepdims=True)
            q_new = zs_ref[0, 0:n_heads, :]
            k_new = zs_ref[0, n_heads:2 * n_heads, :]
            v_new = zs_ref[0, 2 * n_heads:3 * n_heads, :]
            s_new = jnp.sum(q_new * k_new, axis=1, keepdims=True) * scale + cn_col - cn_col
            m_f = jnp.maximum(m_col, s_new)
            a_f = jnp.exp(m_col - m_f)
            p_new = jnp.exp(s_new - m_f)
            o_ref[0] = (a_f * acc_sc[...] + p_new * v_new) / (a_f * l_col + p_new)
        return

    if phase == "values":
        return _decode_values(dec, carry, v_refs, state, own, diag, n_flat)

    lf = jnp.concatenate([lf_refs[g][0, 0] for g in range(n_grp)], axis=0)
    n_rows = n_grp * n_flat
    row_tot = _dot3(lf, _ones_where(same))
    within = _dot3(lf, _ones_where(same & (r_i > c_i)))
    rr = lax.broadcasted_iota(jnp.int32, (n_rows, n_rows), 0)
    cc = lax.broadcasted_iota(jnp.int32, (n_rows, n_rows), 1)
    suffix = within + _dot3_left(_ones_where(cc > rr), row_tot) + tail_sc[0:1, :]

    q8 = zs_ref[0, 0:n_heads, :].astype(BF16)
    scores = []
    for g in range(n_grp):
        s_t = _dot_nt(q8, k_refs[g][0, 0].astype(BF16))
        flat = [jnp.sum(jnp.where(own, s_t[:, a * 128:(a + 1) * 128], 0.0), axis=0, keepdims=True)
                for a in range(n_flat)]
        scores.append(jnp.concatenate(flat, axis=0) * scale
                      + suffix[g * n_flat:(g + 1) * n_flat, :] + cn_sc[...])
    return scores, row_tot


def _decode_values(dec, carry, v_refs, state, own, diag, n_flat):
    n_grp, n_heads = dec.n_grp, dec.n_heads
    cn_sc, m_sc, l_sc, acc_sc, tail_sc = state
    scores, row_tot = carry
    m_step = scores[0]
    for x in scores[1:]:
        m_step = jnp.maximum(m_step, x)
    shift = 1
    while shift < n_flat:
        m_step = jnp.maximum(m_step, pltpu.roll(m_step, shift, 0))
        shift *= 2
    shift = n_heads
    while shift < 128:
        m_step = jnp.maximum(m_step, pltpu.roll(m_step, shift, 1))
        shift *= 2
    m_prev = m_sc[...]
    m_new = jnp.maximum(m_prev, m_step)
    alpha = jnp.exp(m_prev - m_new)
    probs = [jnp.exp(x - m_new) for x in scores]
    l_step = probs[0]
    for x in probs[1:]:
        l_step = l_step + x
    l_sc[...] = alpha * l_sc[...] + l_step
    pv = jnp.zeros((n_heads, HEAD_DIM), F32)
    for g in range(n_grp):
        blocks = [jnp.where(own, jnp.broadcast_to(probs[g][a:a + 1, :], (n_heads, 128)), 0.0)
                  for a in range(n_flat)]
        pv += _dot(jnp.concatenate(blocks, axis=1).astype(BF16), v_refs[g][0, 0].astype(BF16))
    alpha_col = jnp.sum(jnp.where(diag, alpha, 0.0), axis=1, keepdims=True)
    acc_sc[...] = alpha_col * acc_sc[...] + pv
    m_sc[...] = m_new
    tail_sc[...] += jnp.sum(row_tot, axis=0, keepdims=True)


class _DecodeStatic(NamedTuple):
    n_grp: int
    n_steps: int
    n_items: int
    n_heads: int
    scale: float


class _DecodePlan:
    def __init__(self, page_table, zs3, cn, cache_k, cache_v, cache_lf, layer, n_heads, b0, nb, n_grp):
        n_pages = page_table.shape[1]
        assert n_pages % n_grp == 0 and cache_k.shape[2] == 128 * n_heads and cache_lf.shape[2] == n_heads
        assert n_heads & (n_heads - 1) == 0 and n_heads <= 8
        self.page_table, self.layer, self.b0, self.nb, self.n_pages = page_table, layer, b0, nb, n_pages
        self.arrays = (zs3, cn, cache_k, cache_v, cache_lf)
        n_steps = n_pages // n_grp
        self.static = _DecodeStatic(n_grp, n_steps, nb * n_steps, n_heads, HEAD_DIM ** -0.5)
        self.n_items = nb * n_steps

    def operands(self, step_of):
        zs3, cn, cache_k, cache_v, cache_lf = self.arrays
        st, b0, layer, n_pages = self.static, self.b0, self.layer, self.n_pages
        n_heads, n_grp, n_steps = st.n_heads, st.n_grp, st.n_steps
        rows_pp = cache_k.shape[2]
        seq = lambda g0, g1: step_of(g0, g1) // n_steps

        def page_spec(block, g):
            def index(g0, g1, pt):
                s = step_of(g0, g1)
                return (pt[b0 + s // n_steps, n_pages - (s % n_steps + 1) * n_grp + g], layer, 0, 0)
            return pl.BlockSpec(block, index)

        kv_specs = [page_spec((1, 1, rows_pp, HEAD_DIM), g) for g in range(n_grp)]
        lf_specs = [page_spec((1, 1, n_heads, 128), g) for g in range(n_grp)]
        in_specs = [pl.BlockSpec((1, 3 * n_heads, HEAD_DIM), lambda g0, g1, pt: (b0 + seq(g0, g1), 0, 0)),
                    pl.BlockSpec((1, 1, 128), lambda g0, g1, pt: (b0 + seq(g0, g1), 0, 0))]
        in_specs += kv_specs + kv_specs + lf_specs
        args = [zs3, cn] + [cache_k] * n_grp + [cache_v] * n_grp + [cache_lf] * n_grp
        out_spec = pl.BlockSpec((1, n_heads, HEAD_DIM), lambda g0, g1, pt: (seq(g0, g1), 0, 0))
        out_shape = jax.ShapeDtypeStruct((self.nb, n_heads, HEAD_DIM), F32)
        scratch = [pltpu.VMEM((n_heads, 128), F32)] * 3 + [pltpu.VMEM((n_heads, HEAD_DIM), F32),
                                                           pltpu.VMEM((n_heads, 128), F32)]
        return in_specs, args, out_spec, out_shape, scratch


def kernel(x_prompt, x_sample, cache_k, cache_v, cache_logf, state_conv, page_table, w_in, b_f, g_v, w_s, b_s,
           conv_w, conv_b, ln_c_g, ln_c_b, w_o, g_pre_mix, g_post_mix, g_pre_mlp, g_post_mlp, w_up, w_down):
    bp, tp, d = x_prompt.shape
    bs, ds, _ = x_sample.shape
    assert ds == 1, "the sample path handles exactly one new token per sequence"
    depth = w_in.shape[0]
    n_pool, _, page, n_heads, _ = cache_k.shape
    d_a = n_heads * HEAD_DIM
    n_heads_b = w_s.shape[1]
    d_b = n_heads_b * HEAD_DIM
    d_c = conv_w.shape[2]
    off_f = 3 * d_a
    assert w_in.shape[2] == off_f + n_heads + 2 * d_b + 2 * d_c and d_b == d_c and page == CHUNK
    mp = bp * tp

    n_rest = 2 * d_b + 2 * d_c
    w_in_t = jnp.swapaxes(w_in, 1, 2)
    w_o_b, w_down_b = w_o.astype(BF16), w_down.astype(BF16)
    d_ff = w_up.shape[2]
    bf_pad = jnp.pad(b_f, ((0, 0), (0, 128 - n_heads)))
    row2 = lambda a, l: a[l][None, :]
    proj = functools.partial(_ws_matmul, transposed=True)

    n_pages = page_table.shape[1]
    assert bs % 2 == 0
    steps_up = (d_ff // UP_TILE[1]) * (mp // UP_TILE[0])
    steps_down = (mp // DOWN_TILE[0]) * (d_ff // DOWN_TILE[1])
    grp_up = (bs // 2) * n_pages // steps_up
    grp_down = (bs // 2) * n_pages // steps_down

    cache_k2 = cache_k.reshape(n_pool, depth, page * n_heads, HEAD_DIM)
    cache_v2 = cache_v.reshape(n_pool, depth, page * n_heads, HEAD_DIM)
    cache_lf = cache_logf.reshape(n_pool, depth, page * n_heads // 128, 128)

    xp = x_prompt.reshape(mp, d)
    xs = x_sample.reshape(bs, d)
    xpn = _norm_bf16(xp, row2(g_pre_mix, 0), 512)
    xsn = _norm_bf16(xs, row2(g_pre_mix, 0), bs)
    fp_l, cp_l, ks_l, vs_l, fs_l, cs_l, us_l = ([] for _ in range(7))
    k_pages = v_pages = None
    for l in range(depth):
        g_next = row2(g_pre_mix, l + 1) if l + 1 < depth else None
        q, zf = proj(xpn, w_in_t, l, 0, d_a, 1024, d_a, "q", "proj_q", f_row=off_f)
        kb, k_pages = proj(xpn, w_in_t, l, d_a, d_a, 1024, d_a, "kv", "proj_k",
                           pages=(k_pages, bp, tp, depth, n_heads, page))
        vb, v_pages = proj(xpn, w_in_t, l, 2 * d_a, d_a, 1024, d_a, "kv", "proj_v",
                           pages=(v_pages, bp, tp, depth, n_heads, page))
        z3 = proj(xpn, w_in_t, l, off_f + n_heads, n_rest, 1024, 1024, "f32", "proj_rest").reshape(bp, tp, n_rest)
        logf, cq, ct = _fox_prefix(zf.reshape(bp, tp, 128), bf_pad[l][None, :], n_heads, 0)
        attn = _fox_attn_prompt(q.reshape(bp, tp, d_a), kb.reshape(bp, tp, d_a), vb.reshape(bp, tp, d_a),
                                cq, ct, n_heads, 512)
        sgu, conv, conv_tail = _mixer_prompt(z3, row2(g_v, l), w_s[l], b_s[l].T, conv_w[l], row2(conv_b, l),
                                             row2(ln_c_g, l), row2(ln_c_b, l), 256)
        x1, x1n = _wo_block(attn.reshape(mp, d_a), sgu.reshape(mp, d_b), conv.reshape(mp, d_c), w_o_b, l, xp,
                            row2(g_post_mix, l), row2(g_pre_mlp, l), 256)
        fp_l.append(logf)
        cp_l.append(conv_tail)
        zs_q, zs_f = proj(xsn, w_in_t, l, 0, d_a, bs, d_a, "f32", "proj_q_sample", f_row=off_f)
        zs_kv = proj(xsn, w_in_t, l, d_a, 2 * d_a, bs, d_a, "f32", "proj_kv_sample")
        zs_rest = proj(xsn, w_in_t, l, off_f + n_heads, n_rest, bs, 1024, "f32", "proj_rest_sample")
        w0_row = jnp.repeat(w_s[l][:, 0, 0], HEAD_DIM)[None, :]
        b0_row = jnp.repeat(b_s[l][:, 0], HEAD_DIM)[None, :]
        sgu_s, conv_s, state_s, vn_s, logf_s = _mixer_sample(
            zs_rest, zs_f, bf_pad[l][None, :], row2(g_v, l), w0_row, b0_row, state_conv[l], conv_w[l],
            row2(conv_b, l), row2(ln_c_g, l), row2(ln_c_b, l))
        zs3 = jnp.concatenate([zs_q, zs_kv], axis=1).reshape(bs, 3 * n_heads, HEAD_DIM)
        plan = lambda b0, n_grp: _DecodePlan(page_table, zs3, logf_s[:, None, :], cache_k2, cache_v2, cache_lf,
                                             l, n_heads, b0, bs // 2, n_grp)
        hid, attn_s0 = _ws_matmul(x1n, w_up, l, 0, d_ff, UP_TILE[0], UP_TILE[1], "relu2", "mlp_up",
                                  dec=plan(0, grp_up))
        xp, xpn, attn_s1 = _down_block(hid, w_down_b, l, x1, row2(g_post_mlp, l), g_next, DOWN_TILE[0],
                                       DOWN_TILE[1], dec=plan(bs // 2, grp_down))
        attn_s = jnp.concatenate([attn_s0, attn_s1], axis=0).reshape(bs, d_a).astype(BF16)
        x1s, x1sn = _wo_block(attn_s, sgu_s, conv_s, w_o_b, l, xs, row2(g_post_mix, l), row2(g_pre_mlp, l), bs)
        hid_s = _ws_matmul(x1sn, w_up, l, 0, d_ff, bs, 1024, "relu2", "mlp_up_sample")
        xs, xsn, _ = _down_block(hid_s, w_down_b, l, x1s, row2(g_post_mlp, l), g_next, bs, 2048)
        ks_l.append(zs_kv[:, :d_a])
        vs_l.append(zs_kv[:, d_a:])
        fs_l.append(logf_s[:, :n_heads])
        cs_l.append(state_s)
        us_l.append(vn_s)

    npp = tp // page
    kv_out = lambda pages: pages.reshape(bp, npp, depth, page, n_heads, HEAD_DIM)
    logf_prompt = jnp.stack(fp_l, axis=1).reshape(bp, depth, npp, page, n_heads).swapaxes(1, 2)
    return (xp.reshape(bp, tp, d), xs.reshape(bs, ds, d),
            kv_out(k_pages), kv_out(v_pages), logf_prompt,
            jnp.stack(cp_l, axis=0),
            jnp.stack(ks_l, axis=1).reshape(bs, depth, ds, n_heads, HEAD_DIM),
            jnp.stack(vs_l, axis=1).reshape(bs, depth, ds, n_heads, HEAD_DIM),
            jnp.stack(fs_l, axis=1).reshape(bs, depth, ds, n_heads),
            jnp.stack(cs_l, axis=0),
            jnp.stack(us_l, axis=0).reshape(depth, bs, ds, d_b))
```

```python
import functools
from typing import NamedTuple

import numpy as np
import jax
import jax.numpy as jnp
from jax import lax
from jax.experimental import pallas as pl
from jax.experimental.pallas import tpu as pltpu

F32 = jnp.float32
BF16 = jnp.bfloat16

HEAD_DIM = 128
CHUNK = 128
EPS = 1e-6
MIB = 1024 * 1024
GELU_C = float(np.sqrt(2.0 / np.pi))
LOG2E = float(np.log2(np.e))

UP_TILE = (1024, 1024)
DOWN_TILE = (512, 1024)
MIXER_TM = 256


def _params(semantics, vmem_mib):
    return pltpu.CompilerParams(dimension_semantics=semantics, vmem_limit_bytes=vmem_mib * MIB)


def _rms(x, g):
    return x * lax.rsqrt(jnp.mean(x * x, axis=-1, keepdims=True) + EPS) * g


def _gelu(x):
    return x * (0.5 * (1.0 + jnp.tanh(GELU_C * (x + 0.044715 * (x * x * x)))))


def _sigmoid(x):
    return 1.0 / (1.0 + jnp.exp(-x))


def _log_sigmoid(x):
    return jnp.minimum(x, 0.0) - jnp.log1p(jnp.exp(-jnp.abs(x)))


def _split3(x):
    hi = x.astype(BF16)
    r = x - hi.astype(F32)
    mid = r.astype(BF16)
    lo = (r - mid.astype(F32)).astype(BF16)
    return hi, mid, lo


def _dot(a, b):
    return jnp.dot(a, b, preferred_element_type=F32)


def _dot_nt(a, b):
    return lax.dot_general(a, b, (((1,), (1,)), ((), ())), preferred_element_type=F32)


def _dot3(x, w):
    hi, mid, lo = _split3(x)
    return (_dot(hi, w) + _dot(mid, w)) + _dot(lo, w)


def _dot3_left(w, x):
    hi, mid, lo = _split3(x)
    return (_dot(w, hi) + _dot(w, mid)) + _dot(w, lo)


def _ones_where(cond):
    return jnp.where(cond, 1.0, 0.0).astype(BF16)


def _norm_kernel(x_ref, g_ref, o_ref):
    o_ref[...] = _rms(x_ref[...], g_ref[...]).astype(o_ref.dtype)


def _norm_bf16(x, g, tm):
    m, d = x.shape
    return pl.pallas_call(
        _norm_kernel,
        grid=(m // tm,),
        in_specs=[pl.BlockSpec((tm, d), lambda i: (i, 0)), pl.BlockSpec((1, d), lambda i: (0, 0))],
        out_specs=pl.BlockSpec((tm, d), lambda i: (i, 0)),
        out_shape=jax.ShapeDtypeStruct((m, d), BF16),
        compiler_params=_params(("arbitrary",), 32),
        name="pre_norm",
    )(x, g)


def _ws_kernel(*refs, mode, transposed, with_f, has_prev, side_cast, dec, n_heads, page, q_scale):
    refs = list(refs)
    if dec is not None:
        refs = refs[1:]
    x_ref, w_ref = refs[0], refs[1]
    pos = 2
    if with_f:
        wf_ref = refs[pos]
        pos += 1
    if has_prev:
        pos += 1
    if side_cast:
        side_in = refs[pos]
        pos += 1
    if dec is not None:
        dec_in = refs[pos:pos + 2 + 3 * dec.n_grp]
        pos += 2 + 3 * dec.n_grp
    n_out = 2 if (mode == "kv" or with_f) else 1
    outs = refs[pos:pos + n_out]
    pos += n_out
    if side_cast:
        refs[pos][...] = side_in[...].astype(BF16)
        pos += 1
    if dec is not None:
        dec_out = refs[pos]
        pos += 1
    w_sc = refs[pos]
    pos += 1
    if with_f:
        wf_sc = refs[pos]
        pos += 1
    dec_state = refs[pos:]

    @pl.when(pl.program_id(1) == 0)
    def _():
        w_sc[...] = (w_ref[0] if transposed else w_ref[...]).astype(BF16)
        if with_f:
            pad = jnp.zeros((wf_sc.shape[0] - wf_ref.shape[1], wf_sc.shape[1]), F32)
            wf_sc[...] = jnp.concatenate([wf_ref[0], pad], axis=0).astype(BF16)

    if dec is not None:
        dec_p = lax.rem(pl.program_id(0) * pl.num_programs(1) + pl.program_id(1), dec.n_steps)
        run_decode = functools.partial(_decode_step, p=dec_p, dec=dec, in_refs=dec_in, o_ref=dec_out,
                                       state=dec_state)
        run_decode("init")
        dec_carry = run_decode("scores")
    x = x_ref[...]
    acc = _dot_nt(x, w_sc[...]) if transposed else _dot(x, w_sc[...])
    if with_f:
        outs[1][...] = _dot_nt(x, wf_sc[...])
    if mode == "f32":
        outs[0][...] = acc
    elif mode == "q":
        outs[0][...] = (acc * q_scale).astype(BF16)
    elif mode == "relu2":
        h = jnp.maximum(acc, 0.0)
        outs[0][...] = (h * h).astype(BF16)
    else:
        outs[0][...] = acc.astype(BF16)
        dst_ref = outs[1]
        for pg in range(acc.shape[0] // page):
            for h in range(n_heads):
                dst_ref[0, pg, pl.ds(h, page, stride=n_heads), :] = (
                    acc[pg * page:(pg + 1) * page, h * HEAD_DIM:(h + 1) * HEAD_DIM])
    if dec is not None:
        run_decode("values", carry=dec_carry)
        run_decode("finish")


def _ws_matmul(x, w, layer, col0, n_cols, tm, tn, mode, name, *, transposed=False, f_row=None, pages=None,
               side_cast=None, dec=None):
    m, k = x.shape
    assert n_cols % tn == 0 and (f_row is None or (n_cols == tn and transposed))
    n_j, n_i = n_cols // tn, m // tm
    if transposed:
        w_spec = pl.BlockSpec((pl.Element(1), pl.Element(tn), pl.Element(k)),
                              lambda j, i, *_: (layer, pl.multiple_of(col0 + j * tn, 8), 0))
    else:
        assert col0 % tn == 0
        w_spec = pl.BlockSpec((None, k, tn), lambda j, i, *_: (layer, 0, col0 // tn + j))
    in_specs = [pl.BlockSpec((tm, k), lambda j, i, *_: (i, 0)), w_spec]
    args = [x, w]
    out_specs = [pl.BlockSpec((tm, tn), lambda j, i, *_: (i, j))]
    out_shape = [jax.ShapeDtypeStruct((m, n_cols), F32 if mode == "f32" else BF16)]
    scratch = [pltpu.VMEM((tn, k) if transposed else (k, tn), BF16)]
    if f_row is not None:
        in_specs.append(pl.BlockSpec((pl.Element(1), pl.Element(8), pl.Element(k)),
                                     lambda j, i, *_: (layer, f_row, 0)))
        args.append(w)
        out_specs.append(pl.BlockSpec((tm, 128), lambda j, i, *_: (i, 0)))
        out_shape.append(jax.ShapeDtypeStruct((m, 128), F32))
        scratch.append(pltpu.VMEM((128, k), BF16))
    aliases = {}
    n_heads = page = 0
    has_prev = False
    if mode == "kv":
        prev, bsz, t, depth, n_heads, page = pages
        assert n_cols == tn == n_heads * HEAD_DIM and t % tm == 0 and tm % page == 0
        tps = t // tm
        out_specs.append(pl.BlockSpec((1, tm // page, None, page * n_heads, HEAD_DIM),
                                      lambda j, i, *_: (i // tps, i % tps, layer, 0, 0)))
        out_shape.append(jax.ShapeDtypeStruct((bsz, t // page, depth, page * n_heads, HEAD_DIM), F32))
        if prev is not None:
            has_prev = True
            in_specs.append(pl.BlockSpec(memory_space=pl.ANY))
            args.append(prev)
            aliases = {2: 1}
    if side_cast is not None:
        _, side_r, side_c = side_cast.shape
        rows = side_r // (n_j * n_i)
        assert rows * n_j * n_i == side_r and rows % 16 == 0
        in_specs.append(pl.BlockSpec((None, rows, side_c), lambda j, i, *_: (layer, j * n_i + i, 0)))
        args.append(side_cast)
        out_specs.append(pl.BlockSpec((rows, side_c), lambda j, i, *_: (j * n_i + i, 0)))
        out_shape.append(jax.ShapeDtypeStruct((side_r, side_c), BF16))
    n_prefetch = 0
    if dec is not None:
        assert dec.n_items == n_j * n_i
        dec_specs, dec_args, dec_out_spec, dec_out_shape, dec_scratch = dec.operands(lambda j, i: j * n_i + i)
        in_specs += dec_specs
        args = [dec.page_table] + args + dec_args
        out_specs.append(dec_out_spec)
        out_shape.append(dec_out_shape)
        scratch += dec_scratch
        n_prefetch = 1
    grid_spec = pltpu.PrefetchScalarGridSpec(
        num_scalar_prefetch=n_prefetch, grid=(n_j, n_i), in_specs=in_specs, out_specs=out_specs,
        scratch_shapes=scratch)
    out = pl.pallas_call(
        functools.partial(_ws_kernel, mode=mode, transposed=transposed, with_f=f_row is not None,
                          has_prev=has_prev, side_cast=side_cast is not None,
                          dec=None if dec is None else dec.static, n_heads=n_heads, page=page,
                          q_scale=HEAD_DIM ** -0.5 * LOG2E),
        grid_spec=grid_spec,
        out_shape=out_shape,
        input_output_aliases=aliases,
        compiler_params=_params(("arbitrary", "arbitrary"), 58),
        name=name,
    )(*args)
    return out if len(out) > 1 else out[0]


def _wo_kernel(a_ref, s_ref, c_ref, w_ref, x_ref, g1_ref, g2_ref, x1_ref, xn_ref, *, d_a, d_b):
    mixed = _dot(a_ref[...], w_ref[0:d_a, :])
    mixed += _dot(s_ref[...], w_ref[d_a:d_a + d_b, :])
    mixed += _dot(c_ref[...], w_ref[d_a + d_b:, :])
    x1 = x_ref[...] + _rms(mixed, g1_ref[...])
    x1_ref[...] = x1
    xn_ref[...] = _rms(x1, g2_ref[...]).astype(xn_ref.dtype)


def _wo_block(attn, sgu, conv, w_o, layer, x, g_post, g_pre_mlp, tm):
    m, d = x.shape
    d_a, d_b, d_c = attn.shape[1], sgu.shape[1], conv.shape[1]
    row = lambda width: pl.BlockSpec((tm, width), lambda i: (i, 0))
    const = lambda shape: pl.BlockSpec(shape, lambda i: (0, 0))
    w_spec = pl.BlockSpec((None, d, d), lambda i: (layer, 0, 0))
    return pl.pallas_call(
        functools.partial(_wo_kernel, d_a=d_a, d_b=d_b),
        grid=(m // tm,),
        in_specs=[row(d_a), row(d_b), row(d_c), w_spec, row(d), const((1, d)), const((1, d))],
        out_specs=[row(d), row(d)],
        out_shape=[jax.ShapeDtypeStruct((m, d), F32), jax.ShapeDtypeStruct((m, d), BF16)],
        compiler_params=_params(("arbitrary",), 48),
        name="wo_norm_residual",
    )(attn, sgu, conv, w_o, x, g_post, g_pre_mlp)


def _down_kernel(*refs, emit_next, dec):
    refs = list(refs)
    if dec is not None:
        refs = refs[1:]
    h_ref, w_ref, x1_ref, g_ref = refs[:4]
    pos = 4
    if emit_next:
        gn_ref = refs[pos]
        pos += 1
    if dec is not None:
        dec_in = refs[pos:pos + 2 + 3 * dec.n_grp]
        pos += 2 + 3 * dec.n_grp
    x2_ref = refs[pos]
    pos += 1
    if emit_next:
        xn_ref = refs[pos]
        pos += 1
    if dec is not None:
        dec_out = refs[pos]
        pos += 1
    acc_ref = refs[pos]
    dec_state = refs[pos + 1:]
    k = pl.program_id(1)

    @pl.when(k == 0)
    def _():
        acc_ref[...] = jnp.zeros_like(acc_ref)

    if dec is not None:
        dec_p = lax.rem(pl.program_id(0) * pl.num_programs(1) + k, dec.n_steps)
        run_decode = functools.partial(_decode_step, p=dec_p, dec=dec, in_refs=dec_in, o_ref=dec_out,
                                       state=dec_state)
        run_decode("init")
        dec_carry = run_decode("scores")
    acc_ref[...] += _dot(h_ref[...], w_ref[...])
    if dec is not None:
        run_decode("values", carry=dec_carry)

    @pl.when(k == pl.num_programs(1) - 1)
    def _():
        x2 = x1_ref[...] + _rms(acc_ref[...], g_ref[...])
        x2_ref[...] = x2
        if emit_next:
            xn_ref[...] = _rms(x2, gn_ref[...]).astype(xn_ref.dtype)

    if dec is not None:
        run_decode("finish")


def _down_block(hid, w_down, x1, g_post, g_next, tm, tk, dec=None):
    m, d = x1.shape
    kdim = hid.shape[1]
    n_i, n_k = m // tm, kdim // tk
    emit_next = g_next is not None
    row = pl.BlockSpec((tm, d), lambda i, k, *_: (i, 0))
    const = pl.BlockSpec((1, d), lambda i, k, *_: (0, 0))
    in_specs = [pl.BlockSpec((tm, tk), lambda i, k, *_: (i, k)),
                pl.BlockSpec((tk, d), lambda i, k, *_: (k, 0)), row, const]
    args = [hid, w_down, x1, g_post]
    out_specs = [row]
    out_shape = [jax.ShapeDtypeStruct((m, d), F32)]
    scratch = [pltpu.VMEM((tm, d), F32)]
    if emit_next:
        in_specs.append(const)
        args.append(g_next)
        out_specs.append(row)
        out_shape.append(jax.ShapeDtypeStruct((m, d), BF16))
    n_prefetch = 0
    if dec is not None:
        assert dec.n_items == n_i * n_k
        dec_specs, dec_args, dec_out_spec, dec_out_shape, dec_scratch = dec.operands(lambda i, k: i * n_k + k)
        in_specs += dec_specs
        args = [dec.page_table] + args + dec_args
        out_specs.append(dec_out_spec)
        out_shape.append(dec_out_shape)
        scratch += dec_scratch
        n_prefetch = 1
    grid_spec = pltpu.PrefetchScalarGridSpec(
        num_scalar_prefetch=n_prefetch, grid=(n_i, n_k), in_specs=in_specs, out_specs=out_specs,
        scratch_shapes=scratch)
    out = pl.pallas_call(
        functools.partial(_down_kernel, emit_next=emit_next, dec=None if dec is None else dec.static),
        grid_spec=grid_spec,
        out_shape=out_shape,
        compiler_params=_params(("arbitrary", "arbitrary"), 58),
        name="down_norm_residual",
    )(*args)
    return out[0], (out[1] if emit_next else None), (out[-1] if dec is not None else None)


def _prefix_kernel(zf_ref, bf_ref, logf_ref, cq_ref, ct_ref, c_sc, *, t, n_heads):
    lf = _log_sigmoid(zf_ref[0] + bf_ref[...])
    logf_ref[0] = lf[:, 0:n_heads]
    r_i = lax.broadcasted_iota(jnp.int32, (CHUNK, CHUNK), 0)
    c_i = lax.broadcasted_iota(jnp.int32, (CHUNK, CHUNK), 1)
    tri = _ones_where(r_i >= c_i)
    carry = jnp.zeros((1, 128), F32)
    for blk in range(t // CHUNK):
        cb = _dot3_left(tri, lf[blk * CHUNK:(blk + 1) * CHUNK, :]) + carry
        c_sc[blk * CHUNK:(blk + 1) * CHUNK, :] = cb * LOG2E
        carry = cb[CHUNK - 1:CHUNK, :]
    c = c_sc[...]
    ct_ref[0] = c.T[0:n_heads, :]
    for h in range(n_heads):
        cq_ref[0, h] = c[:, h:h + 1]


def _fox_prefix(z3, bf_pad, n_heads, f_blk):
    b, t, _ = z3.shape
    return pl.pallas_call(
        functools.partial(_prefix_kernel, t=t, n_heads=n_heads),
        grid=(b,),
        in_specs=[pl.BlockSpec((1, t, 128), lambda i: (i, 0, f_blk)), pl.BlockSpec((1, 128), lambda i: (0, 0))],
        out_specs=[
            pl.BlockSpec((1, t, n_heads), lambda i: (i, 0, 0)),
            pl.BlockSpec((1, n_heads, t, 1), lambda i: (i, 0, 0, 0)),
            pl.BlockSpec((1, n_heads, t), lambda i: (i, 0, 0)),
        ],
        out_shape=[
            jax.ShapeDtypeStruct((b, t, n_heads), F32),
            jax.ShapeDtypeStruct((b, n_heads, t, 1), F32),
            jax.ShapeDtypeStruct((b, n_heads, t), F32),
        ],
        scratch_shapes=[pltpu.VMEM((t, 128), F32)],
        compiler_params=_params(("arbitrary",), 48),
        name="fox_prefix",
    )(z3, bf_pad)


def _fox_attn_kernel(qi_ref, ki_ref, q_ref, k_ref, v_ref, cq_ref, ck_ref, o_ref, m_sc, l_sc, acc_sc, cq_sc,
                     *, tq, n_heads):
    step = pl.program_id(1)
    qi = qi_ref[step]
    ki = ki_ref[step]
    n_sub = tq // 128

    @pl.when(ki == 0)
    def _():
        m_sc[...] = jnp.full_like(m_sc, -jnp.inf)
        l_sc[...] = jnp.zeros_like(l_sc)
        acc_sc[...] = jnp.zeros_like(acc_sc)
        for h in range(n_heads):
            cq_sc[h] = jnp.broadcast_to(cq_ref[0, h], (tq, 128))

    def block(diagonal):
        if diagonal:
            keep = (lax.broadcasted_iota(jnp.int32, (tq, tq), 0) >= lax.broadcasted_iota(jnp.int32, (tq, tq), 1))
        for h in range(n_heads):
            hs = slice(h * HEAD_DIM, (h + 1) * HEAD_DIM)
            s = _dot_nt(q_ref[0, :, hs], k_ref[0, :, hs]) - ck_ref[0, h:h + 1, :]
            if diagonal:
                s = jnp.where(keep, s, -jnp.inf)
            subs = [s[:, j * 128:(j + 1) * 128] for j in range(n_sub)]
            mc = subs[0]
            for x in subs[1:]:
                mc = jnp.maximum(mc, x)
            cq = cq_sc[h]
            m_prev = m_sc[h]
            m_new = jnp.maximum(m_prev, jnp.max(mc, axis=1, keepdims=True) + cq)
            alpha = jnp.exp2(m_prev - m_new)
            shift = m_new - cq
            ps = [jnp.exp2(x - shift) for x in subs]
            lsum = ps[0]
            for x in ps[1:]:
                lsum = lsum + x
            l_sc[h] = alpha * l_sc[h] + jnp.sum(lsum, axis=1, keepdims=True)
            p = jnp.concatenate(ps, axis=1).astype(BF16)
            acc_sc[:, hs] = alpha * acc_sc[:, hs] + _dot(p, v_ref[0, :, hs])
            m_sc[h] = m_new

    @pl.when(ki < qi)
    def _():
        block(False)

    @pl.when(ki == qi)
    def _():
        block(True)
        for h in range(n_heads):
            hs = slice(h * HEAD_DIM, (h + 1) * HEAD_DIM)
            o_ref[0, :, hs] = (acc_sc[:, hs] / l_sc[h]).astype(o_ref.dtype)


def _fox_attn_prompt(q, k, v, cq, ct, n_heads, tq):
    b, t, d_a = q.shape
    nq = t // tq
    pairs = [(i, j) for i in range(nq) for j in range(i + 1)]
    qi_tab = jnp.asarray([pr[0] for pr in pairs], jnp.int32)
    ki_tab = jnp.asarray([pr[1] for pr in pairs], jnp.int32)
    q_blk = pl.BlockSpec((1, tq, d_a), lambda bi, s, qt, kt: (bi, qt[s], 0))
    kv_blk = pl.BlockSpec((1, tq, d_a), lambda bi, s, qt, kt: (bi, kt[s], 0))
    grid_spec = pltpu.PrefetchScalarGridSpec(
        num_scalar_prefetch=2,
        grid=(b, len(pairs)),
        in_specs=[
            q_blk, kv_blk, kv_blk,
            pl.BlockSpec((1, n_heads, tq, 1), lambda bi, s, qt, kt: (bi, 0, qt[s], 0)),
            pl.BlockSpec((1, n_heads, tq), lambda bi, s, qt, kt: (bi, 0, kt[s])),
        ],
        out_specs=q_blk,
        scratch_shapes=[
            pltpu.VMEM((n_heads, tq, 128), F32),
            pltpu.VMEM((n_heads, tq, 128), F32),
            pltpu.VMEM((tq, d_a), F32),
            pltpu.VMEM((n_heads, tq, 128), F32),
        ],
    )
    return pl.pallas_call(
        functools.partial(_fox_attn_kernel, tq=tq, n_heads=n_heads),
        grid_spec=grid_spec,
        out_shape=jax.ShapeDtypeStruct((b, t, d_a), BF16),
        compiler_params=_params(("arbitrary", "arbitrary"), 48),
        name="fox_attn_prompt",
    )(qi_tab, ki_tab, q, k, v, cq, ct)


def _layernorm_silu(y, g, b):
    mu = jnp.mean(y, axis=-1, keepdims=True)
    yc = y - mu
    yn = yc * lax.rsqrt(jnp.mean(yc * yc, axis=-1, keepdims=True) + EPS) * g + b
    return yn * _sigmoid(yn)


def _mixer_kernel(*refs, tm, n_heads_b, conv_w, halo, dec):
    refs = list(refs)
    if dec is not None:
        refs = refs[1:]
    ub_ref, vb_ref, ac_ref, gc_ref, gv_ref, ws_ref, bst_ref, cw_ref, cb_ref, lg_ref, lb_ref = refs[:11]
    pos = 11
    if dec is not None:
        dec_in = refs[pos:pos + 2 + 3 * dec.n_grp]
        pos += 2 + 3 * dec.n_grp
    sgu_ref, conv_ref, tail_ref = refs[pos:pos + 3]
    pos += 3
    if dec is not None:
        dec_out = refs[pos]
        pos += 1
    g_sc = refs[pos]
    dec_state = refs[pos + 1:]
    ti = pl.program_id(1)

    @pl.when(ti == 0)
    def _():
        g_sc[0:halo, :] = jnp.zeros((halo, g_sc.shape[1]), F32)

    if dec is not None:
        dec_p = lax.rem(pl.program_id(0) * pl.num_programs(1) + ti, dec.n_steps)
        run_decode = functools.partial(_decode_step, p=dec_p, dec=dec, in_refs=dec_in, o_ref=dec_out,
                                       state=dec_state)
        run_decode("init")

    u = _gelu(ub_ref[0])
    vn = _rms(_gelu(vb_ref[0]), gv_ref[...])
    r_i = lax.broadcasted_iota(jnp.int32, (CHUNK, CHUNK), 0)
    c_i = lax.broadcasted_iota(jnp.int32, (CHUNK, CHUNK), 1)
    for h in range(n_heads_b):
        hs = slice(h * HEAD_DIM, (h + 1) * HEAD_DIM)
        w_h = jnp.where(r_i >= c_i, ws_ref[h], 0.0).astype(BF16)
        bias_h = bst_ref[:, h:h + 1]
        for c in range(tm // CHUNK):
            rs = slice(c * CHUNK, (c + 1) * CHUNK)
            mix = _dot(w_h, vn[rs, hs].astype(BF16)) + bias_h
            sgu_ref[0, rs, hs] = (u[rs, hs] * mix).astype(sgu_ref.dtype)

    g_sc[halo:halo + tm, :] = ac_ref[0] * _sigmoid(gc_ref[0])
    base = halo - (conv_w - 1)
    n_rows = halo + tm
    window = g_sc[...]
    rotated = {0: window}
    y = cb_ref[...]
    for k in range(conv_w):
        a, r = divmod(base + k, 8)
        if r not in rotated:
            rotated[r] = pltpu.roll(window, n_rows - r, 0)
        y = y + rotated[r][8 * a:8 * a + tm, :] * cw_ref[k:k + 1, :]
    conv_ref[0] = _layernorm_silu(y, lg_ref[...], lb_ref[...]).astype(conv_ref.dtype)
    if dec is not None:
        run_decode("values", carry=run_decode("scores"))

    @pl.when(ti == pl.num_programs(1) - 1)
    def _():
        tail_ref[0] = g_sc[halo + tm - (conv_w - 1):halo + tm, :]

    g_sc[0:halo, :] = g_sc[tm:tm + halo, :]
    if dec is not None:
        run_decode("finish")


def _mixer_prompt(z3, g_v, w_s, b_s_t, conv_w, conv_b, ln_g, ln_b, tm, dec=None):
    b, t, _ = z3.shape
    n_heads_b = w_s.shape[0]
    d_b = n_heads_b * HEAD_DIM
    d_c = conv_w.shape[1]
    kw = conv_w.shape[0]
    halo = 32
    n_t = t // tm
    zcol = lambda c: pl.BlockSpec((1, tm, d_b), lambda bi, ti, *_: (bi, ti, c))
    const = lambda shape: pl.BlockSpec(shape, lambda bi, ti, *_: (0,) * len(shape))
    in_specs = [zcol(0), zcol(1), zcol(2), zcol(3),
                const((1, d_b)), const((n_heads_b, CHUNK, CHUNK)), const((CHUNK, n_heads_b)),
                const((kw, d_c)), const((1, d_c)), const((1, d_c)), const((1, d_c))]
    args = [z3, z3, z3, z3, g_v, w_s, b_s_t, conv_w, conv_b, ln_g, ln_b]
    out_specs = [
        pl.BlockSpec((1, tm, d_b), lambda bi, ti, *_: (bi, ti, 0)),
        pl.BlockSpec((1, tm, d_c), lambda bi, ti, *_: (bi, ti, 0)),
        pl.BlockSpec((1, kw - 1, d_c), lambda bi, ti, *_: (bi, 0, 0)),
    ]
    out_shape = [
        jax.ShapeDtypeStruct((b, t, d_b), BF16),
        jax.ShapeDtypeStruct((b, t, d_c), BF16),
        jax.ShapeDtypeStruct((b, kw - 1, d_c), F32),
    ]
    scratch = [pltpu.VMEM((halo + tm, d_c), F32)]
    n_prefetch = 0
    if dec is not None:
        assert dec.n_items == b * n_t
        dec_specs, dec_args, dec_out_spec, dec_out_shape, dec_scratch = dec.operands(lambda bi, ti: bi * n_t + ti)
        in_specs += dec_specs
        args = [dec.page_table] + args + dec_args
        out_specs.append(dec_out_spec)
        out_shape.append(dec_out_shape)
        scratch += dec_scratch
        n_prefetch = 1
    grid_spec = pltpu.PrefetchScalarGridSpec(
        num_scalar_prefetch=n_prefetch, grid=(b, n_t), in_specs=in_specs, out_specs=out_specs,
        scratch_shapes=scratch)
    out = pl.pallas_call(
        functools.partial(_mixer_kernel, tm=tm, n_heads_b=n_heads_b, conv_w=kw, halo=halo,
                          dec=None if dec is None else dec.static),
        grid_spec=grid_spec,
        out_shape=out_shape,
        compiler_params=_params(("arbitrary", "arbitrary"), 48),
        name="mixer_prompt",
    )(*args)
    return out[0], out[1], out[2], (out[3] if dec is not None else None)


def _mixer_sample_kernel(zs_ref, zf_ref, bf_ref, gv_ref, w0_ref, b0_ref, st_ref, cw_ref, cb_ref, lg_ref, lb_ref,
                         sgu_ref, conv_ref, state_ref, vn_ref, logf_ref, y_sc, *, d_b, conv_w):
    n = zs_ref.shape[0]
    u = _gelu(zs_ref[:, 0:d_b])
    vn = _rms(_gelu(zs_ref[:, d_b:2 * d_b]), gv_ref[...])
    vn_ref[...] = vn
    sgu_ref[...] = (u * (w0_ref[...] * vn + b0_ref[...])).astype(sgu_ref.dtype)
    glu = zs_ref[:, 2 * d_b:3 * d_b] * _sigmoid(zs_ref[:, 3 * d_b:4 * d_b])
    kw = conv_w - 1
    for bi in range(n):
        g_new = glu[bi:bi + 1, :]
        y_sc[bi:bi + 1, :] = (jnp.sum(st_ref[bi] * cw_ref[0:kw, :], axis=0, keepdims=True)
                              + g_new * cw_ref[kw:kw + 1, :])
        state_ref[bi, 0:kw - 1, :] = st_ref[bi, 1:kw, :]
        state_ref[bi, kw - 1:kw, :] = g_new
    conv_ref[...] = _layernorm_silu(y_sc[...] + cb_ref[...], lg_ref[...], lb_ref[...]).astype(conv_ref.dtype)
    logf_ref[...] = _log_sigmoid(zf_ref[...] + bf_ref[...])


def _mixer_sample(zs, zf, bf_pad, g_v, w0_row, b0_row, state, conv_w, conv_b, ln_g, ln_b):
    n = zs.shape[0]
    d_b = g_v.shape[1]
    kw, d_c = conv_w.shape
    full = lambda shape: pl.BlockSpec(shape, lambda i: (0,) * len(shape))
    args = (zs, zf, bf_pad, g_v, w0_row, b0_row, state, conv_w, conv_b, ln_g, ln_b)
    out_shape = [
        jax.ShapeDtypeStruct((n, d_b), BF16),
        jax.ShapeDtypeStruct((n, d_c), BF16),
        jax.ShapeDtypeStruct((n, kw - 1, d_c), F32),
        jax.ShapeDtypeStruct((n, d_b), F32),
        jax.ShapeDtypeStruct((n, 128), F32),
    ]
    return pl.pallas_call(
        functools.partial(_mixer_sample_kernel, d_b=d_b, conv_w=kw),
        grid=(1,),
        in_specs=[full(a.shape) for a in args],
        out_specs=[full(s.shape) for s in out_shape],
        out_shape=out_shape,
        scratch_shapes=[pltpu.VMEM((n, d_c), F32)],
        compiler_params=_params(("arbitrary",), 32),
        name="mixer_sample",
    )(*args)


def _decode_step(phase, p, dec, in_refs, o_ref, state, carry=None):
    n_grp, n_heads, scale = dec.n_grp, dec.n_heads, dec.scale
    zs_ref, cn_ref = in_refs[0], in_refs[1]
    refs = in_refs[2:]
    k_refs, v_refs, lf_refs = refs[:n_grp], refs[n_grp:2 * n_grp], refs[2 * n_grp:3 * n_grp]
    cn_sc, m_sc, l_sc, acc_sc, tail_sc = state
    rows_pp = k_refs[0].shape[2]
    n_flat = rows_pp // 128
    lane = lax.broadcasted_iota(jnp.int32, (n_heads, 128), 1)
    sub = lax.broadcasted_iota(jnp.int32, (n_heads, 128), 0)
    cls_mask = n_heads - 1
    own = sub == (lane & cls_mask)
    diag = sub == lane
    r_i = lax.broadcasted_iota(jnp.int32, (128, 128), 0)
    c_i = lax.broadcasted_iota(jnp.int32, (128, 128), 1)
    same = (r_i & cls_mask) == (c_i & cls_mask)

    if phase == "init":
        @pl.when(p == 0)
        def _():
            spread = _ones_where((r_i < n_heads) & ((c_i & cls_mask) == r_i))
            cn_sc[...] = _dot3(jnp.broadcast_to(cn_ref[0], (n_heads, 128)), spread)
            m_sc[...] = jnp.full_like(m_sc, -jnp.inf)
            l_sc[...] = jnp.zeros_like(l_sc)
            acc_sc[...] = jnp.zeros_like(acc_sc)
            tail_sc[...] = jnp.zeros_like(tail_sc)
        return

    if phase == "finish":
        @pl.when(p == dec.n_steps - 1)
        def _():
            m_col = jnp.max(jnp.where(diag, m_sc[...], -jnp.inf), axis=1, keepdims=True)
            l_row = jnp.sum(l_sc[...], axis=0, keepdims=True)
            l_col = jnp.sum(jnp.where(own, jnp.broadcast_to(l_row, (n_heads, 128)), 0.0), axis=1, keepdims=True)
            cn_col = jnp.sum(jnp.where(diag, cn_sc[...], 0.0), axis=1, keepdims=True)
            q_new = zs_ref[0, 0:n_heads, :]
            k_new = zs_ref[0, n_heads:2 * n_heads, :]
            v_new = zs_ref[0, 2 * n_heads:3 * n_heads, :]
            s_new = jnp.sum(q_new * k_new, axis=1, keepdims=True) * scale + cn_col - cn_col
            m_f = jnp.maximum(m_col, s_new)
            a_f = jnp.exp(m_col - m_f)
            p_new = jnp.exp(s_new - m_f)
            o_ref[0] = (a_f * acc_sc[...] + p_new * v_new) / (a_f * l_col + p_new)
        return

    if phase == "values":
        return _decode_values(dec, carry, v_refs, state, own, diag, n_flat)

    lf = jnp.concatenate([lf_refs[g][0, 0] for g in range(n_grp)], axis=0)
    n_rows = n_grp * n_flat
    row_tot = _dot3(lf, _ones_where(same))
    within = _dot3(lf, _ones_where(same & (r_i > c_i)))
    rr = lax.broadcasted_iota(jnp.int32, (n_rows, n_rows), 0)
    cc = lax.broadcasted_iota(jnp.int32, (n_rows, n_rows), 1)
    suffix = within + _dot3_left(_ones_where(cc > rr), row_tot) + tail_sc[0:1, :]

    q8 = zs_ref[0, 0:n_heads, :].astype(BF16)
    scores = []
    for g in range(n_grp):
        s_t = _dot_nt(q8, k_refs[g][0, 0].astype(BF16))
        flat = [jnp.sum(jnp.where(own, s_t[:, a * 128:(a + 1) * 128], 0.0), axis=0, keepdims=True)
                for a in range(n_flat)]
        scores.append(jnp.concatenate(flat, axis=0) * scale
                      + suffix[g * n_flat:(g + 1) * n_flat, :] + cn_sc[...])
    return scores, row_tot


def _decode_values(dec, carry, v_refs, state, own, diag, n_flat):
    n_grp, n_heads = dec.n_grp, dec.n_heads
    cn_sc, m_sc, l_sc, acc_sc, tail_sc = state
    scores, row_tot = carry
    m_step = scores[0]
    for x in scores[1:]:
        m_step = jnp.maximum(m_step, x)
    shift = 1
    while shift < n_flat:
        m_step = jnp.maximum(m_step, pltpu.roll(m_step, shift, 0))
        shift *= 2
    shift = n_heads
    while shift < 128:
        m_step = jnp.maximum(m_step, pltpu.roll(m_step, shift, 1))
        shift *= 2
    m_prev = m_sc[...]
    m_new = jnp.maximum(m_prev, m_step)
    alpha = jnp.exp(m_prev - m_new)
    probs = [jnp.exp(x - m_new) for x in scores]
    l_step = probs[0]
    for x in probs[1:]:
        l_step = l_step + x
    l_sc[...] = alpha * l_sc[...] + l_step
    pv = jnp.zeros((n_heads, HEAD_DIM), F32)
    for g in range(n_grp):
        blocks = [jnp.where(own, jnp.broadcast_to(probs[g][a:a + 1, :], (n_heads, 128)), 0.0)
                  for a in range(n_flat)]
        pv += _dot(jnp.concatenate(blocks, axis=1).astype(BF16), v_refs[g][0, 0].astype(BF16))
    alpha_col = jnp.sum(jnp.where(diag, alpha, 0.0), axis=1, keepdims=True)
    acc_sc[...] = alpha_col * acc_sc[...] + pv
    m_sc[...] = m_new
    tail_sc[...] += jnp.sum(row_tot, axis=0, keepdims=True)


class _DecodeStatic(NamedTuple):
    n_grp: int
    n_steps: int
    n_items: int
    n_heads: int
    scale: float


class _DecodePlan:
    def __init__(self, page_table, zs3, cn, cache_k, cache_v, cache_lf, layer, n_heads, b0, nb, n_grp):
        n_pages = page_table.shape[1]
        assert n_pages % n_grp == 0 and cache_k.shape[2] == 128 * n_heads and cache_lf.shape[2] == n_heads
        assert n_heads & (n_heads - 1) == 0 and n_heads <= 8
        self.page_table, self.layer, self.b0, self.nb, self.n_pages = page_table, layer, b0, nb, n_pages
        self.arrays = (zs3, cn, cache_k, cache_v, cache_lf)
        n_steps = n_pages // n_grp
        self.static = _DecodeStatic(n_grp, n_steps, nb * n_steps, n_heads, HEAD_DIM ** -0.5)
        self.n_items = nb * n_steps

    def operands(self, step_of):
        zs3, cn, cache_k, cache_v, cache_lf = self.arrays
        st, b0, layer, n_pages = self.static, self.b0, self.layer, self.n_pages
        n_heads, n_grp, n_steps = st.n_heads, st.n_grp, st.n_steps
        rows_pp = cache_k.shape[2]
        seq = lambda g0, g1: step_of(g0, g1) // n_steps

        def page_spec(block, g):
            def index(g0, g1, pt):
                s = step_of(g0, g1)
                return (pt[b0 + s // n_steps, n_pages - (s % n_steps + 1) * n_grp + g], layer, 0, 0)
            return pl.BlockSpec(block, index)

        kv_specs = [page_spec((1, 1, rows_pp, HEAD_DIM), g) for g in range(n_grp)]
        lf_specs = [page_spec((1, 1, n_heads, 128), g) for g in range(n_grp)]
        in_specs = [pl.BlockSpec((1, 3 * n_heads, HEAD_DIM), lambda g0, g1, pt: (b0 + seq(g0, g1), 0, 0)),
                    pl.BlockSpec((1, 1, 128), lambda g0, g1, pt: (b0 + seq(g0, g1), 0, 0))]
        in_specs += kv_specs + kv_specs + lf_specs
        args = [zs3, cn] + [cache_k] * n_grp + [cache_v] * n_grp + [cache_lf] * n_grp
        out_spec = pl.BlockSpec((1, n_heads, HEAD_DIM), lambda g0, g1, pt: (seq(g0, g1), 0, 0))
        out_shape = jax.ShapeDtypeStruct((self.nb, n_heads, HEAD_DIM), F32)
        scratch = [pltpu.VMEM((n_heads, 128), F32)] * 3 + [pltpu.VMEM((n_heads, HEAD_DIM), F32),
                                                           pltpu.VMEM((n_heads, 128), F32)]
        return in_specs, args, out_spec, out_shape, scratch


def kernel(x_prompt, x_sample, cache_k, cache_v, cache_logf, state_conv, page_table, w_in, b_f, g_v, w_s, b_s,
           conv_w, conv_b, ln_c_g, ln_c_b, w_o, g_pre_mix, g_post_mix, g_pre_mlp, g_post_mlp, w_up, w_down):
    bp, tp, d = x_prompt.shape
    bs, ds, _ = x_sample.shape
    assert ds == 1, "the sample path handles exactly one new token per sequence"
    depth = w_in.shape[0]
    n_pool, _, page, n_heads, _ = cache_k.shape
    d_a = n_heads * HEAD_DIM
    n_heads_b = w_s.shape[1]
    d_b = n_heads_b * HEAD_DIM
    d_c = conv_w.shape[2]
    off_f = 3 * d_a
    assert w_in.shape[2] == off_f + n_heads + 2 * d_b + 2 * d_c and d_b == d_c and page == CHUNK
    mp = bp * tp

    n_rest = 2 * d_b + 2 * d_c
    w_in_t = jnp.swapaxes(w_in, 1, 2)
    w_o_b = w_o.astype(BF16)
    d_ff = w_up.shape[2]
    bf_pad = jnp.pad(b_f, ((0, 0), (0, 128 - n_heads)))
    row2 = lambda a, l: a[l][None, :]
    proj = functools.partial(_ws_matmul, transposed=True)

    n_pages = page_table.shape[1]
    assert bs % 2 == 0
    grp_up = (bs // 2) * n_pages // ((d_ff // UP_TILE[1]) * (mp // UP_TILE[0]))
    grp_down = (bs // 2) * n_pages // ((mp // DOWN_TILE[0]) * (d_ff // DOWN_TILE[1]))

    cache_k2 = cache_k.reshape(n_pool, depth, page * n_heads, HEAD_DIM)
    cache_v2 = cache_v.reshape(n_pool, depth, page * n_heads, HEAD_DIM)
    cache_lf = cache_logf.reshape(n_pool, depth, page * n_heads // 128, 128)

    xp = x_prompt.reshape(mp, d)
    xs = x_sample.reshape(bs, d)
    xpn = _norm_bf16(xp, row2(g_pre_mix, 0), 512)
    xsn = _norm_bf16(xs, row2(g_pre_mix, 0), bs)
    fp_l, cp_l, ks_l, vs_l, fs_l, cs_l, us_l = ([] for _ in range(7))
    k_pages = v_pages = None
    for l in range(depth):
        g_next = row2(g_pre_mix, l + 1) if l + 1 < depth else None
        q, zf = proj(xpn, w_in_t, l, 0, d_a, 1024, d_a, "q", "proj_q", f_row=off_f)
        kb, k_pages = proj(xpn, w_in_t, l, d_a, d_a, 1024, d_a, "kv", "proj_k",
                           pages=(k_pages, bp, tp, depth, n_heads, page))
        vb, v_pages = proj(xpn, w_in_t, l, 2 * d_a, d_a, 1024, d_a, "kv", "proj_v",
                           pages=(v_pages, bp, tp, depth, n_heads, page))
        z3 = proj(xpn, w_in_t, l, off_f + n_heads, n_rest, 1024, 1024, "f32", "proj_rest").reshape(bp, tp, n_rest)
        logf, cq, ct = _fox_prefix(zf.reshape(bp, tp, 128), bf_pad[l][None, :], n_heads, 0)
        attn = _fox_attn_prompt(q.reshape(bp, tp, d_a), kb.reshape(bp, tp, d_a), vb.reshape(bp, tp, d_a),
                                cq, ct, n_heads, 512)
        fp_l.append(logf)
        zs_q, zs_f = proj(xsn, w_in_t, l, 0, d_a, bs, d_a, "f32", "proj_q_sample", f_row=off_f)
        zs_kv = proj(xsn, w_in_t, l, d_a, 2 * d_a, bs, d_a, "f32", "proj_kv_sample")
        zs_rest = proj(xsn, w_in_t, l, off_f + n_heads, n_rest, bs, 1024, "f32", "proj_rest_sample")
        w0_row = jnp.repeat(w_s[l][:, 0, 0], HEAD_DIM)[None, :]
        b0_row = jnp.repeat(b_s[l][:, 0], HEAD_DIM)[None, :]
        sgu_s, conv_s, state_s, vn_s, logf_s = _mixer_sample(
            zs_rest, zs_f, bf_pad[l][None, :], row2(g_v, l), w0_row, b0_row, state_conv[l], conv_w[l],
            row2(conv_b, l), row2(ln_c_g, l), row2(ln_c_b, l))
        zs3 = jnp.concatenate([zs_q, zs_kv], axis=1).reshape(bs, 3 * n_heads, HEAD_DIM)
        plan = lambda b0, n_grp: _DecodePlan(page_table, zs3, logf_s[:, None, :], cache_k2, cache_v2, cache_lf,
                                             l, n_heads, b0, bs // 2, n_grp)
        sgu, conv, conv_tail, _ = _mixer_prompt(z3, row2(g_v, l), w_s[l], b_s[l].T, conv_w[l], row2(conv_b, l),
                                                row2(ln_c_g, l), row2(ln_c_b, l), MIXER_TM)
        cp_l.append(conv_tail)
        x1, x1n = _wo_block(attn.reshape(mp, d_a), sgu.reshape(mp, d_b), conv.reshape(mp, d_c), w_o_b, l, xp,
                            row2(g_post_mix, l), row2(g_pre_mlp, l), 512)
        hid, w_down_l, attn_s0 = _ws_matmul(x1n, w_up, l, 0, d_ff, UP_TILE[0], UP_TILE[1], "relu2", "mlp_up",
                                            side_cast=w_down, dec=plan(0, grp_up))
        xp, xpn, attn_s1 = _down_block(hid, w_down_l, x1, row2(g_post_mlp, l), g_next, DOWN_TILE[0],
                                       DOWN_TILE[1], dec=plan(bs // 2, grp_down))
        attn_s = jnp.concatenate([attn_s0, attn_s1], axis=0).reshape(bs, d_a).astype(BF16)
        x1s, x1sn = _wo_block(attn_s, sgu_s, conv_s, w_o_b, l, xs, row2(g_post_mix, l), row2(g_pre_mlp, l), bs)
        hid_s = _ws_matmul(x1sn, w_up, l, 0, d_ff, bs, 1024, "relu2", "mlp_up_sample")
        xs, xsn, _ = _down_block(hid_s, w_down_l, x1s, row2(g_post_mlp, l), g_next, bs, 2048)
        ks_l.append(zs_kv[:, :d_a])
        vs_l.append(zs_kv[:, d_a:])
        fs_l.append(logf_s[:, :n_heads])
        cs_l.append(state_s)
        us_l.append(vn_s)

    npp = tp // page
    kv_out = lambda pages: pages.reshape(bp, npp, depth, page, n_heads, HEAD_DIM)
    logf_prompt = jnp.stack(fp_l, axis=1).reshape(bp, depth, npp, page, n_heads).swapaxes(1, 2)
    return (xp.reshape(bp, tp, d), xs.reshape(bs, ds, d),
            kv_out(k_pages), kv_out(v_pages), logf_prompt,
            jnp.stack(cp_l, axis=0),
            jnp.stack(ks_l, axis=1).reshape(bs, depth, ds, n_heads, HEAD_DIM),
            jnp.stack(vs_l, axis=1).reshape(bs, depth, ds, n_heads, HEAD_DIM),
            jnp.stack(fs_l, axis=1).reshape(bs, depth, ds, n_heads),
            jnp.stack(cs_l, axis=0),
            jnp.stack(us_l, axis=0).reshape(depth, bs, ds, d_b))
```

```python
import functools
from typing import NamedTuple

import numpy as np
import jax
import jax.numpy as jnp
from jax import lax
from jax.experimental import pallas as pl
from jax.experimental.pallas import tpu as pltpu

F32 = jnp.float32
BF16 = jnp.bfloat16

HEAD_DIM = 128
CHUNK = 128
EPS = 1e-6
MIB = 1024 * 1024
GELU_C = float(np.sqrt(2.0 / np.pi))
LOG2E = float(np.log2(np.e))

UP_TILE = (1024, 1024)
DOWN_TILE = (512, 2048)
MIXER_TM = 256


def _params(semantics, vmem_mib):
    return pltpu.CompilerParams(dimension_semantics=semantics, vmem_limit_bytes=vmem_mib * MIB)


def _rms(x, g):
    return x * lax.rsqrt(jnp.mean(x * x, axis=-1, keepdims=True) + EPS) * g


def _gelu(x):
    return x * (0.5 * (1.0 + jnp.tanh(GELU_C * (x + 0.044715 * (x * x * x)))))


def _sigmoid(x):
    return 1.0 / (1.0 + jnp.exp(-x))


def _log_sigmoid(x):
    return jnp.minimum(x, 0.0) - jnp.log1p(jnp.exp(-jnp.abs(x)))


def _split3(x):
    hi = x.astype(BF16)
    r = x - hi.astype(F32)
    mid = r.astype(BF16)
    lo = (r - mid.astype(F32)).astype(BF16)
    return hi, mid, lo


def _dot(a, b):
    return jnp.dot(a, b, preferred_element_type=F32)


def _dot_nt(a, b):
    return lax.dot_general(a, b, (((1,), (1,)), ((), ())), preferred_element_type=F32)


def _dot3(x, w):
    hi, mid, lo = _split3(x)
    return (_dot(hi, w) + _dot(mid, w)) + _dot(lo, w)


def _dot3_left(w, x):
    hi, mid, lo = _split3(x)
    return (_dot(w, hi) + _dot(w, mid)) + _dot(w, lo)


def _ones_where(cond):
    return jnp.where(cond, 1.0, 0.0).astype(BF16)


def _norm_kernel(x_ref, g_ref, o_ref):
    o_ref[...] = _rms(x_ref[...], g_ref[...]).astype(o_ref.dtype)


def _norm_bf16(x, g, tm):
    m, d = x.shape
    return pl.pallas_call(
        _norm_kernel,
        grid=(m // tm,),
        in_specs=[pl.BlockSpec((tm, d), lambda i: (i, 0)), pl.BlockSpec((1, d), lambda i: (0, 0))],
        out_specs=pl.BlockSpec((tm, d), lambda i: (i, 0)),
        out_shape=jax.ShapeDtypeStruct((m, d), BF16),
        compiler_params=_params(("arbitrary",), 32),
        name="pre_norm",
    )(x, g)


def _ws_kernel(*refs, mode, transposed, with_f, has_prev, kv_slot, side_cast, dec, n_heads, page, q_scale):
    refs = list(refs)
    if dec is not None:
        refs = refs[1:]
    x_ref, w_ref = refs[0], refs[1]
    pos = 2
    if with_f:
        wf_ref = refs[pos]
        pos += 1
    if has_prev:
        pos += 1
    if side_cast:
        side_in = refs[pos]
        pos += 1
    if dec is not None:
        dec_in = refs[pos:pos + 2 + 3 * dec.n_grp]
        pos += 2 + 3 * dec.n_grp
    n_out = 2 if (mode == "kv" or with_f) else 1
    outs = refs[pos:pos + n_out]
    pos += n_out
    if side_cast:
        refs[pos][...] = side_in[...].astype(BF16)
        pos += 1
    if dec is not None:
        dec_out = refs[pos]
        pos += 1
    w_sc = refs[pos]
    pos += 1
    if with_f:
        wf_sc = refs[pos]
        pos += 1
    dec_state = refs[pos:]

    @pl.when(pl.program_id(1) == 0)
    def _():
        w_sc[...] = (w_ref[0] if transposed else w_ref[...]).astype(BF16)
        if with_f:
            pad = jnp.zeros((wf_sc.shape[0] - wf_ref.shape[1], wf_sc.shape[1]), F32)
            wf_sc[...] = jnp.concatenate([wf_ref[0], pad], axis=0).astype(BF16)

    if dec is not None:
        dec_p = lax.rem(pl.program_id(0) * pl.num_programs(1) + pl.program_id(1), dec.n_steps)
        run_decode = functools.partial(_decode_step, p=dec_p, dec=dec, in_refs=dec_in, o_ref=dec_out,
                                       state=dec_state)
        run_decode("init")
        dec_carry = run_decode("scores")
    x = x_ref[...]
    acc = _dot_nt(x, w_sc[...]) if transposed else _dot(x, w_sc[...])
    if with_f:
        outs[1][...] = _dot_nt(x, wf_sc[...])
    if mode == "f32":
        outs[0][...] = acc
    elif mode == "q":
        outs[0][...] = (acc * q_scale).astype(BF16)
    elif mode == "relu2":
        h = jnp.maximum(acc, 0.0)
        outs[0][...] = (h * h).astype(BF16)
    else:
        outs[0][...] = acc.astype(BF16)
        dst_ref = outs[1]
        slot = () if has_prev else (kv_slot,)
        for pg in range(acc.shape[0] // page):
            for h in range(n_heads):
                dst_ref[(0, pg) + slot + (pl.ds(h, page, stride=n_heads), slice(None))] = (
                    acc[pg * page:(pg + 1) * page, h * HEAD_DIM:(h + 1) * HEAD_DIM])
        if not has_prev:
            for other in range(dst_ref.shape[2]):
                if other != kv_slot:
                    dst_ref[0, :, other] = jnp.zeros(dst_ref.shape[1:2] + dst_ref.shape[3:], F32)
    if dec is not None:
        run_decode("values", carry=dec_carry)
        run_decode("finish")


def _ws_matmul(x, w, layer, col0, n_cols, tm, tn, mode, name, *, transposed=False, f_row=None, pages=None,
               side_cast=None, dec=None):
    m, k = x.shape
    assert n_cols % tn == 0 and (f_row is None or (n_cols == tn and transposed))
    n_j, n_i = n_cols // tn, m // tm
    if transposed:
        w_spec = pl.BlockSpec((pl.Element(1), pl.Element(tn), pl.Element(k)),
                              lambda j, i, *_: (layer, pl.multiple_of(col0 + j * tn, 8), 0))
    else:
        assert col0 % tn == 0
        w_spec = pl.BlockSpec((None, k, tn), lambda j, i, *_: (layer, 0, col0 // tn + j))
    in_specs = [pl.BlockSpec((tm, k), lambda j, i, *_: (i, 0)), w_spec]
    args = [x, w]
    out_specs = [pl.BlockSpec((tm, tn), lambda j, i, *_: (i, j))]
    out_shape = [jax.ShapeDtypeStruct((m, n_cols), F32 if mode == "f32" else BF16)]
    scratch = [pltpu.VMEM((tn, k) if transposed else (k, tn), BF16)]
    if f_row is not None:
        in_specs.append(pl.BlockSpec((pl.Element(1), pl.Element(8), pl.Element(k)),
                                     lambda j, i, *_: (layer, f_row, 0)))
        args.append(w)
        out_specs.append(pl.BlockSpec((tm, 128), lambda j, i, *_: (i, 0)))
        out_shape.append(jax.ShapeDtypeStruct((m, 128), F32))
        scratch.append(pltpu.VMEM((128, k), BF16))
    aliases = {}
    n_heads = page = 0
    has_prev = False
    if mode == "kv":
        prev, bsz, t, depth, n_heads, page = pages
        assert n_cols == tn == n_heads * HEAD_DIM and t % tm == 0 and tm % page == 0
        tps = t // tm
        if prev is None:
            out_specs.append(pl.BlockSpec((1, tm // page, depth, page * n_heads, HEAD_DIM),
                                          lambda j, i, *_: (i // tps, i % tps, 0, 0, 0)))
        else:
            out_specs.append(pl.BlockSpec((1, tm // page, None, page * n_heads, HEAD_DIM),
                                          lambda j, i, *_: (i // tps, i % tps, layer, 0, 0)))
        out_shape.append(jax.ShapeDtypeStruct((bsz, t // page, depth, page * n_heads, HEAD_DIM), F32))
        if prev is not None:
            has_prev = True
            in_specs.append(pl.BlockSpec(memory_space=pl.ANY))
            args.append(prev)
            aliases = {2: 1}
    if side_cast is not None:
        _, side_r, side_c = side_cast.shape
        rows = side_r // (n_j * n_i)
        assert rows * n_j * n_i == side_r and rows % 16 == 0
        in_specs.append(pl.BlockSpec((None, rows, side_c), lambda j, i, *_: (layer, j * n_i + i, 0)))
        args.append(side_cast)
        out_specs.append(pl.BlockSpec((rows, side_c), lambda j, i, *_: (j * n_i + i, 0)))
        out_shape.append(jax.ShapeDtypeStruct((side_r, side_c), BF16))
    n_prefetch = 0
    if dec is not None:
        assert dec.n_items == n_j * n_i
        dec_specs, dec_args, dec_out_spec, dec_out_shape, dec_scratch = dec.operands(lambda j, i: j * n_i + i)
        in_specs += dec_specs
        args = [dec.page_table] + args + dec_args
        out_specs.append(dec_out_spec)
        out_shape.append(dec_out_shape)
        scratch += dec_scratch
        n_prefetch = 1
    grid_spec = pltpu.PrefetchScalarGridSpec(
        num_scalar_prefetch=n_prefetch, grid=(n_j, n_i), in_specs=in_specs, out_specs=out_specs,
        scratch_shapes=scratch)
    out = pl.pallas_call(
        functools.partial(_ws_kernel, mode=mode, transposed=transposed, with_f=f_row is not None,
                          has_prev=has_prev, kv_slot=layer, side_cast=side_cast is not None,
                          dec=None if dec is None else dec.static, n_heads=n_heads, page=page,
                          q_scale=HEAD_DIM ** -0.5 * LOG2E),
        grid_spec=grid_spec,
        out_shape=out_shape,
        input_output_aliases=aliases,
        compiler_params=_params(("arbitrary", "arbitrary"), 58),
        name=name,
    )(*args)
    return out if len(out) > 1 else out[0]


def _wo_kernel(a_ref, s_ref, c_ref, w_ref, x_ref, g1_ref, g2_ref, x1_ref, xn_ref, *, d_a, d_b):
    mixed = _dot(a_ref[...], w_ref[0:d_a, :])
    mixed += _dot(s_ref[...], w_ref[d_a:d_a + d_b, :])
    mixed += _dot(c_ref[...], w_ref[d_a + d_b:, :])
    x1 = x_ref[...] + _rms(mixed, g1_ref[...])
    x1_ref[...] = x1
    xn_ref[...] = _rms(x1, g2_ref[...]).astype(xn_ref.dtype)


def _wo_block(attn, sgu, conv, w_o, x, g_post, g_pre_mlp, tm):
    m, d = x.shape
    d_a, d_b, d_c = attn.shape[1], sgu.shape[1], conv.shape[1]
    row = lambda width: pl.BlockSpec((tm, width), lambda i: (i, 0))
    const = lambda shape: pl.BlockSpec(shape, lambda i: (0, 0))
    w_spec = const((d, d))
    return pl.pallas_call(
        functools.partial(_wo_kernel, d_a=d_a, d_b=d_b),
        grid=(m // tm,),
        in_specs=[row(d_a), row(d_b), row(d_c), w_spec, row(d), const((1, d)), const((1, d))],
        out_specs=[row(d), row(d)],
        out_shape=[jax.ShapeDtypeStruct((m, d), F32), jax.ShapeDtypeStruct((m, d), BF16)],
        compiler_params=_params(("arbitrary",), 48),
        name="wo_norm_residual",
    )(attn, sgu, conv, w_o, x, g_post, g_pre_mlp)


def _down_kernel(*refs, emit_next, dec):
    refs = list(refs)
    if dec is not None:
        refs = refs[1:]
    h_ref, w_ref, x1_ref, g_ref = refs[:4]
    pos = 4
    if emit_next:
        gn_ref = refs[pos]
        pos += 1
    if dec is not None:
        dec_in = refs[pos:pos + 2 + 3 * dec.n_grp]
        pos += 2 + 3 * dec.n_grp
    x2_ref = refs[pos]
    pos += 1
    if emit_next:
        xn_ref = refs[pos]
        pos += 1
    if dec is not None:
        dec_out = refs[pos]
        pos += 1
    acc_ref = refs[pos]
    dec_state = refs[pos + 1:]
    k = pl.program_id(1)

    @pl.when(k == 0)
    def _():
        acc_ref[...] = jnp.zeros_like(acc_ref)

    if dec is not None:
        dec_p = lax.rem(pl.program_id(0) * pl.num_programs(1) + k, dec.n_steps)
        run_decode = functools.partial(_decode_step, p=dec_p, dec=dec, in_refs=dec_in, o_ref=dec_out,
                                       state=dec_state)
        run_decode("init")
        dec_carry = run_decode("scores")
    acc_ref[...] += _dot(h_ref[...], w_ref[...])
    if dec is not None:
        run_decode("values", carry=dec_carry)

    @pl.when(k == pl.num_programs(1) - 1)
    def _():
        x2 = x1_ref[...] + _rms(acc_ref[...], g_ref[...])
        x2_ref[...] = x2
        if emit_next:
            xn_ref[...] = _rms(x2, gn_ref[...]).astype(xn_ref.dtype)

    if dec is not None:
        run_decode("finish")


def _down_block(hid, w_down, x1, g_post, g_next, tm, tk, dec=None):
    m, d = x1.shape
    kdim = hid.shape[1]
    n_i, n_k = m // tm, kdim // tk
    emit_next = g_next is not None
    row = pl.BlockSpec((tm, d), lambda i, k, *_: (i, 0))
    const = pl.BlockSpec((1, d), lambda i, k, *_: (0, 0))
    x1_spec = pl.BlockSpec((tm, d), lambda i, k, *_: (i, 0), pipeline_mode=pl.Buffered(1))
    in_specs = [pl.BlockSpec((tm, tk), lambda i, k, *_: (i, k)),
                pl.BlockSpec((tk, d), lambda i, k, *_: (k, 0)), x1_spec, const]
    args = [hid, w_down, x1, g_post]
    out_specs = [row]
    out_shape = [jax.ShapeDtypeStruct((m, d), F32)]
    scratch = [pltpu.VMEM((tm, d), F32)]
    if emit_next:
        in_specs.append(const)
        args.append(g_next)
        out_specs.append(row)
        out_shape.append(jax.ShapeDtypeStruct((m, d), BF16))
    n_prefetch = 0
    if dec is not None:
        assert dec.n_items == n_i * n_k
        dec_specs, dec_args, dec_out_spec, dec_out_shape, dec_scratch = dec.operands(lambda i, k: i * n_k + k)
        in_specs += dec_specs
        args = [dec.page_table] + args + dec_args
        out_specs.append(dec_out_spec)
        out_shape.append(dec_out_shape)
        scratch += dec_scratch
        n_prefetch = 1
    grid_spec = pltpu.PrefetchScalarGridSpec(
        num_scalar_prefetch=n_prefetch, grid=(n_i, n_k), in_specs=in_specs, out_specs=out_specs,
        scratch_shapes=scratch)
    out = pl.pallas_call(
        functools.partial(_down_kernel, emit_next=emit_next, dec=None if dec is None else dec.static),
        grid_spec=grid_spec,
        out_shape=out_shape,
        compiler_params=_params(("arbitrary", "arbitrary"), 60),
        name="down_norm_residual",
    )(*args)
    return out[0], (out[1] if emit_next else None), (out[-1] if dec is not None else None)


def _prefix_kernel(zf_ref, bf_ref, logf_ref, cq_ref, ct_ref, c_sc, *, t, n_heads):
    lf = _log_sigmoid(zf_ref[0] + bf_ref[...])
    logf_ref[0] = lf[:, 0:n_heads]
    r_i = lax.broadcasted_iota(jnp.int32, (CHUNK, CHUNK), 0)
    c_i = lax.broadcasted_iota(jnp.int32, (CHUNK, CHUNK), 1)
    tri = _ones_where(r_i >= c_i)
    carry = jnp.zeros((1, 128), F32)
    for blk in range(t // CHUNK):
        cb = _dot3_left(tri, lf[blk * CHUNK:(blk + 1) * CHUNK, :]) + carry
        c_sc[blk * CHUNK:(blk + 1) * CHUNK, :] = cb * LOG2E
        carry = cb[CHUNK - 1:CHUNK, :]
    c = c_sc[...]
    ct_ref[0] = c.T[0:n_heads, :]
    for h in range(n_heads):
        cq_ref[0, h] = c[:, h:h + 1]


def _fox_prefix(z3, bf_pad, n_heads, f_blk):
    b, t, _ = z3.shape
    return pl.pallas_call(
        functools.partial(_prefix_kernel, t=t, n_heads=n_heads),
        grid=(b,),
        in_specs=[pl.BlockSpec((1, t, 128), lambda i: (i, 0, f_blk)), pl.BlockSpec((1, 128), lambda i: (0, 0))],
        out_specs=[
            pl.BlockSpec((1, t, n_heads), lambda i: (i, 0, 0)),
            pl.BlockSpec((1, n_heads, t, 1), lambda i: (i, 0, 0, 0)),
            pl.BlockSpec((1, n_heads, t), lambda i: (i, 0, 0)),
        ],
        out_shape=[
            jax.ShapeDtypeStruct((b, t, n_heads), F32),
            jax.ShapeDtypeStruct((b, n_heads, t, 1), F32),
            jax.ShapeDtypeStruct((b, n_heads, t), F32),
        ],
        scratch_shapes=[pltpu.VMEM((t, 128), F32)],
        compiler_params=_params(("arbitrary",), 48),
        name="fox_prefix",
    )(z3, bf_pad)


def _fox_attn_kernel(qi_ref, ki_ref, q_ref, k_ref, v_ref, cq_ref, ck_ref, o_ref, m_sc, l_sc, acc_sc, cq_sc,
                     *, tq, n_heads):
    step = pl.program_id(1)
    qi = qi_ref[step]
    ki = ki_ref[step]
    n_sub = tq // 128

    @pl.when(ki == 0)
    def _():
        m_sc[...] = jnp.full_like(m_sc, -jnp.inf)
        l_sc[...] = jnp.zeros_like(l_sc)
        acc_sc[...] = jnp.zeros_like(acc_sc)
        for h in range(n_heads):
            cq_sc[h] = jnp.broadcast_to(cq_ref[0, h], (tq, 128))

    def block(diagonal):
        if diagonal:
            keep = (lax.broadcasted_iota(jnp.int32, (tq, tq), 0) >= lax.broadcasted_iota(jnp.int32, (tq, tq), 1))
        for h in range(n_heads):
            hs = slice(h * HEAD_DIM, (h + 1) * HEAD_DIM)
            s = _dot_nt(q_ref[0, :, hs], k_ref[0, :, hs]) - ck_ref[0, h:h + 1, :]
            if diagonal:
                s = jnp.where(keep, s, -jnp.inf)
            subs = [s[:, j * 128:(j + 1) * 128] for j in range(n_sub)]
            mc = subs[0]
            for x in subs[1:]:
                mc = jnp.maximum(mc, x)
            cq = cq_sc[h]
            m_prev = m_sc[h]
            m_new = jnp.maximum(m_prev, jnp.max(mc, axis=1, keepdims=True) + cq)
            alpha = jnp.exp2(m_prev - m_new)
            shift = m_new - cq
            ps = [jnp.exp2(x - shift) for x in subs]
            lsum = ps[0]
            for x in ps[1:]:
                lsum = lsum + x
            l_sc[h] = alpha * l_sc[h] + jnp.sum(lsum, axis=1, keepdims=True)
            p = jnp.concatenate(ps, axis=1).astype(BF16)
            acc_sc[:, hs] = alpha * acc_sc[:, hs] + _dot(p, v_ref[0, :, hs])
            m_sc[h] = m_new

    @pl.when(ki < qi)
    def _():
        block(False)

    @pl.when(ki == qi)
    def _():
        block(True)
        for h in range(n_heads):
            hs = slice(h * HEAD_DIM, (h + 1) * HEAD_DIM)
            o_ref[0, :, hs] = (acc_sc[:, hs] / l_sc[h]).astype(o_ref.dtype)


def _fox_attn_prompt(q, k, v, cq, ct, n_heads, tq):
    b, t, d_a = q.shape
    nq = t // tq
    pairs = [(i, j) for i in range(nq) for j in range(i + 1)]
    qi_tab = jnp.asarray([pr[0] for pr in pairs], jnp.int32)
    ki_tab = jnp.asarray([pr[1] for pr in pairs], jnp.int32)
    q_blk = pl.BlockSpec((1, tq, d_a), lambda bi, s, qt, kt: (bi, qt[s], 0))
    kv_blk = pl.BlockSpec((1, tq, d_a), lambda bi, s, qt, kt: (bi, kt[s], 0))
    grid_spec = pltpu.PrefetchScalarGridSpec(
        num_scalar_prefetch=2,
        grid=(b, len(pairs)),
        in_specs=[
            q_blk, kv_blk, kv_blk,
            pl.BlockSpec((1, n_heads, tq, 1), lambda bi, s, qt, kt: (bi, 0, qt[s], 0)),
            pl.BlockSpec((1, n_heads, tq), lambda bi, s, qt, kt: (bi, 0, kt[s])),
        ],
        out_specs=q_blk,
        scratch_shapes=[
            pltpu.VMEM((n_heads, tq, 128), F32),
            pltpu.VMEM((n_heads, tq, 128), F32),
            pltpu.VMEM((tq, d_a), F32),
            pltpu.VMEM((n_heads, tq, 128), F32),
        ],
    )
    return pl.pallas_call(
        functools.partial(_fox_attn_kernel, tq=tq, n_heads=n_heads),
        grid_spec=grid_spec,
        out_shape=jax.ShapeDtypeStruct((b, t, d_a), BF16),
        compiler_params=_params(("arbitrary", "arbitrary"), 48),
        name="fox_attn_prompt",
    )(qi_tab, ki_tab, q, k, v, cq, ct)


def _layernorm_silu(y, g, b):
    mu = jnp.mean(y, axis=-1, keepdims=True)
    yc = y - mu
    yn = yc * lax.rsqrt(jnp.mean(yc * yc, axis=-1, keepdims=True) + EPS) * g + b
    return yn * _sigmoid(yn)


def _mixer_kernel(ub_ref, vb_ref, ac_ref, gc_ref, gv_ref, ws_ref, bst_ref, cw_ref, cb_ref, lg_ref, lb_ref,
                  sgu_ref, conv_ref, tail_ref, g_sc, *, tm, n_heads_b, conv_w, halo):
    ti = pl.program_id(1)

    @pl.when(ti == 0)
    def _():
        g_sc[0:halo, :] = jnp.zeros((halo, g_sc.shape[1]), F32)

    u = _gelu(ub_ref[0])
    vn = _rms(_gelu(vb_ref[0]), gv_ref[...])
    r_i = lax.broadcasted_iota(jnp.int32, (CHUNK, CHUNK), 0)
    c_i = lax.broadcasted_iota(jnp.int32, (CHUNK, CHUNK), 1)
    for h in range(n_heads_b):
        hs = slice(h * HEAD_DIM, (h + 1) * HEAD_DIM)
        w_h = jnp.where(r_i >= c_i, ws_ref[h], 0.0).astype(BF16)
        bias_h = bst_ref[:, h:h + 1]
        for c in range(tm // CHUNK):
            rs = slice(c * CHUNK, (c + 1) * CHUNK)
            mix = _dot(w_h, vn[rs, hs].astype(BF16)) + bias_h
            sgu_ref[0, rs, hs] = (u[rs, hs] * mix).astype(sgu_ref.dtype)

    g_sc[halo:halo + tm, :] = ac_ref[0] * _sigmoid(gc_ref[0])
    base = halo - (conv_w - 1)
    n_rows = halo + tm
    window = g_sc[...]
    rotated = {0: window}
    y = cb_ref[...]
    for k in range(conv_w):
        a, r = divmod(base + k, 8)
        if r not in rotated:
            rotated[r] = pltpu.roll(window, n_rows - r, 0)
        y = y + rotated[r][8 * a:8 * a + tm, :] * cw_ref[k:k + 1, :]
    conv_ref[0] = _layernorm_silu(y, lg_ref[...], lb_ref[...]).astype(conv_ref.dtype)

    @pl.when(ti == pl.num_programs(1) - 1)
    def _():
        tail_ref[0] = g_sc[halo + tm - (conv_w - 1):halo + tm, :]

    g_sc[0:halo, :] = g_sc[tm:tm + halo, :]


def _mixer_prompt(z3, g_v, w_s, b_s_t, conv_w, conv_b, ln_g, ln_b, tm):
    b, t, _ = z3.shape
    n_heads_b = w_s.shape[0]
    d_b = n_heads_b * HEAD_DIM
    d_c = conv_w.shape[1]
    kw = conv_w.shape[0]
    halo = 32
    zcol = lambda c: pl.BlockSpec((1, tm, d_b), lambda bi, ti: (bi, ti, c))
    const = lambda shape: pl.BlockSpec(shape, lambda bi, ti: (0,) * len(shape))
    return pl.pallas_call(
        functools.partial(_mixer_kernel, tm=tm, n_heads_b=n_heads_b, conv_w=kw, halo=halo),
        grid=(b, t // tm),
        in_specs=[zcol(0), zcol(1), zcol(2), zcol(3),
                  const((1, d_b)), const((n_heads_b, CHUNK, CHUNK)), const((CHUNK, n_heads_b)),
                  const((kw, d_c)), const((1, d_c)), const((1, d_c)), const((1, d_c))],
        out_specs=[
            pl.BlockSpec((1, tm, d_b), lambda bi, ti: (bi, ti, 0)),
            pl.BlockSpec((1, tm, d_c), lambda bi, ti: (bi, ti, 0)),
            pl.BlockSpec((1, kw - 1, d_c), lambda bi, ti: (bi, 0, 0)),
        ],
        out_shape=[
            jax.ShapeDtypeStruct((b, t, d_b), BF16),
            jax.ShapeDtypeStruct((b, t, d_c), BF16),
            jax.ShapeDtypeStruct((b, kw - 1, d_c), F32),
        ],
        scratch_shapes=[pltpu.VMEM((halo + tm, d_c), F32)],
        compiler_params=_params(("arbitrary", "arbitrary"), 32),
        name="mixer_prompt",
    )(z3, z3, z3, z3, g_v, w_s, b_s_t, conv_w, conv_b, ln_g, ln_b)


def _mixer_sample_kernel(zs_ref, zf_ref, bf_ref, gv_ref, w0_ref, b0_ref, st_ref, cw_ref, cb_ref, lg_ref, lb_ref,
                         sgu_ref, conv_ref, state_ref, vn_ref, logf_ref, y_sc, *, d_b, conv_w):
    n = zs_ref.shape[0]
    u = _gelu(zs_ref[:, 0:d_b])
    vn = _rms(_gelu(zs_ref[:, d_b:2 * d_b]), gv_ref[...])
    vn_ref[...] = vn
    sgu_ref[...] = (u * (w0_ref[...] * vn + b0_ref[...])).astype(sgu_ref.dtype)
    glu = zs_ref[:, 2 * d_b:3 * d_b] * _sigmoid(zs_ref[:, 3 * d_b:4 * d_b])
    kw = conv_w - 1
    for bi in range(n):
        g_new = glu[bi:bi + 1, :]
        y_sc[bi:bi + 1, :] = (jnp.sum(st_ref[bi] * cw_ref[0:kw, :], axis=0, keepdims=True)
                              + g_new * cw_ref[kw:kw + 1, :])
        state_ref[bi, 0:kw - 1, :] = st_ref[bi, 1:kw, :]
        state_ref[bi, kw - 1:kw, :] = g_new
    conv_ref[...] = _layernorm_silu(y_sc[...] + cb_ref[...], lg_ref[...], lb_ref[...]).astype(conv_ref.dtype)
    logf_ref[...] = _log_sigmoid(zf_ref[...] + bf_ref[...])


def _mixer_sample(zs, zf, bf_pad, g_v, w0_row, b0_row, state, conv_w, conv_b, ln_g, ln_b):
    n = zs.shape[0]
    d_b = g_v.shape[1]
    kw, d_c = conv_w.shape
    full = lambda shape: pl.BlockSpec(shape, lambda i: (0,) * len(shape))
    args = (zs, zf, bf_pad, g_v, w0_row, b0_row, state, conv_w, conv_b, ln_g, ln_b)
    out_shape = [
        jax.ShapeDtypeStruct((n, d_b), BF16),
        jax.ShapeDtypeStruct((n, d_c), BF16),
        jax.ShapeDtypeStruct((n, kw - 1, d_c), F32),
        jax.ShapeDtypeStruct((n, d_b), F32),
        jax.ShapeDtypeStruct((n, 128), F32),
    ]
    return pl.pallas_call(
        functools.partial(_mixer_sample_kernel, d_b=d_b, conv_w=kw),
        grid=(1,),
        in_specs=[full(a.shape) for a in args],
        out_specs=[full(s.shape) for s in out_shape],
        out_shape=out_shape,
        scratch_shapes=[pltpu.VMEM((n, d_c), F32)],
        compiler_params=_params(("arbitrary",), 32),
        name="mixer_sample",
    )(*args)


def _decode_step(phase, p, dec, in_refs, o_ref, state, carry=None):
    n_grp, n_heads, scale = dec.n_grp, dec.n_heads, dec.scale
    zs_ref, cn_ref = in_refs[0], in_refs[1]
    refs = in_refs[2:]
    k_refs, v_refs, lf_refs = refs[:n_grp], refs[n_grp:2 * n_grp], refs[2 * n_grp:3 * n_grp]
    cn_sc, m_sc, l_sc, acc_sc, tail_sc = state
    rows_pp = k_refs[0].shape[2]
    n_flat = rows_pp // 128
    lane = lax.broadcasted_iota(jnp.int32, (n_heads, 128), 1)
    sub = lax.broadcasted_iota(jnp.int32, (n_heads, 128), 0)
    cls_mask = n_heads - 1
    own = sub == (lane & cls_mask)
    diag = sub == lane
    r_i = lax.broadcasted_iota(jnp.int32, (128, 128), 0)
    c_i = lax.broadcasted_iota(jnp.int32, (128, 128), 1)
    same = (r_i & cls_mask) == (c_i & cls_mask)

    if phase == "init":
        @pl.when(p == 0)
        def _():
            spread = _ones_where((r_i < n_heads) & ((c_i & cls_mask) == r_i))
            cn_sc[...] = _dot3(jnp.broadcast_to(cn_ref[0], (n_heads, 128)), spread)
            m_sc[...] = jnp.full_like(m_sc, -jnp.inf)
            l_sc[...] = jnp.zeros_like(l_sc)
            acc_sc[...] = jnp.zeros_like(acc_sc)
            tail_sc[...] = jnp.zeros_like(tail_sc)
        return

    if phase == "finish":
        @pl.when(p == dec.n_steps - 1)
        def _():
            m_col = jnp.max(jnp.where(diag, m_sc[...], -jnp.inf), axis=1, keepdims=True)
            l_row = jnp.sum(l_sc[...], axis=0, keepdims=True)
            l_col = jnp.sum(jnp.where(own, jnp.broadcast_to(l_row, (n_heads, 128)), 0.0), axis=1, keepdims=True)
            cn_col = jnp.sum(jnp.where(diag, cn_sc[...], 0.0), axis=1, keepdims=True)
            q_new = zs_ref[0, 0:n_heads, :]
            k_new = zs_ref[0, n_heads:2 * n_heads, :]
            v_new = zs_ref[0, 2 * n_heads:3 * n_heads, :]
            s_new = jnp.sum(q_new * k_new, axis=1, keepdims=True) * scale + cn_col - cn_col
            m_f = jnp.maximum(m_col, s_new)
            a_f = jnp.exp(m_col - m_f)
            p_new = jnp.exp(s_new - m_f)
            o_ref[0] = (a_f * acc_sc[...] + p_new * v_new) / (a_f * l_col + p_new)
        return

    if phase == "values":
        return _decode_values(dec, carry, v_refs, state, own, diag, n_flat)

    lf = jnp.concatenate([lf_refs[g][0, 0] for g in range(n_grp)], axis=0)
    n_rows = n_grp * n_flat
    row_tot = _dot3(lf, _ones_where(same))
    within = _dot3(lf, _ones_where(same & (r_i > c_i)))
    rr = lax.broadcasted_iota(jnp.int32, (n_rows, n_rows), 0)
    cc = lax.broadcasted_iota(jnp.int32, (n_rows, n_rows), 1)
    suffix = within + _dot3_left(_ones_where(cc > rr), row_tot) + tail_sc[0:1, :]

    q8 = zs_ref[0, 0:n_heads, :].astype(BF16)
    scores = []
    for g in range(n_grp):
        s_t = _dot_nt(q8, k_refs[g][0, 0].astype(BF16))
        flat = [jnp.sum(jnp.where(own, s_t[:, a * 128:(a + 1) * 128], 0.0), axis=0, keepdims=True)
                for a in range(n_flat)]
        scores.append(jnp.concatenate(flat, axis=0) * scale
                      + suffix[g * n_flat:(g + 1) * n_flat, :] + cn_sc[...])
    return scores, row_tot


def _decode_values(dec, carry, v_refs, state, own, diag, n_flat):
    n_grp, n_heads = dec.n_grp, dec.n_heads
    cn_sc, m_sc, l_sc, acc_sc, tail_sc = state
    scores, row_tot = carry
    m_step = scores[0]
    for x in scores[1:]:
        m_step = jnp.maximum(m_step, x)
    shift = 1
    while shift < n_flat:
        m_step = jnp.maximum(m_step, pltpu.roll(m_step, shift, 0))
        shift *= 2
    shift = n_heads
    while shift < 128:
        m_step = jnp.maximum(m_step, pltpu.roll(m_step, shift, 1))
        shift *= 2
    m_prev = m_sc[...]
    m_new = jnp.maximum(m_prev, m_step)
    alpha = jnp.exp(m_prev - m_new)
    probs = [jnp.exp(x - m_new) for x in scores]
    l_step = probs[0]
    for x in probs[1:]:
        l_step = l_step + x
    l_sc[...] = alpha * l_sc[...] + l_step
    pv = jnp.zeros((n_heads, HEAD_DIM), F32)
    for g in range(n_grp):
        blocks = [jnp.where(own, jnp.broadcast_to(probs[g][a:a + 1, :], (n_heads, 128)), 0.0)
                  for a in range(n_flat)]
        pv += _dot(jnp.concatenate(blocks, axis=1).astype(BF16), v_refs[g][0, 0].astype(BF16))
    alpha_col = jnp.sum(jnp.where(diag, alpha, 0.0), axis=1, keepdims=True)
    acc_sc[...] = alpha_col * acc_sc[...] + pv
    m_sc[...] = m_new
    tail_sc[...] += jnp.sum(row_tot, axis=0, keepdims=True)


class _DecodeStatic(NamedTuple):
    n_grp: int
    n_steps: int
    n_items: int
    n_heads: int
    scale: float


class _DecodePlan:
    def __init__(self, page_table, zs3, cn, cache_k, cache_v, cache_lf, layer, n_heads, b0, nb, n_grp):
        n_pages = page_table.shape[1]
        assert n_pages % n_grp == 0 and cache_k.shape[2] == 128 * n_heads and cache_lf.shape[2] == n_heads
        assert n_heads & (n_heads - 1) == 0 and n_heads <= 8
        self.page_table, self.layer, self.b0, self.nb, self.n_pages = page_table, layer, b0, nb, n_pages
        self.arrays = (zs3, cn, cache_k, cache_v, cache_lf)
        n_steps = n_pages // n_grp
        self.static = _DecodeStatic(n_grp, n_steps, nb * n_steps, n_heads, HEAD_DIM ** -0.5)
        self.n_items = nb * n_steps

    def operands(self, step_of):
        zs3, cn, cache_k, cache_v, cache_lf = self.arrays
        st, b0, layer, n_pages = self.static, self.b0, self.layer, self.n_pages
        n_heads, n_grp, n_steps = st.n_heads, st.n_grp, st.n_steps
        rows_pp = cache_k.shape[2]
        seq = lambda g0, g1: step_of(g0, g1) // n_steps

        def page_spec(block, g):
            def index(g0, g1, pt):
                s = step_of(g0, g1)
                return (pt[b0 + s // n_steps, n_pages - (s % n_steps + 1) * n_grp + g], layer, 0, 0)
            return pl.BlockSpec(block, index)

        kv_specs = [page_spec((1, 1, rows_pp, HEAD_DIM), g) for g in range(n_grp)]
        lf_specs = [page_spec((1, 1, n_heads, 128), g) for g in range(n_grp)]
        in_specs = [pl.BlockSpec((1, 3 * n_heads, HEAD_DIM), lambda g0, g1, pt: (b0 + seq(g0, g1), 0, 0)),
                    pl.BlockSpec((1, 1, 128), lambda g0, g1, pt: (b0 + seq(g0, g1), 0, 0))]
        in_specs += kv_specs + kv_specs + lf_specs
        args = [zs3, cn] + [cache_k] * n_grp + [cache_v] * n_grp + [cache_lf] * n_grp
        out_spec = pl.BlockSpec((1, n_heads, HEAD_DIM), lambda g0, g1, pt: (seq(g0, g1), 0, 0))
        out_shape = jax.ShapeDtypeStruct((self.nb, n_heads, HEAD_DIM), F32)
        scratch = [pltpu.VMEM((n_heads, 128), F32)] * 3 + [pltpu.VMEM((n_heads, HEAD_DIM), F32),
                                                           pltpu.VMEM((n_heads, 128), F32)]
        return in_specs, args, out_spec, out_shape, scratch


def kernel(x_prompt, x_sample, cache_k, cache_v, cache_logf, state_conv, page_table, w_in, b_f, g_v, w_s, b_s,
           conv_w, conv_b, ln_c_g, ln_c_b, w_o, g_pre_mix, g_post_mix, g_pre_mlp, g_post_mlp, w_up, w_down):
    bp, tp, d = x_prompt.shape
    bs, ds, _ = x_sample.shape
    assert ds == 1, "the sample path handles exactly one new token per sequence"
    depth = w_in.shape[0]
    n_pool, _, page, n_heads, _ = cache_k.shape
    d_a = n_heads * HEAD_DIM
    n_heads_b = w_s.shape[1]
    d_b = n_heads_b * HEAD_DIM
    d_c = conv_w.shape[2]
    off_f = 3 * d_a
    assert w_in.shape[2] == off_f + n_heads + 2 * d_b + 2 * d_c and d_b == d_c and page == CHUNK
    mp = bp * tp

    n_rest = 2 * d_b + 2 * d_c
    w_in_t = jnp.swapaxes(w_in, 1, 2)
    d_ff = w_up.shape[2]
    bf_pad = jnp.pad(b_f, ((0, 0), (0, 128 - n_heads)))
    row2 = lambda a, l: a[l][None, :]
    proj = functools.partial(_ws_matmul, transposed=True)

    n_pages = page_table.shape[1]
    assert bs % 2 == 0
    grp_up = (bs // 2) * n_pages // ((d_ff // UP_TILE[1]) * (mp // UP_TILE[0]))
    grp_down = (bs // 2) * n_pages // ((mp // DOWN_TILE[0]) * (d_ff // DOWN_TILE[1]))

    cache_k2 = cache_k.reshape(n_pool, depth, page * n_heads, HEAD_DIM)
    cache_v2 = cache_v.reshape(n_pool, depth, page * n_heads, HEAD_DIM)
    cache_lf = cache_logf.reshape(n_pool, depth, page * n_heads // 128, 128)

    xp = x_prompt.reshape(mp, d)
    xs = x_sample.reshape(bs, d)
    xpn = _norm_bf16(xp, row2(g_pre_mix, 0), 512)
    xsn = _norm_bf16(xs, row2(g_pre_mix, 0), bs)
    fp_l, cp_l, ks_l, vs_l, fs_l, cs_l, us_l = ([] for _ in range(7))
    k_pages = v_pages = None
    for l in range(depth):
        g_next = row2(g_pre_mix, l + 1) if l + 1 < depth else None
        q, zf = proj(xpn, w_in_t, l, 0, d_a, 1024, d_a, "q", "proj_q", f_row=off_f)
        kb, k_pages = proj(xpn, w_in_t, l, d_a, d_a, 1024, d_a, "kv", "proj_k",
                           pages=(k_pages, bp, tp, depth, n_heads, page))
        vb, v_pages = proj(xpn, w_in_t, l, 2 * d_a, d_a, 1024, d_a, "kv", "proj_v",
                           pages=(v_pages, bp, tp, depth, n_heads, page))
        z_rest, w_o_l = proj(xpn, w_in_t, l, off_f + n_heads, n_rest, 1024, 1024, "f32", "proj_rest", side_cast=w_o)
        z3 = z_rest.reshape(bp, tp, n_rest)
        logf, cq, ct = _fox_prefix(zf.reshape(bp, tp, 128), bf_pad[l][None, :], n_heads, 0)
        attn = _fox_attn_prompt(q.reshape(bp, tp, d_a), kb.reshape(bp, tp, d_a), vb.reshape(bp, tp, d_a),
                                cq, ct, n_heads, 512)
        fp_l.append(logf)
        zs_q, zs_f = proj(xsn, w_in_t, l, 0, d_a, bs, d_a, "f32", "proj_q_sample", f_row=off_f)
        zs_kv = proj(xsn, w_in_t, l, d_a, 2 * d_a, bs, d_a, "f32", "proj_kv_sample")
        zs_rest = proj(xsn, w_in_t, l, off_f + n_heads, n_rest, bs, 1024, "f32", "proj_rest_sample")
        w0_row = jnp.repeat(w_s[l][:, 0, 0], HEAD_DIM)[None, :]
        b0_row = jnp.repeat(b_s[l][:, 0], HEAD_DIM)[None, :]
        sgu_s, conv_s, state_s, vn_s, logf_s = _mixer_sample(
            zs_rest, zs_f, bf_pad[l][None, :], row2(g_v, l), w0_row, b0_row, state_conv[l], conv_w[l],
            row2(conv_b, l), row2(ln_c_g, l), row2(ln_c_b, l))
        zs3 = jnp.concatenate([zs_q, zs_kv], axis=1).reshape(bs, 3 * n_heads, HEAD_DIM)
        plan = lambda b0, n_grp: _DecodePlan(page_table, zs3, logf_s[:, None, :], cache_k2, cache_v2, cache_lf,
                                             l, n_heads, b0, bs // 2, n_grp)
        sgu, conv, conv_tail = _mixer_prompt(z3, row2(g_v, l), w_s[l], b_s[l].T, conv_w[l], row2(conv_b, l),
                                             row2(ln_c_g, l), row2(ln_c_b, l), MIXER_TM)
        cp_l.append(conv_tail)
        x1, x1n = _wo_block(attn.reshape(mp, d_a), sgu.reshape(mp, d_b), conv.reshape(mp, d_c), w_o_l, xp,
                            row2(g_post_mix, l), row2(g_pre_mlp, l), 512)
        hid, w_down_l, attn_s0 = _ws_matmul(x1n, w_up, l, 0, d_ff, UP_TILE[0], UP_TILE[1], "relu2", "mlp_up",
                                            side_cast=w_down, dec=plan(0, grp_up))
        xp, xpn, attn_s1 = _down_block(hid, w_down_l, x1, row2(g_post_mlp, l), g_next, DOWN_TILE[0],
                                       DOWN_TILE[1], dec=plan(bs // 2, grp_down))
        attn_s = jnp.concatenate([attn_s0, attn_s1], axis=0).reshape(bs, d_a).astype(BF16)
        x1s, x1sn = _wo_block(attn_s, sgu_s, conv_s, w_o_l, xs, row2(g_post_mix, l), row2(g_pre_mlp, l), bs)
        hid_s = _ws_matmul(x1sn, w_up, l, 0, d_ff, bs, 1024, "relu2", "mlp_up_sample")
        xs, xsn, _ = _down_block(hid_s, w_down_l, x1s, row2(g_post_mlp, l), g_next, bs, 2048)
        ks_l.append(zs_kv[:, :d_a])
        vs_l.append(zs_kv[:, d_a:])
        fs_l.append(logf_s[:, :n_heads])
        cs_l.append(state_s)
        us_l.append(vn_s)

    npp = tp // page
    kv_out = lambda pages: pages.reshape(bp, npp, depth, page, n_heads, HEAD_DIM)
    logf_prompt = jnp.stack(fp_l, axis=1).reshape(bp, depth, npp, page, n_heads).swapaxes(1, 2)
    return (xp.reshape(bp, tp, d), xs.reshape(bs, ds, d),
            kv_out(k_pages), kv_out(v_pages), logf_prompt,
            jnp.stack(cp_l, axis=0),
            jnp.stack(ks_l, axis=1).reshape(bs, depth, ds, n_heads, HEAD_DIM),
            jnp.stack(vs_l, axis=1).reshape(bs, depth, ds, n_heads, HEAD_DIM),
            jnp.stack(fs_l, axis=1).reshape(bs, depth, ds, n_heads),
            jnp.stack(cs_l, axis=0),
            jnp.stack(us_l, axis=0).reshape(depth, bs, ds, d_b))
```

```python
import functools
from typing import NamedTuple

import numpy as np
import jax
import jax.numpy as jnp
from jax import lax
from jax.experimental import pallas as pl
from jax.experimental.pallas import tpu as pltpu

F32 = jnp.float32
BF16 = jnp.bfloat16

HEAD_DIM = 128
CHUNK = 128
EPS = 1e-6
MIB = 1024 * 1024
GELU_C = float(np.sqrt(2.0 / np.pi))
LOG2E = float(np.log2(np.e))

UP_TILE = (1024, 1024)
DOWN_TILE = (512, 2048)
MIXER_TM = 256


def _params(semantics, vmem_mib):
    return pltpu.CompilerParams(dimension_semantics=semantics, vmem_limit_bytes=vmem_mib * MIB)


def _rms(x, g):
    return x * lax.rsqrt(jnp.mean(x * x, axis=-1, keepdims=True) + EPS) * g


def _gelu(x):
    return x * (0.5 * (1.0 + jnp.tanh(GELU_C * (x + 0.044715 * (x * x * x)))))


def _sigmoid(x):
    return 1.0 / (1.0 + jnp.exp(-x))


def _log_sigmoid(x):
    return jnp.minimum(x, 0.0) - jnp.log1p(jnp.exp(-jnp.abs(x)))


def _split3(x):
    hi = x.astype(BF16)
    r = x - hi.astype(F32)
    mid = r.astype(BF16)
    lo = (r - mid.astype(F32)).astype(BF16)
    return hi, mid, lo


def _dot(a, b):
    return jnp.dot(a, b, preferred_element_type=F32)


def _dot_nt(a, b):
    return lax.dot_general(a, b, (((1,), (1,)), ((), ())), preferred_element_type=F32)


def _dot3(x, w):
    hi, mid, lo = _split3(x)
    return (_dot(hi, w) + _dot(mid, w)) + _dot(lo, w)


def _dot3_left(w, x):
    hi, mid, lo = _split3(x)
    return (_dot(w, hi) + _dot(w, mid)) + _dot(w, lo)


def _ones_where(cond):
    return jnp.where(cond, 1.0, 0.0).astype(BF16)


def _norm_kernel(x_ref, g_ref, o_ref):
    o_ref[...] = _rms(x_ref[...], g_ref[...]).astype(o_ref.dtype)


def _norm_bf16(x, g, tm):
    m, d = x.shape
    return pl.pallas_call(
        _norm_kernel,
        grid=(m // tm,),
        in_specs=[pl.BlockSpec((tm, d), lambda i: (i, 0)), pl.BlockSpec((1, d), lambda i: (0, 0))],
        out_specs=pl.BlockSpec((tm, d), lambda i: (i, 0)),
        out_shape=jax.ShapeDtypeStruct((m, d), BF16),
        compiler_params=_params(("arbitrary",), 32),
        name="pre_norm",
    )(x, g)


def _ws_kernel(*refs, mode, transposed, with_f, has_prev, kv_slot, side_cast, dec, n_heads, page, q_scale):
    refs = list(refs)
    if dec is not None:
        refs = refs[1:]
    x_ref, w_ref = refs[0], refs[1]
    pos = 2
    if with_f:
        wf_ref = refs[pos]
        pos += 1
    if has_prev:
        pos += 1
    if side_cast:
        side_in = refs[pos]
        pos += 1
    if dec is not None:
        dec_in = refs[pos:pos + 2 + 3 * dec.n_grp]
        pos += 2 + 3 * dec.n_grp
    n_out = 2 if (mode == "kv" or with_f) else 1
    outs = refs[pos:pos + n_out]
    pos += n_out
    if side_cast:
        refs[pos][...] = side_in[...].astype(BF16)
        pos += 1
    if dec is not None:
        dec_out = refs[pos]
        pos += 1
    w_sc = refs[pos]
    pos += 1
    if with_f:
        wf_sc = refs[pos]
        pos += 1
    dec_state = refs[pos:]

    @pl.when(pl.program_id(1) == 0)
    def _():
        w_sc[...] = (w_ref[0] if transposed else w_ref[...]).astype(BF16)
        if with_f:
            pad = jnp.zeros((wf_sc.shape[0] - wf_ref.shape[1], wf_sc.shape[1]), F32)
            wf_sc[...] = jnp.concatenate([wf_ref[0], pad], axis=0).astype(BF16)

    if dec is not None:
        dec_p = lax.rem(pl.program_id(0) * pl.num_programs(1) + pl.program_id(1), dec.n_steps)
        run_decode = functools.partial(_decode_step, p=dec_p, dec=dec, in_refs=dec_in, o_ref=dec_out,
                                       state=dec_state)
        run_decode("init")
        dec_carry = run_decode("scores")
    x = x_ref[...]
    acc = _dot_nt(x, w_sc[...]) if transposed else _dot(x, w_sc[...])
    if with_f:
        outs[1][...] = _dot_nt(x, wf_sc[...])
    if mode == "f32":
        outs[0][...] = acc
    elif mode == "q":
        outs[0][...] = (acc * q_scale).astype(BF16)
    elif mode == "relu2":
        h = jnp.maximum(acc, 0.0)
        outs[0][...] = (h * h).astype(BF16)
    else:
        outs[0][...] = acc.astype(BF16)
        dst_ref = outs[1]
        slot = () if has_prev else (kv_slot,)
        for pg in range(acc.shape[0] // page):
            for h in range(n_heads):
                dst_ref[(0, pg) + slot + (pl.ds(h, page, stride=n_heads), slice(None))] = (
                    acc[pg * page:(pg + 1) * page, h * HEAD_DIM:(h + 1) * HEAD_DIM])
        if not has_prev:
            for other in range(dst_ref.shape[2]):
                if other != kv_slot:
                    dst_ref[0, :, other] = jnp.zeros(dst_ref.shape[1:2] + dst_ref.shape[3:], F32)
    if dec is not None:
        run_decode("values", carry=dec_carry)
        run_decode("finish")


def _ws_matmul(x, w, layer, col0, n_cols, tm, tn, mode, name, *, transposed=False, f_row=None, pages=None,
               side_cast=None, dec=None):
    m, k = x.shape
    assert n_cols % tn == 0 and (f_row is None or (n_cols == tn and transposed))
    n_j, n_i = n_cols // tn, m // tm
    if transposed:
        w_spec = pl.BlockSpec((pl.Element(1), pl.Element(tn), pl.Element(k)),
                              lambda j, i, *_: (layer, pl.multiple_of(col0 + j * tn, 8), 0))
    else:
        assert col0 % tn == 0
        w_spec = pl.BlockSpec((None, k, tn), lambda j, i, *_: (layer, 0, col0 // tn + j))
    in_specs = [pl.BlockSpec((tm, k), lambda j, i, *_: (i, 0)), w_spec]
    args = [x, w]
    out_specs = [pl.BlockSpec((tm, tn), lambda j, i, *_: (i, j))]
    out_shape = [jax.ShapeDtypeStruct((m, n_cols), F32 if mode == "f32" else BF16)]
    scratch = [pltpu.VMEM((tn, k) if transposed else (k, tn), BF16)]
    if f_row is not None:
        in_specs.append(pl.BlockSpec((pl.Element(1), pl.Element(8), pl.Element(k)),
                                     lambda j, i, *_: (layer, f_row, 0)))
        args.append(w)
        out_specs.append(pl.BlockSpec((tm, 128), lambda j, i, *_: (i, 0)))
        out_shape.append(jax.ShapeDtypeStruct((m, 128), F32))
        scratch.append(pltpu.VMEM((128, k), BF16))
    aliases = {}
    n_heads = page = 0
    has_prev = False
    if mode == "kv":
        prev, bsz, t, depth, n_heads, page = pages
        assert n_cols == tn == n_heads * HEAD_DIM and t % tm == 0 and tm % page == 0
        tps = t // tm
        if prev is None:
            out_specs.append(pl.BlockSpec((1, tm // page, depth, page * n_heads, HEAD_DIM),
                                          lambda j, i, *_: (i // tps, i % tps, 0, 0, 0)))
        else:
            out_specs.append(pl.BlockSpec((1, tm // page, None, page * n_heads, HEAD_DIM),
                                          lambda j, i, *_: (i // tps, i % tps, layer, 0, 0)))
        out_shape.append(jax.ShapeDtypeStruct((bsz, t // page, depth, page * n_heads, HEAD_DIM), F32))
        if prev is not None:
            has_prev = True
            in_specs.append(pl.BlockSpec(memory_space=pl.ANY))
            args.append(prev)
            aliases = {2: 1}
    if side_cast is not None:
        _, side_r, side_c = side_cast.shape
        rows = side_r // (n_j * n_i)
        assert rows * n_j * n_i == side_r and rows % 16 == 0
        in_specs.append(pl.BlockSpec((None, rows, side_c), lambda j, i, *_: (layer, j * n_i + i, 0)))
        args.append(side_cast)
        out_specs.append(pl.BlockSpec((rows, side_c), lambda j, i, *_: (j * n_i + i, 0)))
        out_shape.append(jax.ShapeDtypeStruct((side_r, side_c), BF16))
    n_prefetch = 0
    if dec is not None:
        assert dec.n_items == n_j * n_i
        dec_specs, dec_args, dec_out_spec, dec_out_shape, dec_scratch = dec.operands(lambda j, i: j * n_i + i)
        in_specs += dec_specs
        args = [dec.page_table] + args + dec_args
        out_specs.append(dec_out_spec)
        out_shape.append(dec_out_shape)
        scratch += dec_scratch
        n_prefetch = 1
    grid_spec = pltpu.PrefetchScalarGridSpec(
        num_scalar_prefetch=n_prefetch, grid=(n_j, n_i), in_specs=in_specs, out_specs=out_specs,
        scratch_shapes=scratch)
    out = pl.pallas_call(
        functools.partial(_ws_kernel, mode=mode, transposed=transposed, with_f=f_row is not None,
                          has_prev=has_prev, kv_slot=layer, side_cast=side_cast is not None,
                          dec=None if dec is None else dec.static, n_heads=n_heads, page=page,
                          q_scale=HEAD_DIM ** -0.5 * LOG2E),
        grid_spec=grid_spec,
        out_shape=out_shape,
        input_output_aliases=aliases,
        compiler_params=_params(("arbitrary", "arbitrary"), 58),
        name=name,
    )(*args)
    return out if len(out) > 1 else out[0]


def _wo_kernel(a_ref, s_ref, c_ref, w_ref, x_ref, g1_ref, g2_ref, x1_ref, xn_ref, *, d_a, d_b):
    n_split = 2 if a_ref.shape[0] % 32 == 0 else 1
    rows_per = a_ref.shape[0] // n_split
    for part in range(n_split):
        rows = slice(part * rows_per, (part + 1) * rows_per)
        mixed = _dot(a_ref[rows, :], w_ref[0:d_a, :])
        mixed += _dot(s_ref[rows, :], w_ref[d_a:d_a + d_b, :])
        mixed += _dot(c_ref[rows, :], w_ref[d_a + d_b:, :])
        x1 = x_ref[rows, :] + _rms(mixed, g1_ref[...])
        x1_ref[rows, :] = x1
        xn_ref[rows, :] = _rms(x1, g2_ref[...]).astype(xn_ref.dtype)


def _wo_block(attn, sgu, conv, w_o, x, g_post, g_pre_mlp, tm):
    m, d = x.shape
    d_a, d_b, d_c = attn.shape[1], sgu.shape[1], conv.shape[1]
    row = lambda width: pl.BlockSpec((tm, width), lambda i: (i, 0))
    const = lambda shape: pl.BlockSpec(shape, lambda i: (0, 0))
    w_spec = const((d, d))
    return pl.pallas_call(
        functools.partial(_wo_kernel, d_a=d_a, d_b=d_b),
        grid=(m // tm,),
        in_specs=[row(d_a), row(d_b), row(d_c), w_spec, row(d), const((1, d)), const((1, d))],
        out_specs=[row(d), row(d)],
        out_shape=[jax.ShapeDtypeStruct((m, d), F32), jax.ShapeDtypeStruct((m, d), BF16)],
        compiler_params=_params(("arbitrary",), 48),
        name="wo_norm_residual",
    )(attn, sgu, conv, w_o, x, g_post, g_pre_mlp)


def _down_kernel(*refs, emit_next, dec):
    refs = list(refs)
    if dec is not None:
        refs = refs[1:]
    h_ref, w_ref, x1_ref, g_ref = refs[:4]
    pos = 4
    if emit_next:
        gn_ref = refs[pos]
        pos += 1
    if dec is not None:
        dec_in = refs[pos:pos + 2 + 3 * dec.n_grp]
        pos += 2 + 3 * dec.n_grp
    x2_ref = refs[pos]
    pos += 1
    if emit_next:
        xn_ref = refs[pos]
        pos += 1
    if dec is not None:
        dec_out = refs[pos]
        pos += 1
    acc_ref = refs[pos]
    dec_state = refs[pos + 1:]
    k = pl.program_id(1)

    @pl.when(k == 0)
    def _():
        acc_ref[...] = jnp.zeros_like(acc_ref)

    if dec is not None:
        dec_p = lax.rem(pl.program_id(0) * pl.num_programs(1) + k, dec.n_steps)
        run_decode = functools.partial(_decode_step, p=dec_p, dec=dec, in_refs=dec_in, o_ref=dec_out,
                                       state=dec_state)
        run_decode("init")
        dec_carry = run_decode("scores")
    acc_ref[...] += _dot(h_ref[...], w_ref[...])
    if dec is not None:
        run_decode("values", carry=dec_carry)

    @pl.when(k == pl.num_programs(1) - 1)
    def _():
        x2 = x1_ref[...] + _rms(acc_ref[...], g_ref[...])
        x2_ref[...] = x2
        if emit_next:
            xn_ref[...] = _rms(x2, gn_ref[...]).astype(xn_ref.dtype)

    if dec is not None:
        run_decode("finish")


def _down_block(hid, w_down, x1, g_post, g_next, tm, tk, dec=None):
    m, d = x1.shape
    kdim = hid.shape[1]
    n_i, n_k = m // tm, kdim // tk
    emit_next = g_next is not None
    row = pl.BlockSpec((tm, d), lambda i, k, *_: (i, 0))
    const = pl.BlockSpec((1, d), lambda i, k, *_: (0, 0))
    x1_spec = pl.BlockSpec((tm, d), lambda i, k, *_: (i, 0), pipeline_mode=pl.Buffered(1))
    in_specs = [pl.BlockSpec((tm, tk), lambda i, k, *_: (i, k)),
                pl.BlockSpec((tk, d), lambda i, k, *_: (k, 0)), x1_spec, const]
    args = [hid, w_down, x1, g_post]
    out_specs = [row]
    out_shape = [jax.ShapeDtypeStruct((m, d), F32)]
    scratch = [pltpu.VMEM((tm, d), F32)]
    if emit_next:
        in_specs.append(const)
        args.append(g_next)
        out_specs.append(row)
        out_shape.append(jax.ShapeDtypeStruct((m, d), BF16))
    n_prefetch = 0
    if dec is not None:
        assert dec.n_items == n_i * n_k
        dec_specs, dec_args, dec_out_spec, dec_out_shape, dec_scratch = dec.operands(lambda i, k: i * n_k + k)
        in_specs += dec_specs
        args = [dec.page_table] + args + dec_args
        out_specs.append(dec_out_spec)
        out_shape.append(dec_out_shape)
        scratch += dec_scratch
        n_prefetch = 1
    grid_spec = pltpu.PrefetchScalarGridSpec(
        num_scalar_prefetch=n_prefetch, grid=(n_i, n_k), in_specs=in_specs, out_specs=out_specs,
        scratch_shapes=scratch)
    out = pl.pallas_call(
        functools.partial(_down_kernel, emit_next=emit_next, dec=None if dec is None else dec.static),
        grid_spec=grid_spec,
        out_shape=out_shape,
        compiler_params=_params(("arbitrary", "arbitrary"), 60),
        name="down_norm_residual",
    )(*args)
    return out[0], (out[1] if emit_next else None), (out[-1] if dec is not None else None)


def _prefix_kernel(zf_ref, bf_ref, logf_ref, cq_ref, ct_ref, c_sc, *, t, n_heads):
    lf = _log_sigmoid(zf_ref[0] + bf_ref[...])
    logf_ref[0] = lf[:, 0:n_heads]
    r_i = lax.broadcasted_iota(jnp.int32, (CHUNK, CHUNK), 0)
    c_i = lax.broadcasted_iota(jnp.int32, (CHUNK, CHUNK), 1)
    tri = _ones_where(r_i >= c_i)
    carry = jnp.zeros((1, 128), F32)
    for blk in range(t // CHUNK):
        cb = _dot3_left(tri, lf[blk * CHUNK:(blk + 1) * CHUNK, :]) + carry
        c_sc[blk * CHUNK:(blk + 1) * CHUNK, :] = cb * LOG2E
        carry = cb[CHUNK - 1:CHUNK, :]
    c = c_sc[...]
    ct_ref[0] = c.T[0:n_heads, :]
    for h in range(n_heads):
        cq_ref[0, h] = c[:, h:h + 1]


def _fox_prefix(z3, bf_pad, n_heads, f_blk):
    b, t, _ = z3.shape
    return pl.pallas_call(
        functools.partial(_prefix_kernel, t=t, n_heads=n_heads),
        grid=(b,),
        in_specs=[pl.BlockSpec((1, t, 128), lambda i: (i, 0, f_blk)), pl.BlockSpec((1, 128), lambda i: (0, 0))],
        out_specs=[
            pl.BlockSpec((1, t, n_heads), lambda i: (i, 0, 0)),
            pl.BlockSpec((1, n_heads, t, 1), lambda i: (i, 0, 0, 0)),
            pl.BlockSpec((1, n_heads, t), lambda i: (i, 0, 0)),
        ],
        out_shape=[
            jax.ShapeDtypeStruct((b, t, n_heads), F32),
            jax.ShapeDtypeStruct((b, n_heads, t, 1), F32),
            jax.ShapeDtypeStruct((b, n_heads, t), F32),
        ],
        scratch_shapes=[pltpu.VMEM((t, 128), F32)],
        compiler_params=_params(("arbitrary",), 48),
        name="fox_prefix",
    )(z3, bf_pad)


def _fox_attn_kernel(qi_ref, ki_ref, q_ref, k_ref, v_ref, cq_ref, ck_ref, o_ref, m_sc, l_sc, acc_sc, cq_sc,
                     *, tq, n_heads):
    step = pl.program_id(1)
    qi = qi_ref[step]
    ki = ki_ref[step]
    n_sub = tq // 128

    @pl.when(ki == 0)
    def _():
        m_sc[...] = jnp.full_like(m_sc, -jnp.inf)
        l_sc[...] = jnp.zeros_like(l_sc)
        acc_sc[...] = jnp.zeros_like(acc_sc)
        for h in range(n_heads):
            cq_sc[h] = jnp.broadcast_to(cq_ref[0, h], (tq, 128))

    def block(diagonal):
        if diagonal:
            keep = (lax.broadcasted_iota(jnp.int32, (tq, tq), 0) >= lax.broadcasted_iota(jnp.int32, (tq, tq), 1))
        head_cols = lambda h: slice(h * HEAD_DIM, (h + 1) * HEAD_DIM)
        qk = lambda h: _dot_nt(q_ref[0, :, head_cols(h)], k_ref[0, :, head_cols(h)])
        qk_next = qk(0)
        for h in range(n_heads):
            hs = head_cols(h)
            s = qk_next - ck_ref[0, h:h + 1, :]
            if h + 1 < n_heads:
                qk_next = qk(h + 1)
            if diagonal:
                s = jnp.where(keep, s, -jnp.inf)
            subs = [s[:, j * 128:(j + 1) * 128] for j in range(n_sub)]
            mc = subs[0]
            for x in subs[1:]:
                mc = jnp.maximum(mc, x)
            cq = cq_sc[h]
            m_prev = m_sc[h]
            m_new = jnp.maximum(m_prev, jnp.max(mc, axis=1, keepdims=True) + cq)
            alpha = jnp.exp2(m_prev - m_new)
            shift = m_new - cq
            ps = [jnp.exp2(x - shift) for x in subs]
            lsum = ps[0]
            for x in ps[1:]:
                lsum = lsum + x
            l_sc[h] = alpha * l_sc[h] + jnp.sum(lsum, axis=1, keepdims=True)
            p = jnp.concatenate(ps, axis=1).astype(BF16)
            acc_sc[:, hs] = alpha * acc_sc[:, hs] + _dot(p, v_ref[0, :, hs])
            m_sc[h] = m_new

    @pl.when(ki < qi)
    def _():
        block(False)

    @pl.when(ki == qi)
    def _():
        block(True)
        for h in range(n_heads):
            hs = slice(h * HEAD_DIM, (h + 1) * HEAD_DIM)
            o_ref[0, :, hs] = (acc_sc[:, hs] / l_sc[h]).astype(o_ref.dtype)


def _fox_attn_prompt(q, k, v, cq, ct, n_heads, tq):
    b, t, d_a = q.shape
    nq = t // tq
    pairs = [(i, j) for i in range(nq) for j in range(i + 1)]
    qi_tab = jnp.asarray([pr[0] for pr in pairs], jnp.int32)
    ki_tab = jnp.asarray([pr[1] for pr in pairs], jnp.int32)
    q_blk = pl.BlockSpec((1, tq, d_a), lambda bi, s, qt, kt: (bi, qt[s], 0))
    kv_blk = pl.BlockSpec((1, tq, d_a), lambda bi, s, qt, kt: (bi, kt[s], 0))
    grid_spec = pltpu.PrefetchScalarGridSpec(
        num_scalar_prefetch=2,
        grid=(b, len(pairs)),
        in_specs=[
            q_blk, kv_blk, kv_blk,
            pl.BlockSpec((1, n_heads, tq, 1), lambda bi, s, qt, kt: (bi, 0, qt[s], 0)),
            pl.BlockSpec((1, n_heads, tq), lambda bi, s, qt, kt: (bi, 0, kt[s])),
        ],
        out_specs=q_blk,
        scratch_shapes=[
            pltpu.VMEM((n_heads, tq, 128), F32),
            pltpu.VMEM((n_heads, tq, 128), F32),
            pltpu.VMEM((tq, d_a), F32),
            pltpu.VMEM((n_heads, tq, 128), F32),
        ],
    )
    return pl.pallas_call(
        functools.partial(_fox_attn_kernel, tq=tq, n_heads=n_heads),
        grid_spec=grid_spec,
        out_shape=jax.ShapeDtypeStruct((b, t, d_a), BF16),
        compiler_params=_params(("arbitrary", "arbitrary"), 48),
        name="fox_attn_prompt",
    )(qi_tab, ki_tab, q, k, v, cq, ct)


def _layernorm_silu(y, g, b):
    mu = jnp.mean(y, axis=-1, keepdims=True)
    yc = y - mu
    yn = yc * lax.rsqrt(jnp.mean(yc * yc, axis=-1, keepdims=True) + EPS) * g + b
    return yn * _sigmoid(yn)


def _mixer_kernel(ub_ref, vb_ref, ac_ref, gc_ref, gv_ref, ws_ref, bst_ref, cw_ref, cb_ref, lg_ref, lb_ref,
                  sgu_ref, conv_ref, tail_ref, g_sc, *, tm, n_heads_b, conv_w, halo):
    ti = pl.program_id(1)

    @pl.when(ti == 0)
    def _():
        g_sc[0:halo, :] = jnp.zeros((halo, g_sc.shape[1]), F32)

    u = _gelu(ub_ref[0])
    vn = _rms(_gelu(vb_ref[0]), gv_ref[...])
    r_i = lax.broadcasted_iota(jnp.int32, (CHUNK, CHUNK), 0)
    c_i = lax.broadcasted_iota(jnp.int32, (CHUNK, CHUNK), 1)
    for h in range(n_heads_b):
        hs = slice(h * HEAD_DIM, (h + 1) * HEAD_DIM)
        w_h = jnp.where(r_i >= c_i, ws_ref[h], 0.0).astype(BF16)
        bias_h = bst_ref[:, h:h + 1]
        for c in range(tm // CHUNK):
            rs = slice(c * CHUNK, (c + 1) * CHUNK)
            mix = _dot(w_h, vn[rs, hs].astype(BF16)) + bias_h
            sgu_ref[0, rs, hs] = (u[rs, hs] * mix).astype(sgu_ref.dtype)

    g_sc[halo:halo + tm, :] = ac_ref[0] * _sigmoid(gc_ref[0])
    base = halo - (conv_w - 1)
    n_rows = halo + tm
    window = g_sc[...]
    rotated = {0: window}
    y = cb_ref[...]
    for k in range(conv_w):
        a, r = divmod(base + k, 8)
        if r not in rotated:
            rotated[r] = pltpu.roll(window, n_rows - r, 0)
        y = y + rotated[r][8 * a:8 * a + tm, :] * cw_ref[k:k + 1, :]
    conv_ref[0] = _layernorm_silu(y, lg_ref[...], lb_ref[...]).astype(conv_ref.dtype)

    @pl.when(ti == pl.num_programs(1) - 1)
    def _():
        tail_ref[0] = g_sc[halo + tm - (conv_w - 1):halo + tm, :]

    g_sc[0:halo, :] = g_sc[tm:tm + halo, :]


def _mixer_prompt(z3, g_v, w_s, b_s_t, conv_w, conv_b, ln_g, ln_b, tm):
    b, t, _ = z3.shape
    n_heads_b = w_s.shape[0]
    d_b = n_heads_b * HEAD_DIM
    d_c = conv_w.shape[1]
    kw = conv_w.shape[0]
    halo = 32
    zcol = lambda c: pl.BlockSpec((1, tm, d_b), lambda bi, ti: (bi, ti, c))
    const = lambda shape: pl.BlockSpec(shape, lambda bi, ti: (0,) * len(shape))
    return pl.pallas_call(
        functools.partial(_mixer_kernel, tm=tm, n_heads_b=n_heads_b, conv_w=kw, halo=halo),
        grid=(b, t // tm),
        in_specs=[zcol(0), zcol(1), zcol(2), zcol(3),
                  const((1, d_b)), const((n_heads_b, CHUNK, CHUNK)), const((CHUNK, n_heads_b)),
                  const((kw, d_c)), const((1, d_c)), const((1, d_c)), const((1, d_c))],
        out_specs=[
            pl.BlockSpec((1, tm, d_b), lambda bi, ti: (bi, ti, 0)),
            pl.BlockSpec((1, tm, d_c), lambda bi, ti: (bi, ti, 0)),
            pl.BlockSpec((1, kw - 1, d_c), lambda bi, ti: (bi, 0, 0)),
        ],
        out_shape=[
            jax.ShapeDtypeStruct((b, t, d_b), BF16),
            jax.ShapeDtypeStruct((b, t, d_c), BF16),
            jax.ShapeDtypeStruct((b, kw - 1, d_c), F32),
        ],
        scratch_shapes=[pltpu.VMEM((halo + tm, d_c), F32)],
        compiler_params=_params(("arbitrary", "arbitrary"), 32),
        name="mixer_prompt",
    )(z3, z3, z3, z3, g_v, w_s, b_s_t, conv_w, conv_b, ln_g, ln_b)


def _mixer_sample_kernel(zs_ref, zf_ref, bf_ref, gv_ref, w0_ref, b0_ref, st_ref, cw_ref, cb_ref, lg_ref, lb_ref,
                         sgu_ref, conv_ref, state_ref, vn_ref, logf_ref, y_sc, *, d_b, conv_w):
    n = zs_ref.shape[0]
    u = _gelu(zs_ref[:, 0:d_b])
    vn = _rms(_gelu(zs_ref[:, d_b:2 * d_b]), gv_ref[...])
    vn_ref[...] = vn
    sgu_ref[...] = (u * (w0_ref[...] * vn + b0_ref[...])).astype(sgu_ref.dtype)
    glu = zs_ref[:, 2 * d_b:3 * d_b] * _sigmoid(zs_ref[:, 3 * d_b:4 * d_b])
    kw = conv_w - 1
    for bi in range(n):
        g_new = glu[bi:bi + 1, :]
        y_sc[bi:bi + 1, :] = (jnp.sum(st_ref[bi] * cw_ref[0:kw, :], axis=0, keepdims=True)
                              + g_new * cw_ref[kw:kw + 1, :])
        state_ref[bi, 0:kw - 1, :] = st_ref[bi, 1:kw, :]
        state_ref[bi, kw - 1:kw, :] = g_new
    conv_ref[...] = _layernorm_silu(y_sc[...] + cb_ref[...], lg_ref[...], lb_ref[...]).astype(conv_ref.dtype)
    logf_ref[...] = _log_sigmoid(zf_ref[...] + bf_ref[...])


def _mixer_sample(zs, zf, bf_pad, g_v, w0_row, b0_row, state, conv_w, conv_b, ln_g, ln_b):
    n = zs.shape[0]
    d_b = g_v.shape[1]
    kw, d_c = conv_w.shape
    full = lambda shape: pl.BlockSpec(shape, lambda i: (0,) * len(shape))
    args = (zs, zf, bf_pad, g_v, w0_row, b0_row, state, conv_w, conv_b, ln_g, ln_b)
    out_shape = [
        jax.ShapeDtypeStruct((n, d_b), BF16),
        jax.ShapeDtypeStruct((n, d_c), BF16),
        jax.ShapeDtypeStruct((n, kw - 1, d_c), F32),
        jax.ShapeDtypeStruct((n, d_b), F32),
        jax.ShapeDtypeStruct((n, 128), F32),
    ]
    return pl.pallas_call(
        functools.partial(_mixer_sample_kernel, d_b=d_b, conv_w=kw),
        grid=(1,),
        in_specs=[full(a.shape) for a in args],
        out_specs=[full(s.shape) for s in out_shape],
        out_shape=out_shape,
        scratch_shapes=[pltpu.VMEM((n, d_c), F32)],
        compiler_params=_params(("arbitrary",), 32),
        name="mixer_sample",
    )(*args)


def _decode_step(phase, p, dec, in_refs, o_ref, state, carry=None):
    n_grp, n_heads, scale = dec.n_grp, dec.n_heads, dec.scale
    zs_ref, cn_ref = in_refs[0], in_refs[1]
    refs = in_refs[2:]
    k_refs, v_refs, lf_refs = refs[:n_grp], refs[n_grp:2 * n_grp], refs[2 * n_grp:3 * n_grp]
    cn_sc, m_sc, l_sc, acc_sc, tail_sc = state
    rows_pp = k_refs[0].shape[2]
    n_flat = rows_pp // 128
    lane = lax.broadcasted_iota(jnp.int32, (n_heads, 128), 1)
    sub = lax.broadcasted_iota(jnp.int32, (n_heads, 128), 0)
    cls_mask = n_heads - 1
    own = sub == (lane & cls_mask)
    diag = sub == lane
    r_i = lax.broadcasted_iota(jnp.int32, (128, 128), 0)
    c_i = lax.broadcasted_iota(jnp.int32, (128, 128), 1)
    same = (r_i & cls_mask) == (c_i & cls_mask)

    if phase == "init":
        @pl.when(p == 0)
        def _():
            spread = _ones_where((r_i < n_heads) & ((c_i & cls_mask) == r_i))
            cn_sc[...] = _dot3(jnp.broadcast_to(cn_ref[0], (n_heads, 128)), spread)
            m_sc[...] = jnp.full_like(m_sc, -jnp.inf)
            l_sc[...] = jnp.zeros_like(l_sc)
            acc_sc[...] = jnp.zeros_like(acc_sc)
            tail_sc[...] = jnp.zeros_like(tail_sc)
        return

    if phase == "finish":
        @pl.when(p == dec.n_steps - 1)
        def _():
            m_col = jnp.max(jnp.where(diag, m_sc[...], -jnp.inf), axis=1, keepdims=True)
            l_row = jnp.sum(l_sc[...], axis=0, keepdims=True)
            l_col = jnp.sum(jnp.where(own, jnp.broadcast_to(l_row, (n_heads, 128)), 0.0), axis=1, keepdims=True)
            cn_col = jnp.sum(jnp.where(diag, cn_sc[...], 0.0), axis=1, keepdims=True)
            q_new = zs_ref[0, 0:n_heads, :]
            k_new = zs_ref[0, n_heads:2 * n_heads, :]
            v_new = zs_ref[0, 2 * n_heads:3 * n_heads, :]
            s_new = jnp.sum(q_new * k_new, axis=1, keepdims=True) * scale + cn_col - cn_col
            m_f = jnp.maximum(m_col, s_new)
            a_f = jnp.exp(m_col - m_f)
            p_new = jnp.exp(s_new - m_f)
            o_ref[0] = (a_f * acc_sc[...] + p_new * v_new) / (a_f * l_col + p_new)
        return

    if phase == "values":
        return _decode_values(dec, carry, v_refs, state, own, diag, n_flat)

    lf = jnp.concatenate([lf_refs[g][0, 0] for g in range(n_grp)], axis=0)
    n_rows = n_grp * n_flat
    row_tot = _dot3(lf, _ones_where(same))
    within = _dot3(lf, _ones_where(same & (r_i > c_i)))
    rr = lax.broadcasted_iota(jnp.int32, (n_rows, n_rows), 0)
    cc = lax.broadcasted_iota(jnp.int32, (n_rows, n_rows), 1)
    suffix = within + _dot3_left(_ones_where(cc > rr), row_tot) + tail_sc[0:1, :]

    q8 = zs_ref[0, 0:n_heads, :].astype(BF16)
    scores = []
    for g in range(n_grp):
        s_t = _dot_nt(q8, k_refs[g][0, 0].astype(BF16))
        flat = [jnp.sum(jnp.where(own, s_t[:, a * 128:(a + 1) * 128], 0.0), axis=0, keepdims=True)
                for a in range(n_flat)]
        scores.append(jnp.concatenate(flat, axis=0) * scale
                      + suffix[g * n_flat:(g + 1) * n_flat, :] + cn_sc[...])
    return scores, row_tot


def _decode_values(dec, carry, v_refs, state, own, diag, n_flat):
    n_grp, n_heads = dec.n_grp, dec.n_heads
    cn_sc, m_sc, l_sc, acc_sc, tail_sc = state
    scores, row_tot = carry
    m_step = scores[0]
    for x in scores[1:]:
        m_step = jnp.maximum(m_step, x)
    shift = 1
    while shift < n_flat:
        m_step = jnp.maximum(m_step, pltpu.roll(m_step, shift, 0))
        shift *= 2
    shift = n_heads
    while shift < 128:
        m_step = jnp.maximum(m_step, pltpu.roll(m_step, shift, 1))
        shift *= 2
    m_prev = m_sc[...]
    m_new = jnp.maximum(m_prev, m_step)
    alpha = jnp.exp(m_prev - m_new)
    probs = [jnp.exp(x - m_new) for x in scores]
    l_step = probs[0]
    for x in probs[1:]:
        l_step = l_step + x
    l_sc[...] = alpha * l_sc[...] + l_step
    pv = jnp.zeros((n_heads, HEAD_DIM), F32)
    for g in range(n_grp):
        blocks = [jnp.where(own, jnp.broadcast_to(probs[g][a:a + 1, :], (n_heads, 128)), 0.0)
                  for a in range(n_flat)]
        pv += _dot(jnp.concatenate(blocks, axis=1).astype(BF16), v_refs[g][0, 0].astype(BF16))
    alpha_col = jnp.sum(jnp.where(diag, alpha, 0.0), axis=1, keepdims=True)
    acc_sc[...] = alpha_col * acc_sc[...] + pv
    m_sc[...] = m_new
    tail_sc[...] += jnp.sum(row_tot, axis=0, keepdims=True)


class _DecodeStatic(NamedTuple):
    n_grp: int
    n_steps: int
    n_items: int
    n_heads: int
    scale: float


class _DecodePlan:
    def __init__(self, page_table, zs3, cn, cache_k, cache_v, cache_lf, layer, n_heads, b0, nb, n_grp):
        n_pages = page_table.shape[1]
        assert n_pages % n_grp == 0 and cache_k.shape[2] == 128 * n_heads and cache_lf.shape[2] == n_heads
        assert n_heads & (n_heads - 1) == 0 and n_heads <= 8
        self.page_table, self.layer, self.b0, self.nb, self.n_pages = page_table, layer, b0, nb, n_pages
        self.arrays = (zs3, cn, cache_k, cache_v, cache_lf)
        n_steps = n_pages // n_grp
        self.static = _DecodeStatic(n_grp, n_steps, nb * n_steps, n_heads, HEAD_DIM ** -0.5)
        self.n_items = nb * n_steps

    def operands(self, step_of):
        zs3, cn, cache_k, cache_v, cache_lf = self.arrays
        st, b0, layer, n_pages = self.static, self.b0, self.layer, self.n_pages
        n_heads, n_grp, n_steps = st.n_heads, st.n_grp, st.n_steps
        rows_pp = cache_k.shape[2]
        seq = lambda g0, g1: step_of(g0, g1) // n_steps

        def page_spec(block, g):
            def index(g0, g1, pt):
                s = step_of(g0, g1)
                return (pt[b0 + s // n_steps, n_pages - (s % n_steps + 1) * n_grp + g], layer, 0, 0)
            return pl.BlockSpec(block, index)

        kv_specs = [page_spec((1, 1, rows_pp, HEAD_DIM), g) for g in range(n_grp)]
        lf_specs = [page_spec((1, 1, n_heads, 128), g) for g in range(n_grp)]
        in_specs = [pl.BlockSpec((1, 3 * n_heads, HEAD_DIM), lambda g0, g1, pt: (b0 + seq(g0, g1), 0, 0)),
                    pl.BlockSpec((1, 1, 128), lambda g0, g1, pt: (b0 + seq(g0, g1), 0, 0))]
        in_specs += kv_specs + kv_specs + lf_specs
        args = [zs3, cn] + [cache_k] * n_grp + [cache_v] * n_grp + [cache_lf] * n_grp
        out_spec = pl.BlockSpec((1, n_heads, HEAD_DIM), lambda g0, g1, pt: (seq(g0, g1), 0, 0))
        out_shape = jax.ShapeDtypeStruct((self.nb, n_heads, HEAD_DIM), F32)
        scratch = [pltpu.VMEM((n_heads, 128), F32)] * 3 + [pltpu.VMEM((n_heads, HEAD_DIM), F32),
                                                           pltpu.VMEM((n_heads, 128), F32)]
        return in_specs, args, out_spec, out_shape, scratch


def kernel(x_prompt, x_sample, cache_k, cache_v, cache_logf, state_conv, page_table, w_in, b_f, g_v, w_s, b_s,
           conv_w, conv_b, ln_c_g, ln_c_b, w_o, g_pre_mix, g_post_mix, g_pre_mlp, g_post_mlp, w_up, w_down):
    bp, tp, d = x_prompt.shape
    bs, ds, _ = x_sample.shape
    assert ds == 1, "the sample path handles exactly one new token per sequence"
    depth = w_in.shape[0]
    n_pool, _, page, n_heads, _ = cache_k.shape
    d_a = n_heads * HEAD_DIM
    n_heads_b = w_s.shape[1]
    d_b = n_heads_b * HEAD_DIM
    d_c = conv_w.shape[2]
    off_f = 3 * d_a
    assert w_in.shape[2] == off_f + n_heads + 2 * d_b + 2 * d_c and d_b == d_c and page == CHUNK
    mp = bp * tp

    n_rest = 2 * d_b + 2 * d_c
    w_in_t = jnp.swapaxes(w_in, 1, 2)
    d_ff = w_up.shape[2]
    bf_pad = jnp.pad(b_f, ((0, 0), (0, 128 - n_heads)))
    row2 = lambda a, l: a[l][None, :]
    proj = functools.partial(_ws_matmul, transposed=True)

    n_pages = page_table.shape[1]
    assert bs % 2 == 0
    grp_up = (bs // 2) * n_pages // ((d_ff // UP_TILE[1]) * (mp // UP_TILE[0]))
    grp_down = (bs // 2) * n_pages // ((mp // DOWN_TILE[0]) * (d_ff // DOWN_TILE[1]))

    cache_k2 = cache_k.reshape(n_pool, depth, page * n_heads, HEAD_DIM)
    cache_v2 = cache_v.reshape(n_pool, depth, page * n_heads, HEAD_DIM)
    cache_lf = cache_logf.reshape(n_pool, depth, page * n_heads // 128, 128)

    xp = x_prompt.reshape(mp, d)
    xs = x_sample.reshape(bs, d)
    xpn = _norm_bf16(xp, row2(g_pre_mix, 0), 512)
    xsn = _norm_bf16(xs, row2(g_pre_mix, 0), bs)
    fp_l, cp_l, ks_l, vs_l, fs_l, cs_l, us_l = ([] for _ in range(7))
    k_pages = v_pages = None
    for l in range(depth):
        g_next = row2(g_pre_mix, l + 1) if l + 1 < depth else None
        q, zf = proj(xpn, w_in_t, l, 0, d_a, 1024, d_a, "q", "proj_q", f_row=off_f)
        kb, k_pages = proj(xpn, w_in_t, l, d_a, d_a, 1024, d_a, "kv", "proj_k",
                           pages=(k_pages, bp, tp, depth, n_heads, page))
        vb, v_pages = proj(xpn, w_in_t, l, 2 * d_a, d_a, 1024, d_a, "kv", "proj_v",
                           pages=(v_pages, bp, tp, depth, n_heads, page))
        z_rest, w_o_l = proj(xpn, w_in_t, l, off_f + n_heads, n_rest, 1024, 1024, "f32", "proj_rest", side_cast=w_o)
        z3 = z_rest.reshape(bp, tp, n_rest)
        logf, cq, ct = _fox_prefix(zf.reshape(bp, tp, 128), bf_pad[l][None, :], n_heads, 0)
        attn = _fox_attn_prompt(q.reshape(bp, tp, d_a), kb.reshape(bp, tp, d_a), vb.reshape(bp, tp, d_a),
                                cq, ct, n_heads, 512)
        fp_l.append(logf)
        zs_q, zs_f = proj(xsn, w_in_t, l, 0, d_a, bs, d_a, "f32", "proj_q_sample", f_row=off_f)
        zs_kv = proj(xsn, w_in_t, l, d_a, 2 * d_a, bs, d_a, "f32", "proj_kv_sample")
        zs_rest = proj(xsn, w_in_t, l, off_f + n_heads, n_rest, bs, 1024, "f32", "proj_rest_sample")
        w0_row = jnp.repeat(w_s[l][:, 0, 0], HEAD_DIM)[None, :]
        b0_row = jnp.repeat(b_s[l][:, 0], HEAD_DIM)[None, :]
        sgu_s, conv_s, state_s, vn_s, logf_s = _mixer_sample(
            zs_rest, zs_f, bf_pad[l][None, :], row2(g_v, l), w0_row, b0_row, state_conv[l], conv_w[l],
            row2(conv_b, l), row2(ln_c_g, l), row2(ln_c_b, l))
        zs3 = jnp.concatenate([zs_q, zs_kv], axis=1).reshape(bs, 3 * n_heads, HEAD_DIM)
        plan = lambda b0, n_grp: _DecodePlan(page_table, zs3, logf_s[:, None, :], cache_k2, cache_v2, cache_lf,
                                             l, n_heads, b0, bs // 2, n_grp)
        sgu, conv, conv_tail = _mixer_prompt(z3, row2(g_v, l), w_s[l], b_s[l].T, conv_w[l], row2(conv_b, l),
                                             row2(ln_c_g, l), row2(ln_c_b, l), MIXER_TM)
        cp_l.append(conv_tail)
        x1, x1n = _wo_block(attn.reshape(mp, d_a), sgu.reshape(mp, d_b), conv.reshape(mp, d_c), w_o_l, xp,
                            row2(g_post_mix, l), row2(g_pre_mlp, l), 512)
        hid, w_down_l, attn_s0 = _ws_matmul(x1n, w_up, l, 0, d_ff, UP_TILE[0], UP_TILE[1], "relu2", "mlp_up",
                                            side_cast=w_down, dec=plan(0, grp_up))
        xp, xpn, attn_s1 = _down_block(hid, w_down_l, x1, row2(g_post_mlp, l), g_next, DOWN_TILE[0],
                                       DOWN_TILE[1], dec=plan(bs // 2, grp_down))
        attn_s = jnp.concatenate([attn_s0, attn_s1], axis=0).reshape(bs, d_a).astype(BF16)
        x1s, x1sn = _wo_block(attn_s, sgu_s, conv_s, w_o_l, xs, row2(g_post_mix, l), row2(g_pre_mlp, l), bs)
        hid_s = _ws_matmul(x1sn, w_up, l, 0, d_ff, bs, 1024, "relu2", "mlp_up_sample")
        xs, xsn, _ = _down_block(hid_s, w_down_l, x1s, row2(g_post_mlp, l), g_next, bs, 2048)
        ks_l.append(zs_kv[:, :d_a])
        vs_l.append(zs_kv[:, d_a:])
        fs_l.append(logf_s[:, :n_heads])
        cs_l.append(state_s)
        us_l.append(vn_s)

    npp = tp // page
    kv_out = lambda pages: pages.reshape(bp, npp, depth, page, n_heads, HEAD_DIM)
    logf_prompt = jnp.stack(fp_l, axis=1).reshape(bp, depth, npp, page, n_heads).swapaxes(1, 2)
    return (xp.reshape(bp, tp, d), xs.reshape(bs, ds, d),
            kv_out(k_pages), kv_out(v_pages), logf_prompt,
            jnp.stack(cp_l, axis=0),
            jnp.stack(ks_l, axis=1).reshape(bs, depth, ds, n_heads, HEAD_DIM),
            jnp.stack(vs_l, axis=1).reshape(bs, depth, ds, n_heads, HEAD_DIM),
            jnp.stack(fs_l, axis=1).reshape(bs, depth, ds, n_heads),
            jnp.stack(cs_l, axis=0),
            jnp.stack(us_l, axis=0).reshape(depth, bs, ds, d_b))
```

```python
import functools
from typing import NamedTuple

import numpy as np
import jax
import jax.numpy as jnp
from jax import lax
from jax.experimental import pallas as pl
from jax.experimental.pallas import tpu as pltpu

F32 = jnp.float32
BF16 = jnp.bfloat16

HEAD_DIM = 128
CHUNK = 128
EPS = 1e-6
MIB = 1024 * 1024
GELU_C = float(np.sqrt(2.0 / np.pi))
LOG2E = float(np.log2(np.e))

UP_TILE = (1024, 1024)
DOWN_TILE = (512, 2048)
MIXER_TM = 256


def _params(semantics, vmem_mib):
    return pltpu.CompilerParams(dimension_semantics=semantics, vmem_limit_bytes=vmem_mib * MIB)


def _rms(x, g):
    return x * lax.rsqrt(jnp.mean(x * x, axis=-1, keepdims=True) + EPS) * g


def _gelu(x):
    return x * (0.5 * (1.0 + jnp.tanh(GELU_C * (x + 0.044715 * (x * x * x)))))


def _sigmoid(x):
    return 1.0 / (1.0 + jnp.exp(-x))


def _log_sigmoid(x):
    return jnp.minimum(x, 0.0) - jnp.log1p(jnp.exp(-jnp.abs(x)))


def _split3(x):
    hi = x.astype(BF16)
    r = x - hi.astype(F32)
    mid = r.astype(BF16)
    lo = (r - mid.astype(F32)).astype(BF16)
    return hi, mid, lo


def _dot(a, b):
    return jnp.dot(a, b, preferred_element_type=F32)


def _dot_nt(a, b):
    return lax.dot_general(a, b, (((1,), (1,)), ((), ())), preferred_element_type=F32)


def _dot3(x, w):
    hi, mid, lo = _split3(x)
    return (_dot(hi, w) + _dot(mid, w)) + _dot(lo, w)


def _dot3_left(w, x):
    hi, mid, lo = _split3(x)
    return (_dot(w, hi) + _dot(w, mid)) + _dot(w, lo)


def _ones_where(cond):
    return jnp.where(cond, 1.0, 0.0).astype(BF16)


def _norm_kernel(x_ref, g_ref, o_ref):
    o_ref[...] = _rms(x_ref[...], g_ref[...]).astype(o_ref.dtype)


def _norm_bf16(x, g, tm):
    m, d = x.shape
    return pl.pallas_call(
        _norm_kernel,
        grid=(m // tm,),
        in_specs=[pl.BlockSpec((tm, d), lambda i: (i, 0)), pl.BlockSpec((1, d), lambda i: (0, 0))],
        out_specs=pl.BlockSpec((tm, d), lambda i: (i, 0)),
        out_shape=jax.ShapeDtypeStruct((m, d), BF16),
        compiler_params=_params(("arbitrary",), 32),
        name="pre_norm",
    )(x, g)


def _ws_kernel(*refs, mode, transposed, with_f, has_prev, kv_slot, side_cast, with_xs, dec, n_heads, page,
               q_scale):
    refs = list(refs)
    if dec is not None:
        refs = refs[1:]
    x_ref, w_ref = refs[0], refs[1]
    pos = 2
    if with_f:
        wf_ref = refs[pos]
        pos += 1
    if has_prev:
        pos += 1
    if side_cast:
        side_in = refs[pos]
        pos += 1
    if with_xs:
        xs_ref = refs[pos]
        pos += 1
    if dec is not None:
        dec_in = refs[pos:pos + 2 + 3 * dec.n_grp]
        pos += 2 + 3 * dec.n_grp
    n_out = 2 if (mode == "kv" or with_f) else 1
    outs = refs[pos:pos + n_out]
    pos += n_out
    if side_cast:
        refs[pos][...] = side_in[...].astype(BF16)
        pos += 1
    if with_xs:
        xs_outs = refs[pos:pos + (2 if with_f else 1)]
        pos += len(xs_outs)
    if dec is not None:
        dec_out = refs[pos]
        pos += 1
    w_sc = refs[pos]
    pos += 1
    if with_f:
        wf_sc = refs[pos]
        pos += 1
    dec_state = refs[pos:]

    @pl.when(pl.program_id(1) == 0)
    def _():
        w_sc[...] = (w_ref[0] if transposed else w_ref[...]).astype(BF16)
        if with_f:
            pad = jnp.zeros((wf_sc.shape[0] - wf_ref.shape[1], wf_sc.shape[1]), F32)
            wf_sc[...] = jnp.concatenate([wf_ref[0], pad], axis=0).astype(BF16)
        if with_xs:
            xs = xs_ref[...]
            xs_outs[0][...] = _dot_nt(xs, w_sc[...]) if transposed else _dot(xs, w_sc[...])
            if with_f:
                xs_outs[1][...] = _dot_nt(xs, wf_sc[...])

    if dec is not None:
        dec_p = lax.rem(pl.program_id(0) * pl.num_programs(1) + pl.program_id(1), dec.n_steps)
        run_decode = functools.partial(_decode_step, p=dec_p, dec=dec, in_refs=dec_in, o_ref=dec_out,
                                       state=dec_state)
        run_decode("init")
        dec_carry = run_decode("scores")
    x = x_ref[...]
    acc = _dot_nt(x, w_sc[...]) if transposed else _dot(x, w_sc[...])
    if with_f:
        outs[1][...] = _dot_nt(x, wf_sc[...])
    if mode == "f32":
        outs[0][...] = acc
    elif mode == "q":
        outs[0][...] = (acc * q_scale).astype(BF16)
    elif mode == "relu2":
        h = jnp.maximum(acc, 0.0)
        outs[0][...] = (h * h).astype(BF16)
    else:
        outs[0][...] = acc.astype(BF16)
        dst_ref = outs[1]
        slot = () if has_prev else (kv_slot,)
        for pg in range(acc.shape[0] // page):
            for h in range(n_heads):
                dst_ref[(0, pg) + slot + (pl.ds(h, page, stride=n_heads), slice(None))] = (
                    acc[pg * page:(pg + 1) * page, h * HEAD_DIM:(h + 1) * HEAD_DIM])
        if not has_prev:
            for other in range(dst_ref.shape[2]):
                if other != kv_slot:
                    dst_ref[0, :, other] = jnp.zeros(dst_ref.shape[1:2] + dst_ref.shape[3:], F32)
    if dec is not None:
        run_decode("values", carry=dec_carry)
        run_decode("finish")


def _ws_matmul(x, w, layer, col0, n_cols, tm, tn, mode, name, *, transposed=False, f_row=None, pages=None,
               side_cast=None, xs=None, dec=None):
    m, k = x.shape
    assert n_cols % tn == 0 and (f_row is None or (n_cols == tn and transposed))
    n_j, n_i = n_cols // tn, m // tm
    if transposed:
        w_spec = pl.BlockSpec((pl.Element(1), pl.Element(tn), pl.Element(k)),
                              lambda j, i, *_: (layer, pl.multiple_of(col0 + j * tn, 8), 0))
    else:
        assert col0 % tn == 0
        w_spec = pl.BlockSpec((None, k, tn), lambda j, i, *_: (layer, 0, col0 // tn + j))
    in_specs = [pl.BlockSpec((tm, k), lambda j, i, *_: (i, 0)), w_spec]
    args = [x, w]
    out_specs = [pl.BlockSpec((tm, tn), lambda j, i, *_: (i, j))]
    out_shape = [jax.ShapeDtypeStruct((m, n_cols), F32 if mode == "f32" else BF16)]
    scratch = [pltpu.VMEM((tn, k) if transposed else (k, tn), BF16)]
    if f_row is not None:
        in_specs.append(pl.BlockSpec((pl.Element(1), pl.Element(8), pl.Element(k)),
                                     lambda j, i, *_: (layer, f_row, 0)))
        args.append(w)
        out_specs.append(pl.BlockSpec((tm, 128), lambda j, i, *_: (i, 0)))
        out_shape.append(jax.ShapeDtypeStruct((m, 128), F32))
        scratch.append(pltpu.VMEM((128, k), BF16))
    aliases = {}
    n_heads = page = 0
    has_prev = False
    if mode == "kv":
        prev, bsz, t, depth, n_heads, page = pages
        assert n_cols == tn == n_heads * HEAD_DIM and t % tm == 0 and tm % page == 0
        tps = t // tm
        if prev is None:
            out_specs.append(pl.BlockSpec((1, tm // page, depth, page * n_heads, HEAD_DIM),
                                          lambda j, i, *_: (i // tps, i % tps, 0, 0, 0)))
        else:
            out_specs.append(pl.BlockSpec((1, tm // page, None, page * n_heads, HEAD_DIM),
                                          lambda j, i, *_: (i // tps, i % tps, layer, 0, 0)))
        out_shape.append(jax.ShapeDtypeStruct((bsz, t // page, depth, page * n_heads, HEAD_DIM), F32))
        if prev is not None:
            has_prev = True
            in_specs.append(pl.BlockSpec(memory_space=pl.ANY))
            args.append(prev)
            aliases = {2: 1}
    if side_cast is not None:
        _, side_r, side_c = side_cast.shape
        rows = side_r // (n_j * n_i)
        assert rows * n_j * n_i == side_r and rows % 16 == 0
        in_specs.append(pl.BlockSpec((None, rows, side_c), lambda j, i, *_: (layer, j * n_i + i, 0)))
        args.append(side_cast)
        out_specs.append(pl.BlockSpec((rows, side_c), lambda j, i, *_: (j * n_i + i, 0)))
        out_shape.append(jax.ShapeDtypeStruct((side_r, side_c), BF16))
    if xs is not None:
        ns = xs.shape[0]
        in_specs.append(pl.BlockSpec((ns, k), lambda j, i, *_: (0, 0)))
        args.append(xs)
        out_specs.append(pl.BlockSpec((ns, tn), lambda j, i, *_: (0, j)))
        out_shape.append(jax.ShapeDtypeStruct((ns, n_cols), F32))
        if f_row is not None:
            out_specs.append(pl.BlockSpec((ns, 128), lambda j, i, *_: (0, 0)))
            out_shape.append(jax.ShapeDtypeStruct((ns, 128), F32))
    n_prefetch = 0
    if dec is not None:
        assert dec.n_items == n_j * n_i
        dec_specs, dec_args, dec_out_spec, dec_out_shape, dec_scratch = dec.operands(lambda j, i: j * n_i + i)
        in_specs += dec_specs
        args = [dec.page_table] + args + dec_args
        out_specs.append(dec_out_spec)
        out_shape.append(dec_out_shape)
        scratch += dec_scratch
        n_prefetch = 1
    grid_spec = pltpu.PrefetchScalarGridSpec(
        num_scalar_prefetch=n_prefetch, grid=(n_j, n_i), in_specs=in_specs, out_specs=out_specs,
        scratch_shapes=scratch)
    out = pl.pallas_call(
        functools.partial(_ws_kernel, mode=mode, transposed=transposed, with_f=f_row is not None,
                          has_prev=has_prev, kv_slot=layer, side_cast=side_cast is not None, with_xs=xs is not None,
                          dec=None if dec is None else dec.static, n_heads=n_heads, page=page,
                          q_scale=HEAD_DIM ** -0.5 * LOG2E),
        grid_spec=grid_spec,
        out_shape=out_shape,
        input_output_aliases=aliases,
        compiler_params=_params(("arbitrary", "arbitrary"), 58),
        name=name,
    )(*args)
    return out if len(out) > 1 else out[0]


def _wo_kernel(a_ref, s_ref, c_ref, w_ref, x_ref, g1_ref, g2_ref, x1_ref, xn_ref, *, d_a, d_b):
    n_split = 2 if a_ref.shape[0] % 32 == 0 else 1
    rows_per = a_ref.shape[0] // n_split
    for part in range(n_split):
        rows = slice(part * rows_per, (part + 1) * rows_per)
        mixed = _dot(a_ref[rows, :], w_ref[0:d_a, :])
        mixed += _dot(s_ref[rows, :], w_ref[d_a:d_a + d_b, :])
        mixed += _dot(c_ref[rows, :], w_ref[d_a + d_b:, :])
        x1 = x_ref[rows, :] + _rms(mixed, g1_ref[...])
        x1_ref[rows, :] = x1
        xn_ref[rows, :] = _rms(x1, g2_ref[...]).astype(xn_ref.dtype)


def _wo_block(attn, sgu, conv, w_o, x, g_post, g_pre_mlp, tm):
    m, d = x.shape
    d_a, d_b, d_c = attn.shape[1], sgu.shape[1], conv.shape[1]
    row = lambda width: pl.BlockSpec((tm, width), lambda i: (i, 0))
    const = lambda shape: pl.BlockSpec(shape, lambda i: (0, 0))
    w_spec = const((d, d))
    return pl.pallas_call(
        functools.partial(_wo_kernel, d_a=d_a, d_b=d_b),
        grid=(m // tm,),
        in_specs=[row(d_a), row(d_b), row(d_c), w_spec, row(d), const((1, d)), const((1, d))],
        out_specs=[row(d), row(d)],
        out_shape=[jax.ShapeDtypeStruct((m, d), F32), jax.ShapeDtypeStruct((m, d), BF16)],
        compiler_params=_params(("arbitrary",), 48),
        name="wo_norm_residual",
    )(attn, sgu, conv, w_o, x, g_post, g_pre_mlp)


def _down_kernel(*refs, emit_next, dec):
    refs = list(refs)
    if dec is not None:
        refs = refs[1:]
    h_ref, w_ref, x1_ref, g_ref = refs[:4]
    pos = 4
    if emit_next:
        gn_ref = refs[pos]
        pos += 1
    if dec is not None:
        dec_in = refs[pos:pos + 2 + 3 * dec.n_grp]
        pos += 2 + 3 * dec.n_grp
    x2_ref = refs[pos]
    pos += 1
    if emit_next:
        xn_ref = refs[pos]
        pos += 1
    if dec is not None:
        dec_out = refs[pos]
        pos += 1
    acc_ref = refs[pos]
    dec_state = refs[pos + 1:]
    k = pl.program_id(1)

    @pl.when(k == 0)
    def _():
        acc_ref[...] = jnp.zeros_like(acc_ref)

    if dec is not None:
        dec_p = lax.rem(pl.program_id(0) * pl.num_programs(1) + k, dec.n_steps)
        run_decode = functools.partial(_decode_step, p=dec_p, dec=dec, in_refs=dec_in, o_ref=dec_out,
                                       state=dec_state)
        run_decode("init")
        dec_carry = run_decode("scores")
    acc_ref[...] += _dot(h_ref[...], w_ref[...])
    if dec is not None:
        run_decode("values", carry=dec_carry)

    @pl.when(k == pl.num_programs(1) - 1)
    def _():
        x2 = x1_ref[...] + _rms(acc_ref[...], g_ref[...])
        x2_ref[...] = x2
        if emit_next:
            xn_ref[...] = _rms(x2, gn_ref[...]).astype(xn_ref.dtype)

    if dec is not None:
        run_decode("finish")


def _down_block(hid, w_down, x1, g_post, g_next, tm, tk, dec=None):
    m, d = x1.shape
    kdim = hid.shape[1]
    n_i, n_k = m // tm, kdim // tk
    emit_next = g_next is not None
    row = pl.BlockSpec((tm, d), lambda i, k, *_: (i, 0))
    const = pl.BlockSpec((1, d), lambda i, k, *_: (0, 0))
    x1_spec = pl.BlockSpec((tm, d), lambda i, k, *_: (i, 0), pipeline_mode=pl.Buffered(1))
    in_specs = [pl.BlockSpec((tm, tk), lambda i, k, *_: (i, k)),
                pl.BlockSpec((tk, d), lambda i, k, *_: (k, 0)), x1_spec, const]
    args = [hid, w_down, x1, g_post]
    out_specs = [row]
    out_shape = [jax.ShapeDtypeStruct((m, d), F32)]
    scratch = [pltpu.VMEM((tm, d), F32)]
    if emit_next:
        in_specs.append(const)
        args.append(g_next)
        out_specs.append(row)
        out_shape.append(jax.ShapeDtypeStruct((m, d), BF16))
    n_prefetch = 0
    if dec is not None:
        assert dec.n_items == n_i * n_k
        dec_specs, dec_args, dec_out_spec, dec_out_shape, dec_scratch = dec.operands(lambda i, k: i * n_k + k)
        in_specs += dec_specs
        args = [dec.page_table] + args + dec_args
        out_specs.append(dec_out_spec)
        out_shape.append(dec_out_shape)
        scratch += dec_scratch
        n_prefetch = 1
    grid_spec = pltpu.PrefetchScalarGridSpec(
        num_scalar_prefetch=n_prefetch, grid=(n_i, n_k), in_specs=in_specs, out_specs=out_specs,
        scratch_shapes=scratch)
    out = pl.pallas_call(
        functools.partial(_down_kernel, emit_next=emit_next, dec=None if dec is None else dec.static),
        grid_spec=grid_spec,
        out_shape=out_shape,
        compiler_params=_params(("arbitrary", "arbitrary"), 60),
        name="down_norm_residual",
    )(*args)
    return out[0], (out[1] if emit_next else None), (out[-1] if dec is not None else None)


def _prefix_kernel(zf_ref, bf_ref, logf_ref, cq_ref, ct_ref, c_sc, *, t, n_heads):
    lf = _log_sigmoid(zf_ref[0] + bf_ref[...])
    logf_ref[0] = lf[:, 0:n_heads]
    r_i = lax.broadcasted_iota(jnp.int32, (CHUNK, CHUNK), 0)
    c_i = lax.broadcasted_iota(jnp.int32, (CHUNK, CHUNK), 1)
    tri = _ones_where(r_i >= c_i)
    carry = jnp.zeros((1, 128), F32)
    for blk in range(t // CHUNK):
        cb = _dot3_left(tri, lf[blk * CHUNK:(blk + 1) * CHUNK, :]) + carry
        c_sc[blk * CHUNK:(blk + 1) * CHUNK, :] = cb * LOG2E
        carry = cb[CHUNK - 1:CHUNK, :]
    c = c_sc[...]
    ct_ref[0] = c.T[0:n_heads, :]
    for h in range(n_heads):
        cq_ref[0, h] = c[:, h:h + 1]


def _fox_prefix(z3, bf_pad, n_heads, f_blk):
    b, t, _ = z3.shape
    return pl.pallas_call(
        functools.partial(_prefix_kernel, t=t, n_heads=n_heads),
        grid=(b,),
        in_specs=[pl.BlockSpec((1, t, 128), lambda i: (i, 0, f_blk)), pl.BlockSpec((1, 128), lambda i: (0, 0))],
        out_specs=[
            pl.BlockSpec((1, t, n_heads), lambda i: (i, 0, 0)),
            pl.BlockSpec((1, n_heads, t, 1), lambda i: (i, 0, 0, 0)),
            pl.BlockSpec((1, n_heads, t), lambda i: (i, 0, 0)),
        ],
        out_shape=[
            jax.ShapeDtypeStruct((b, t, n_heads), F32),
            jax.ShapeDtypeStruct((b, n_heads, t, 1), F32),
            jax.ShapeDtypeStruct((b, n_heads, t), F32),
        ],
        scratch_shapes=[pltpu.VMEM((t, 128), F32)],
        compiler_params=_params(("arbitrary",), 48),
        name="fox_prefix",
    )(z3, bf_pad)


def _fox_attn_kernel(qi_ref, ki_ref, q_ref, k_ref, v_ref, cq_ref, ck_ref, o_ref, m_sc, l_sc, acc_sc, cq_sc,
                     *, tq, n_heads):
    step = pl.program_id(1)
    qi = qi_ref[step]
    ki = ki_ref[step]
    n_sub = tq // 128

    @pl.when(ki == 0)
    def _():
        m_sc[...] = jnp.full_like(m_sc, -jnp.inf)
        l_sc[...] = jnp.zeros_like(l_sc)
        acc_sc[...] = jnp.zeros_like(acc_sc)
        for h in range(n_heads):
            cq_sc[h] = jnp.broadcast_to(cq_ref[0, h], (tq, 128))

    def block(diagonal):
        if diagonal:
            keep = (lax.broadcasted_iota(jnp.int32, (tq, tq), 0) >= lax.broadcasted_iota(jnp.int32, (tq, tq), 1))
        head_cols = lambda h: slice(h * HEAD_DIM, (h + 1) * HEAD_DIM)
        qk = lambda h: _dot_nt(q_ref[0, :, head_cols(h)], k_ref[0, :, head_cols(h)])
        qk_next = qk(0)
        for h in range(n_heads):
            hs = head_cols(h)
            s = qk_next - ck_ref[0, h:h + 1, :]
            if h + 1 < n_heads:
                qk_next = qk(h + 1)
            if diagonal:
                s = jnp.where(keep, s, -jnp.inf)
            subs = [s[:, j * 128:(j + 1) * 128] for j in range(n_sub)]
            mc = subs[0]
            for x in subs[1:]:
                mc = jnp.maximum(mc, x)
            cq = cq_sc[h]
            m_prev = m_sc[h]
            m_new = jnp.maximum(m_prev, jnp.max(mc, axis=1, keepdims=True) + cq)
            alpha = jnp.exp2(m_prev - m_new)
            shift = m_new - cq
            ps = [jnp.exp2(x - shift) for x in subs]
            lsum = ps[0]
            for x in ps[1:]:
                lsum = lsum + x
            l_sc[h] = alpha * l_sc[h] + jnp.sum(lsum, axis=1, keepdims=True)
            p = jnp.concatenate(ps, axis=1).astype(BF16)
            acc_sc[:, hs] = alpha * acc_sc[:, hs] + _dot(p, v_ref[0, :, hs])
            m_sc[h] = m_new

    @pl.when(ki < qi)
    def _():
        block(False)

    @pl.when(ki == qi)
    def _():
        block(True)
        for h in range(n_heads):
            hs = slice(h * HEAD_DIM, (h + 1) * HEAD_DIM)
            o_ref[0, :, hs] = (acc_sc[:, hs] / l_sc[h]).astype(o_ref.dtype)


def _fox_attn_prompt(q, k, v, cq, ct, n_heads, tq):
    b, t, d_a = q.shape
    nq = t // tq
    pairs = [(i, j) for i in range(nq) for j in range(i + 1)]
    qi_tab = jnp.asarray([pr[0] for pr in pairs], jnp.int32)
    ki_tab = jnp.asarray([pr[1] for pr in pairs], jnp.int32)
    q_blk = pl.BlockSpec((1, tq, d_a), lambda bi, s, qt, kt: (bi, qt[s], 0))
    kv_blk = pl.BlockSpec((1, tq, d_a), lambda bi, s, qt, kt: (bi, kt[s], 0))
    grid_spec = pltpu.PrefetchScalarGridSpec(
        num_scalar_prefetch=2,
        grid=(b, len(pairs)),
        in_specs=[
            q_blk, kv_blk, kv_blk,
            pl.BlockSpec((1, n_heads, tq, 1), lambda bi, s, qt, kt: (bi, 0, qt[s], 0)),
            pl.BlockSpec((1, n_heads, tq), lambda bi, s, qt, kt: (bi, 0, kt[s])),
        ],
        out_specs=q_blk,
        scratch_shapes=[
            pltpu.VMEM((n_heads, tq, 128), F32),
            pltpu.VMEM((n_heads, tq, 128), F32),
            pltpu.VMEM((tq, d_a), F32),
            pltpu.VMEM((n_heads, tq, 128), F32),
        ],
    )
    return pl.pallas_call(
        functools.partial(_fox_attn_kernel, tq=tq, n_heads=n_heads),
        grid_spec=grid_spec,
        out_shape=jax.ShapeDtypeStruct((b, t, d_a), BF16),
        compiler_params=_params(("arbitrary", "arbitrary"), 48),
        name="fox_attn_prompt",
    )(qi_tab, ki_tab, q, k, v, cq, ct)


def _layernorm_silu(y, g, b):
    mu = jnp.mean(y, axis=-1, keepdims=True)
    yc = y - mu
    yn = yc * lax.rsqrt(jnp.mean(yc * yc, axis=-1, keepdims=True) + EPS) * g + b
    return yn * _sigmoid(yn)


def _mixer_kernel(ub_ref, vb_ref, ac_ref, gc_ref, gv_ref, ws_ref, bst_ref, cw_ref, cb_ref, lg_ref, lb_ref,
                  sgu_ref, conv_ref, tail_ref, g_sc, *, tm, n_heads_b, conv_w, halo):
    ti = pl.program_id(1)

    @pl.when(ti == 0)
    def _():
        g_sc[0:halo, :] = jnp.zeros((halo, g_sc.shape[1]), F32)

    u = _gelu(ub_ref[0])
    vn = _rms(_gelu(vb_ref[0]), gv_ref[...])
    r_i = lax.broadcasted_iota(jnp.int32, (CHUNK, CHUNK), 0)
    c_i = lax.broadcasted_iota(jnp.int32, (CHUNK, CHUNK), 1)
    for h in range(n_heads_b):
        hs = slice(h * HEAD_DIM, (h + 1) * HEAD_DIM)
        w_h = jnp.where(r_i >= c_i, ws_ref[h], 0.0).astype(BF16)
        bias_h = bst_ref[:, h:h + 1]
        for c in range(tm // CHUNK):
            rs = slice(c * CHUNK, (c + 1) * CHUNK)
            mix = _dot(w_h, vn[rs, hs].astype(BF16)) + bias_h
            sgu_ref[0, rs, hs] = (u[rs, hs] * mix).astype(sgu_ref.dtype)

    g_sc[halo:halo + tm, :] = ac_ref[0] * _sigmoid(gc_ref[0])
    base = halo - (conv_w - 1)
    n_rows = halo + tm
    window = g_sc[...]
    rotated = {0: window}
    y = cb_ref[...]
    for k in range(conv_w):
        a, r = divmod(base + k, 8)
        if r not in rotated:
            rotated[r] = pltpu.roll(window, n_rows - r, 0)
        y = y + rotated[r][8 * a:8 * a + tm, :] * cw_ref[k:k + 1, :]
    conv_ref[0] = _layernorm_silu(y, lg_ref[...], lb_ref[...]).astype(conv_ref.dtype)

    @pl.when(ti == pl.num_programs(1) - 1)
    def _():
        tail_ref[0] = g_sc[halo + tm - (conv_w - 1):halo + tm, :]

    g_sc[0:halo, :] = g_sc[tm:tm + halo, :]


def _mixer_prompt(z3, g_v, w_s, b_s_t, conv_w, conv_b, ln_g, ln_b, tm):
    b, t, _ = z3.shape
    n_heads_b = w_s.shape[0]
    d_b = n_heads_b * HEAD_DIM
    d_c = conv_w.shape[1]
    kw = conv_w.shape[0]
    halo = 32
    zcol = lambda c: pl.BlockSpec((1, tm, d_b), lambda bi, ti: (bi, ti, c))
    const = lambda shape: pl.BlockSpec(shape, lambda bi, ti: (0,) * len(shape))
    return pl.pallas_call(
        functools.partial(_mixer_kernel, tm=tm, n_heads_b=n_heads_b, conv_w=kw, halo=halo),
        grid=(b, t // tm),
        in_specs=[zcol(0), zcol(1), zcol(2), zcol(3),
                  const((1, d_b)), const((n_heads_b, CHUNK, CHUNK)), const((CHUNK, n_heads_b)),
                  const((kw, d_c)), const((1, d_c)), const((1, d_c)), const((1, d_c))],
        out_specs=[
            pl.BlockSpec((1, tm, d_b), lambda bi, ti: (bi, ti, 0)),
            pl.BlockSpec((1, tm, d_c), lambda bi, ti: (bi, ti, 0)),
            pl.BlockSpec((1, kw - 1, d_c), lambda bi, ti: (bi, 0, 0)),
        ],
        out_shape=[
            jax.ShapeDtypeStruct((b, t, d_b), BF16),
            jax.ShapeDtypeStruct((b, t, d_c), BF16),
            jax.ShapeDtypeStruct((b, kw - 1, d_c), F32),
        ],
        scratch_shapes=[pltpu.VMEM((halo + tm, d_c), F32)],
        compiler_params=_params(("arbitrary", "arbitrary"), 32),
        name="mixer_prompt",
    )(z3, z3, z3, z3, g_v, w_s, b_s_t, conv_w, conv_b, ln_g, ln_b)


def _mixer_sample_kernel(zs_ref, zf_ref, bf_ref, gv_ref, w0_ref, b0_ref, st_ref, cw_ref, cb_ref, lg_ref, lb_ref,
                         sgu_ref, conv_ref, state_ref, vn_ref, logf_ref, y_sc, *, d_b, conv_w):
    n = zs_ref.shape[0]
    u = _gelu(zs_ref[:, 0:d_b])
    vn = _rms(_gelu(zs_ref[:, d_b:2 * d_b]), gv_ref[...])
    vn_ref[...] = vn
    sgu_ref[...] = (u * (w0_ref[...] * vn + b0_ref[...])).astype(sgu_ref.dtype)
    glu = zs_ref[:, 2 * d_b:3 * d_b] * _sigmoid(zs_ref[:, 3 * d_b:4 * d_b])
    kw = conv_w - 1
    for bi in range(n):
        g_new = glu[bi:bi + 1, :]
        y_sc[bi:bi + 1, :] = (jnp.sum(st_ref[bi] * cw_ref[0:kw, :], axis=0, keepdims=True)
                              + g_new * cw_ref[kw:kw + 1, :])
        state_ref[bi, 0:kw - 1, :] = st_ref[bi, 1:kw, :]
        state_ref[bi, kw - 1:kw, :] = g_new
    conv_ref[...] = _layernorm_silu(y_sc[...] + cb_ref[...], lg_ref[...], lb_ref[...]).astype(conv_ref.dtype)
    logf_ref[...] = _log_sigmoid(zf_ref[...] + bf_ref[...])


def _mixer_sample(zs, zf, bf_pad, g_v, w0_row, b0_row, state, conv_w, conv_b, ln_g, ln_b):
    n = zs.shape[0]
    d_b = g_v.shape[1]
    kw, d_c = conv_w.shape
    full = lambda shape: pl.BlockSpec(shape, lambda i: (0,) * len(shape))
    args = (zs, zf, bf_pad, g_v, w0_row, b0_row, state, conv_w, conv_b, ln_g, ln_b)
    out_shape = [
        jax.ShapeDtypeStruct((n, d_b), BF16),
        jax.ShapeDtypeStruct((n, d_c), BF16),
        jax.ShapeDtypeStruct((n, kw - 1, d_c), F32),
        jax.ShapeDtypeStruct((n, d_b), F32),
        jax.ShapeDtypeStruct((n, 128), F32),
    ]
    return pl.pallas_call(
        functools.partial(_mixer_sample_kernel, d_b=d_b, conv_w=kw),
        grid=(1,),
        in_specs=[full(a.shape) for a in args],
        out_specs=[full(s.shape) for s in out_shape],
        out_shape=out_shape,
        scratch_shapes=[pltpu.VMEM((n, d_c), F32)],
        compiler_params=_params(("arbitrary",), 32),
        name="mixer_sample",
    )(*args)


def _decode_step(phase, p, dec, in_refs, o_ref, state, carry=None):
    n_grp, n_heads, scale = dec.n_grp, dec.n_heads, dec.scale
    zs_ref, cn_ref = in_refs[0], in_refs[1]
    refs = in_refs[2:]
    k_refs, v_refs, lf_refs = refs[:n_grp], refs[n_grp:2 * n_grp], refs[2 * n_grp:3 * n_grp]
    cn_sc, m_sc, l_sc, acc_sc, tail_sc = state
    rows_pp = k_refs[0].shape[2]
    n_flat = rows_pp // 128
    lane = lax.broadcasted_iota(jnp.int32, (n_heads, 128), 1)
    sub = lax.broadcasted_iota(jnp.int32, (n_heads, 128), 0)
    cls_mask = n_heads - 1
    own = sub == (lane & cls_mask)
    diag = sub == lane
    r_i = lax.broadcasted_iota(jnp.int32, (128, 128), 0)
    c_i = lax.broadcasted_iota(jnp.int32, (128, 128), 1)
    same = (r_i & cls_mask) == (c_i & cls_mask)

    if phase == "init":
        @pl.when(p == 0)
        def _():
            spread = _ones_where((r_i < n_heads) & ((c_i & cls_mask) == r_i))
            cn_sc[...] = _dot3(jnp.broadcast_to(cn_ref[0], (n_heads, 128)), spread)
            m_sc[...] = jnp.full_like(m_sc, -jnp.inf)
            l_sc[...] = jnp.zeros_like(l_sc)
            acc_sc[...] = jnp.zeros_like(acc_sc)
            tail_sc[...] = jnp.zeros_like(tail_sc)
        return

    if phase == "finish":
        @pl.when(p == dec.n_steps - 1)
        def _():
            m_col = jnp.max(jnp.where(diag, m_sc[...], -jnp.inf), axis=1, keepdims=True)
            l_row = jnp.sum(l_sc[...], axis=0, keepdims=True)
            l_col = jnp.sum(jnp.where(own, jnp.broadcast_to(l_row, (n_heads, 128)), 0.0), axis=1, keepdims=True)
            cn_col = jnp.sum(jnp.where(diag, cn_sc[...], 0.0), axis=1, keepdims=True)
            q_new = zs_ref[0, 0:n_heads, :]
            k_new = zs_ref[0, n_heads:2 * n_heads, :]
            v_new = zs_ref[0, 2 * n_heads:3 * n_heads, :]
            s_new = jnp.sum(q_new * k_new, axis=1, keepdims=True) * scale + cn_col - cn_col
            m_f = jnp.maximum(m_col, s_new)
            a_f = jnp.exp(m_col - m_f)
            p_new = jnp.exp(s_new - m_f)
            o_ref[0] = (a_f * acc_sc[...] + p_new * v_new) / (a_f * l_col + p_new)
        return

    if phase == "values":
        return _decode_values(dec, carry, v_refs, state, own, diag, n_flat)

    lf = jnp.concatenate([lf_refs[g][0, 0] for g in range(n_grp)], axis=0)
    n_rows = n_grp * n_flat
    row_tot = _dot3(lf, _ones_where(same))
    within = _dot3(lf, _ones_where(same & (r_i > c_i)))
    rr = lax.broadcasted_iota(jnp.int32, (n_rows, n_rows), 0)
    cc = lax.broadcasted_iota(jnp.int32, (n_rows, n_rows), 1)
    suffix = within + _dot3_left(_ones_where(cc > rr), row_tot) + tail_sc[0:1, :]

    q8 = zs_ref[0, 0:n_heads, :].astype(BF16)
    scores = []
    for g in range(n_grp):
        s_t = _dot_nt(q8, k_refs[g][0, 0].astype(BF16))
        flat = [jnp.sum(jnp.where(own, s_t[:, a * 128:(a + 1) * 128], 0.0), axis=0, keepdims=True)
                for a in range(n_flat)]
        scores.append(jnp.concatenate(flat, axis=0) * scale
                      + suffix[g * n_flat:(g + 1) * n_flat, :] + cn_sc[...])
    return scores, row_tot


def _decode_values(dec, carry, v_refs, state, own, diag, n_flat):
    n_grp, n_heads = dec.n_grp, dec.n_heads
    cn_sc, m_sc, l_sc, acc_sc, tail_sc = state
    scores, row_tot = carry
    m_step = scores[0]
    for x in scores[1:]:
        m_step = jnp.maximum(m_step, x)
    shift = 1
    while shift < n_flat:
        m_step = jnp.maximum(m_step, pltpu.roll(m_step, shift, 0))
        shift *= 2
    shift = n_heads
    while shift < 128:
        m_step = jnp.maximum(m_step, pltpu.roll(m_step, shift, 1))
        shift *= 2
    m_prev = m_sc[...]
    m_new = jnp.maximum(m_prev, m_step)
    alpha = jnp.exp(m_prev - m_new)
    probs = [jnp.exp(x - m_new) for x in scores]
    l_step = probs[0]
    for x in probs[1:]:
        l_step = l_step + x
    l_sc[...] = alpha * l_sc[...] + l_step
    pv = jnp.zeros((n_heads, HEAD_DIM), F32)
    for g in range(n_grp):
        blocks = [jnp.where(own, jnp.broadcast_to(probs[g][a:a + 1, :], (n_heads, 128)), 0.0)
                  for a in range(n_flat)]
        pv += _dot(jnp.concatenate(blocks, axis=1).astype(BF16), v_refs[g][0, 0].astype(BF16))
    alpha_col = jnp.sum(jnp.where(diag, alpha, 0.0), axis=1, keepdims=True)
    acc_sc[...] = alpha_col * acc_sc[...] + pv
    m_sc[...] = m_new
    tail_sc[...] += jnp.sum(row_tot, axis=0, keepdims=True)


class _DecodeStatic(NamedTuple):
    n_grp: int
    n_steps: int
    n_items: int
    n_heads: int
    scale: float


class _DecodePlan:
    def __init__(self, page_table, zs3, cn, cache_k, cache_v, cache_lf, layer, n_heads, b0, nb, n_grp):
        n_pages = page_table.shape[1]
        assert n_pages % n_grp == 0 and cache_k.shape[2] == 128 * n_heads and cache_lf.shape[2] == n_heads
        assert n_heads & (n_heads - 1) == 0 and n_heads <= 8
        self.page_table, self.layer, self.b0, self.nb, self.n_pages = page_table, layer, b0, nb, n_pages
        self.arrays = (zs3, cn, cache_k, cache_v, cache_lf)
        n_steps = n_pages // n_grp
        self.static = _DecodeStatic(n_grp, n_steps, nb * n_steps, n_heads, HEAD_DIM ** -0.5)
        self.n_items = nb * n_steps

    def operands(self, step_of):
        zs3, cn, cache_k, cache_v, cache_lf = self.arrays
        st, b0, layer, n_pages = self.static, self.b0, self.layer, self.n_pages
        n_heads, n_grp, n_steps = st.n_heads, st.n_grp, st.n_steps
        rows_pp = cache_k.shape[2]
        seq = lambda g0, g1: step_of(g0, g1) // n_steps

        def page_spec(block, g):
            def index(g0, g1, pt):
                s = step_of(g0, g1)
                return (pt[b0 + s // n_steps, n_pages - (s % n_steps + 1) * n_grp + g], layer, 0, 0)
            return pl.BlockSpec(block, index)

        kv_specs = [page_spec((1, 1, rows_pp, HEAD_DIM), g) for g in range(n_grp)]
        lf_specs = [page_spec((1, 1, n_heads, 128), g) for g in range(n_grp)]
        in_specs = [pl.BlockSpec((1, 3 * n_heads, HEAD_DIM), lambda g0, g1, pt: (b0 + seq(g0, g1), 0, 0)),
                    pl.BlockSpec((1, 1, 128), lambda g0, g1, pt: (b0 + seq(g0, g1), 0, 0))]
        in_specs += kv_specs + kv_specs + lf_specs
        args = [zs3, cn] + [cache_k] * n_grp + [cache_v] * n_grp + [cache_lf] * n_grp
        out_spec = pl.BlockSpec((1, n_heads, HEAD_DIM), lambda g0, g1, pt: (seq(g0, g1), 0, 0))
        out_shape = jax.ShapeDtypeStruct((self.nb, n_heads, HEAD_DIM), F32)
        scratch = [pltpu.VMEM((n_heads, 128), F32)] * 3 + [pltpu.VMEM((n_heads, HEAD_DIM), F32),
                                                           pltpu.VMEM((n_heads, 128), F32)]
        return in_specs, args, out_spec, out_shape, scratch


def kernel(x_prompt, x_sample, cache_k, cache_v, cache_logf, state_conv, page_table, w_in, b_f, g_v, w_s, b_s,
           conv_w, conv_b, ln_c_g, ln_c_b, w_o, g_pre_mix, g_post_mix, g_pre_mlp, g_post_mlp, w_up, w_down):
    bp, tp, d = x_prompt.shape
    bs, ds, _ = x_sample.shape
    assert ds == 1, "the sample path handles exactly one new token per sequence"
    depth = w_in.shape[0]
    n_pool, _, page, n_heads, _ = cache_k.shape
    d_a = n_heads * HEAD_DIM
    n_heads_b = w_s.shape[1]
    d_b = n_heads_b * HEAD_DIM
    d_c = conv_w.shape[2]
    off_f = 3 * d_a
    assert w_in.shape[2] == off_f + n_heads + 2 * d_b + 2 * d_c and d_b == d_c and page == CHUNK
    mp = bp * tp

    n_rest = 2 * d_b + 2 * d_c
    w_in_t = jnp.swapaxes(w_in, 1, 2)
    d_ff = w_up.shape[2]
    bf_pad = jnp.pad(b_f, ((0, 0), (0, 128 - n_heads)))
    row2 = lambda a, l: a[l][None, :]
    proj = functools.partial(_ws_matmul, transposed=True)

    n_pages = page_table.shape[1]
    assert bs % 2 == 0
    grp_up = (bs // 2) * n_pages // ((d_ff // UP_TILE[1]) * (mp // UP_TILE[0]))
    grp_down = (bs // 2) * n_pages // ((mp // DOWN_TILE[0]) * (d_ff // DOWN_TILE[1]))

    cache_k2 = cache_k.reshape(n_pool, depth, page * n_heads, HEAD_DIM)
    cache_v2 = cache_v.reshape(n_pool, depth, page * n_heads, HEAD_DIM)
    cache_lf = cache_logf.reshape(n_pool, depth, page * n_heads // 128, 128)

    xp = x_prompt.reshape(mp, d)
    xs = x_sample.reshape(bs, d)
    xpn = _norm_bf16(xp, row2(g_pre_mix, 0), 512)
    xsn = _norm_bf16(xs, row2(g_pre_mix, 0), bs)
    fp_l, cp_l, ks_l, vs_l, fs_l, cs_l, us_l = ([] for _ in range(7))
    k_pages = v_pages = None
    for l in range(depth):
        g_next = row2(g_pre_mix, l + 1) if l + 1 < depth else None
        q, zf, zs_q, zs_f = proj(xpn, w_in_t, l, 0, d_a, 1024, d_a, "q", "proj_q", f_row=off_f, xs=xsn)
        kb, k_pages, zs_k = proj(xpn, w_in_t, l, d_a, d_a, 1024, d_a, "kv", "proj_k", xs=xsn,
                                 pages=(k_pages, bp, tp, depth, n_heads, page))
        vb, v_pages, zs_v = proj(xpn, w_in_t, l, 2 * d_a, d_a, 1024, d_a, "kv", "proj_v", xs=xsn,
                                 pages=(v_pages, bp, tp, depth, n_heads, page))
        z_rest, w_o_l, zs_rest = proj(xpn, w_in_t, l, off_f + n_heads, n_rest, 1024, 1024, "f32", "proj_rest",
                                      side_cast=w_o, xs=xsn)
        z3 = z_rest.reshape(bp, tp, n_rest)
        logf, cq, ct = _fox_prefix(zf.reshape(bp, tp, 128), bf_pad[l][None, :], n_heads, 0)
        attn = _fox_attn_prompt(q.reshape(bp, tp, d_a), kb.reshape(bp, tp, d_a), vb.reshape(bp, tp, d_a),
                                cq, ct, n_heads, 512)
        fp_l.append(logf)
        w0_row = jnp.repeat(w_s[l][:, 0, 0], HEAD_DIM)[None, :]
        b0_row = jnp.repeat(b_s[l][:, 0], HEAD_DIM)[None, :]
        sgu_s, conv_s, state_s, vn_s, logf_s = _mixer_sample(
            zs_rest, zs_f, bf_pad[l][None, :], row2(g_v, l), w0_row, b0_row, state_conv[l], conv_w[l],
            row2(conv_b, l), row2(ln_c_g, l), row2(ln_c_b, l))
        zs3 = jnp.concatenate([zs_q, zs_k, zs_v], axis=1).reshape(bs, 3 * n_heads, HEAD_DIM)
        plan = lambda b0, n_grp: _DecodePlan(page_table, zs3, logf_s[:, None, :], cache_k2, cache_v2, cache_lf,
                                             l, n_heads, b0, bs // 2, n_grp)
        sgu, conv, conv_tail = _mixer_prompt(z3, row2(g_v, l), w_s[l], b_s[l].T, conv_w[l], row2(conv_b, l),
                                             row2(ln_c_g, l), row2(ln_c_b, l), MIXER_TM)
        cp_l.append(conv_tail)
        x1, x1n = _wo_block(attn.reshape(mp, d_a), sgu.reshape(mp, d_b), conv.reshape(mp, d_c), w_o_l, xp,
                            row2(g_post_mix, l), row2(g_pre_mlp, l), 512)
        hid, w_down_l, attn_s0 = _ws_matmul(x1n, w_up, l, 0, d_ff, UP_TILE[0], UP_TILE[1], "relu2", "mlp_up",
                                            side_cast=w_down, dec=plan(0, grp_up))
        xp, xpn, attn_s1 = _down_block(hid, w_down_l, x1, row2(g_post_mlp, l), g_next, DOWN_TILE[0],
                                       DOWN_TILE[1], dec=plan(bs // 2, grp_down))
        attn_s = jnp.concatenate([attn_s0, attn_s1], axis=0).reshape(bs, d_a).astype(BF16)
        x1s, x1sn = _wo_block(attn_s, sgu_s, conv_s, w_o_l, xs, row2(g_post_mix, l), row2(g_pre_mlp, l), bs)
        hid_s = _ws_matmul(x1sn, w_up, l, 0, d_ff, bs, 1024, "relu2", "mlp_up_sample")
        xs, xsn, _ = _down_block(hid_s, w_down_l, x1s, row2(g_post_mlp, l), g_next, bs, 2048)
        ks_l.append(zs_k)
        vs_l.append(zs_v)
        fs_l.append(logf_s[:, :n_heads])
        cs_l.append(state_s)
        us_l.append(vn_s)

    npp = tp // page
    kv_out = lambda pages: pages.reshape(bp, npp, depth, page, n_heads, HEAD_DIM)
    logf_prompt = jnp.stack(fp_l, axis=1).reshape(bp, depth, npp, page, n_heads).swapaxes(1, 2)
    return (xp.reshape(bp, tp, d), xs.reshape(bs, ds, d),
            kv_out(k_pages), kv_out(v_pages), logf_prompt,
            jnp.stack(cp_l, axis=0),
            jnp.stack(ks_l, axis=1).reshape(bs, depth, ds, n_heads, HEAD_DIM),
            jnp.stack(vs_l, axis=1).reshape(bs, depth, ds, n_heads, HEAD_DIM),
            jnp.stack(fs_l, axis=1).reshape(bs, depth, ds, n_heads),
            jnp.stack(cs_l, axis=0),
            jnp.stack(us_l, axis=0).reshape(depth, bs, ds, d_b))
```

```python
import functools
from typing import NamedTuple

import numpy as np
import jax
import jax.numpy as jnp
from jax import lax
from jax.experimental import pallas as pl
from jax.experimental.pallas import tpu as pltpu

F32 = jnp.float32
BF16 = jnp.bfloat16

HEAD_DIM = 128
CHUNK = 128
EPS = 1e-6
MIB = 1024 * 1024
GELU_C = float(np.sqrt(2.0 / np.pi))
LOG2E = float(np.log2(np.e))

UP_TILE = (1024, 1024)
DOWN_TILE = (512, 2048)
MIXER_TM = 256


def _params(semantics, vmem_mib):
    return pltpu.CompilerParams(dimension_semantics=semantics, vmem_limit_bytes=vmem_mib * MIB)


def _rms(x, g):
    return x * lax.rsqrt(jnp.mean(x * x, axis=-1, keepdims=True) + EPS) * g


def _gelu(x):
    return x * (0.5 * (1.0 + jnp.tanh(GELU_C * (x + 0.044715 * (x * x * x)))))


def _sigmoid(x):
    return 1.0 / (1.0 + jnp.exp(-x))


def _log_sigmoid(x):
    return jnp.minimum(x, 0.0) - jnp.log1p(jnp.exp(-jnp.abs(x)))


def _split3(x):
    hi = x.astype(BF16)
    r = x - hi.astype(F32)
    mid = r.astype(BF16)
    lo = (r - mid.astype(F32)).astype(BF16)
    return hi, mid, lo


def _dot(a, b):
    return jnp.dot(a, b, preferred_element_type=F32)


def _dot_nt(a, b):
    return lax.dot_general(a, b, (((1,), (1,)), ((), ())), preferred_element_type=F32)


def _dot3(x, w):
    hi, mid, lo = _split3(x)
    return (_dot(hi, w) + _dot(mid, w)) + _dot(lo, w)


def _dot3_left(w, x):
    hi, mid, lo = _split3(x)
    return (_dot(w, hi) + _dot(w, mid)) + _dot(w, lo)


def _ones_where(cond):
    return jnp.where(cond, 1.0, 0.0).astype(BF16)


def _norm_kernel(x_ref, g_ref, o_ref):
    o_ref[...] = _rms(x_ref[...], g_ref[...]).astype(o_ref.dtype)


def _norm_bf16(x, g, tm):
    m, d = x.shape
    return pl.pallas_call(
        _norm_kernel,
        grid=(m // tm,),
        in_specs=[pl.BlockSpec((tm, d), lambda i: (i, 0)), pl.BlockSpec((1, d), lambda i: (0, 0))],
        out_specs=pl.BlockSpec((tm, d), lambda i: (i, 0)),
        out_shape=jax.ShapeDtypeStruct((m, d), BF16),
        compiler_params=_params(("arbitrary",), 32),
        name="pre_norm",
    )(x, g)


def _ws_kernel(*refs, mode, transposed, with_f, has_prev, kv_slot, side_cast, with_xs, norm_in, dec, n_heads,
               page, q_scale):
    refs = list(refs)
    if dec is not None:
        refs = refs[1:]
    x_ref, w_ref = refs[0], refs[1]
    pos = 2
    if with_f:
        wf_ref = refs[pos]
        pos += 1
    if has_prev:
        pos += 1
    if side_cast:
        side_in = refs[pos]
        pos += 1
    if with_xs:
        xs_ref = refs[pos]
        pos += 1
    if norm_in:
        gn_ref = refs[pos]
        pos += 1
    if dec is not None:
        dec_in = refs[pos:pos + 2 + 3 * dec.n_grp]
        pos += 2 + 3 * dec.n_grp
    n_out = 2 if (mode == "kv" or with_f) else 1
    outs = refs[pos:pos + n_out]
    pos += n_out
    if side_cast:
        refs[pos][...] = side_in[...].astype(BF16)
        pos += 1
    if with_xs:
        xs_outs = refs[pos:pos + (2 if with_f else 1)]
        pos += len(xs_outs)
    if norm_in:
        xn_out = refs[pos]
        pos += 1
    if dec is not None:
        dec_out = refs[pos]
        pos += 1
    w_sc = refs[pos]
    pos += 1
    if with_f:
        wf_sc = refs[pos]
        pos += 1
    dec_state = refs[pos:]

    @pl.when(pl.program_id(1) == 0)
    def _():
        w_sc[...] = (w_ref[0] if transposed else w_ref[...]).astype(BF16)
        if with_f:
            pad = jnp.zeros((wf_sc.shape[0] - wf_ref.shape[1], wf_sc.shape[1]), F32)
            wf_sc[...] = jnp.concatenate([wf_ref[0], pad], axis=0).astype(BF16)
        if with_xs:
            xs = xs_ref[...]
            xs_outs[0][...] = _dot_nt(xs, w_sc[...]) if transposed else _dot(xs, w_sc[...])
            if with_f:
                xs_outs[1][...] = _dot_nt(xs, wf_sc[...])

    if dec is not None:
        dec_p = lax.rem(pl.program_id(0) * pl.num_programs(1) + pl.program_id(1), dec.n_steps)
        run_decode = functools.partial(_decode_step, p=dec_p, dec=dec, in_refs=dec_in, o_ref=dec_out,
                                       state=dec_state)
        run_decode("init")
        dec_carry = run_decode("scores")
    if norm_in:
        x = _rms(x_ref[...], gn_ref[...]).astype(BF16)
        xn_out[...] = x
    else:
        x = x_ref[...]
    acc = _dot_nt(x, w_sc[...]) if transposed else _dot(x, w_sc[...])
    if with_f:
        outs[1][...] = _dot_nt(x, wf_sc[...])
    if mode == "f32":
        outs[0][...] = acc
    elif mode == "q":
        outs[0][...] = (acc * q_scale).astype(BF16)
    elif mode == "relu2":
        h = jnp.maximum(acc, 0.0)
        outs[0][...] = (h * h).astype(BF16)
    else:
        outs[0][...] = acc.astype(BF16)
        dst_ref = outs[1]
        slot = () if has_prev else (kv_slot,)
        for pg in range(acc.shape[0] // page):
            for h in range(n_heads):
                dst_ref[(0, pg) + slot + (pl.ds(h, page, stride=n_heads), slice(None))] = (
                    acc[pg * page:(pg + 1) * page, h * HEAD_DIM:(h + 1) * HEAD_DIM])
        if not has_prev:
            for other in range(dst_ref.shape[2]):
                if other != kv_slot:
                    dst_ref[0, :, other] = jnp.zeros(dst_ref.shape[1:2] + dst_ref.shape[3:], F32)
    if dec is not None:
        run_decode("values", carry=dec_carry)
        run_decode("finish")


def _ws_matmul(x, w, layer, col0, n_cols, tm, tn, mode, name, *, transposed=False, f_row=None, pages=None,
               side_cast=None, xs=None, norm_g=None, dec=None):
    m, k = x.shape
    assert n_cols % tn == 0 and (f_row is None or (n_cols == tn and transposed))
    n_j, n_i = n_cols // tn, m // tm
    if transposed:
        w_spec = pl.BlockSpec((pl.Element(1), pl.Element(tn), pl.Element(k)),
                              lambda j, i, *_: (layer, pl.multiple_of(col0 + j * tn, 8), 0))
    else:
        assert col0 % tn == 0
        w_spec = pl.BlockSpec((None, k, tn), lambda j, i, *_: (layer, 0, col0 // tn + j))
    in_specs = [pl.BlockSpec((tm, k), lambda j, i, *_: (i, 0)), w_spec]
    args = [x, w]
    out_specs = [pl.BlockSpec((tm, tn), lambda j, i, *_: (i, j))]
    out_shape = [jax.ShapeDtypeStruct((m, n_cols), F32 if mode == "f32" else BF16)]
    scratch = [pltpu.VMEM((tn, k) if transposed else (k, tn), BF16)]
    if f_row is not None:
        in_specs.append(pl.BlockSpec((pl.Element(1), pl.Element(8), pl.Element(k)),
                                     lambda j, i, *_: (layer, f_row, 0)))
        args.append(w)
        out_specs.append(pl.BlockSpec((tm, 128), lambda j, i, *_: (i, 0)))
        out_shape.append(jax.ShapeDtypeStruct((m, 128), F32))
        scratch.append(pltpu.VMEM((128, k), BF16))
    aliases = {}
    n_heads = page = 0
    has_prev = False
    if mode == "kv":
        prev, bsz, t, depth, n_heads, page = pages
        assert n_cols == tn == n_heads * HEAD_DIM and t % tm == 0 and tm % page == 0
        tps = t // tm
        if prev is None:
            out_specs.append(pl.BlockSpec((1, tm // page, depth, page * n_heads, HEAD_DIM),
                                          lambda j, i, *_: (i // tps, i % tps, 0, 0, 0)))
        else:
            out_specs.append(pl.BlockSpec((1, tm // page, None, page * n_heads, HEAD_DIM),
                                          lambda j, i, *_: (i // tps, i % tps, layer, 0, 0)))
        out_shape.append(jax.ShapeDtypeStruct((bsz, t // page, depth, page * n_heads, HEAD_DIM), F32))
        if prev is not None:
            has_prev = True
            in_specs.append(pl.BlockSpec(memory_space=pl.ANY))
            args.append(prev)
            aliases = {2: 1}
    if side_cast is not None:
        _, side_r, side_c = side_cast.shape
        rows = side_r // (n_j * n_i)
        assert rows * n_j * n_i == side_r and rows % 16 == 0
        in_specs.append(pl.BlockSpec((None, rows, side_c), lambda j, i, *_: (layer, j * n_i + i, 0)))
        args.append(side_cast)
        out_specs.append(pl.BlockSpec((rows, side_c), lambda j, i, *_: (j * n_i + i, 0)))
        out_shape.append(jax.ShapeDtypeStruct((side_r, side_c), BF16))
    if xs is not None:
        ns = xs.shape[0]
        in_specs.append(pl.BlockSpec((ns, k), lambda j, i, *_: (0, 0)))
        args.append(xs)
        out_specs.append(pl.BlockSpec((ns, tn), lambda j, i, *_: (0, j)))
        out_shape.append(jax.ShapeDtypeStruct((ns, n_cols), F32))
        if f_row is not None:
            out_specs.append(pl.BlockSpec((ns, 128), lambda j, i, *_: (0, 0)))
            out_shape.append(jax.ShapeDtypeStruct((ns, 128), F32))
    if norm_g is not None:
        assert n_j == 1
        in_specs.append(pl.BlockSpec((1, k), lambda j, i, *_: (0, 0)))
        args.append(norm_g)
        out_specs.append(pl.BlockSpec((tm, k), lambda j, i, *_: (i, 0)))
        out_shape.append(jax.ShapeDtypeStruct((m, k), BF16))
    n_prefetch = 0
    if dec is not None:
        assert dec.n_items == n_j * n_i
        dec_specs, dec_args, dec_out_spec, dec_out_shape, dec_scratch = dec.operands(lambda j, i: j * n_i + i)
        in_specs += dec_specs
        args = [dec.page_table] + args + dec_args
        out_specs.append(dec_out_spec)
        out_shape.append(dec_out_shape)
        scratch += dec_scratch
        n_prefetch = 1
    grid_spec = pltpu.PrefetchScalarGridSpec(
        num_scalar_prefetch=n_prefetch, grid=(n_j, n_i), in_specs=in_specs, out_specs=out_specs,
        scratch_shapes=scratch)
    out = pl.pallas_call(
        functools.partial(_ws_kernel, mode=mode, transposed=transposed, with_f=f_row is not None,
                          has_prev=has_prev, kv_slot=layer, side_cast=side_cast is not None, with_xs=xs is not None,
                          norm_in=norm_g is not None,
                          dec=None if dec is None else dec.static, n_heads=n_heads, page=page,
                          q_scale=HEAD_DIM ** -0.5 * LOG2E),
        grid_spec=grid_spec,
        out_shape=out_shape,
        input_output_aliases=aliases,
        compiler_params=_params(("arbitrary", "arbitrary"), 58),
        name=name,
    )(*args)
    return out if len(out) > 1 else out[0]


def _wo_kernel(a_ref, s_ref, c_ref, w_ref, x_ref, g1_ref, g2_ref, x1_ref, xn_ref, *, d_a, d_b):
    n_split = 2 if a_ref.shape[0] % 32 == 0 else 1
    rows_per = a_ref.shape[0] // n_split
    for part in range(n_split):
        rows = slice(part * rows_per, (part + 1) * rows_per)
        mixed = _dot(a_ref[rows, :], w_ref[0:d_a, :])
        mixed += _dot(s_ref[rows, :], w_ref[d_a:d_a + d_b, :])
        mixed += _dot(c_ref[rows, :], w_ref[d_a + d_b:, :])
        x1 = x_ref[rows, :] + _rms(mixed, g1_ref[...])
        x1_ref[rows, :] = x1
        xn_ref[rows, :] = _rms(x1, g2_ref[...]).astype(xn_ref.dtype)


def _wo_block(attn, sgu, conv, w_o, x, g_post, g_pre_mlp, tm):
    m, d = x.shape
    d_a, d_b, d_c = attn.shape[1], sgu.shape[1], conv.shape[1]
    row = lambda width: pl.BlockSpec((tm, width), lambda i: (i, 0))
    const = lambda shape: pl.BlockSpec(shape, lambda i: (0, 0))
    w_spec = const((d, d))
    return pl.pallas_call(
        functools.partial(_wo_kernel, d_a=d_a, d_b=d_b),
        grid=(m // tm,),
        in_specs=[row(d_a), row(d_b), row(d_c), w_spec, row(d), const((1, d)), const((1, d))],
        out_specs=[row(d), row(d)],
        out_shape=[jax.ShapeDtypeStruct((m, d), F32), jax.ShapeDtypeStruct((m, d), BF16)],
        compiler_params=_params(("arbitrary",), 48),
        name="wo_norm_residual",
    )(attn, sgu, conv, w_o, x, g_post, g_pre_mlp)


def _down_kernel(*refs, emit_next, dec):
    refs = list(refs)
    if dec is not None:
        refs = refs[1:]
    h_ref, w_ref, x1_ref, g_ref = refs[:4]
    pos = 4
    if emit_next:
        gn_ref = refs[pos]
        pos += 1
    if dec is not None:
        dec_in = refs[pos:pos + 2 + 3 * dec.n_grp]
        pos += 2 + 3 * dec.n_grp
    x2_ref = refs[pos]
    pos += 1
    if emit_next:
        xn_ref = refs[pos]
        pos += 1
    if dec is not None:
        dec_out = refs[pos]
        pos += 1
    acc_ref = refs[pos]
    dec_state = refs[pos + 1:]
    k = pl.program_id(1)

    @pl.when(k == 0)
    def _():
        acc_ref[...] = jnp.zeros_like(acc_ref)

    if dec is not None:
        dec_p = lax.rem(pl.program_id(0) * pl.num_programs(1) + k, dec.n_steps)
        run_decode = functools.partial(_decode_step, p=dec_p, dec=dec, in_refs=dec_in, o_ref=dec_out,
                                       state=dec_state)
        run_decode("init")
        dec_carry = run_decode("scores")
    acc_ref[...] += _dot(h_ref[...], w_ref[...])
    if dec is not None:
        run_decode("values", carry=dec_carry)

    @pl.when(k == pl.num_programs(1) - 1)
    def _():
        x2 = x1_ref[...] + _rms(acc_ref[...], g_ref[...])
        x2_ref[...] = x2
        if emit_next:
            xn_ref[...] = _rms(x2, gn_ref[...]).astype(xn_ref.dtype)

    if dec is not None:
        run_decode("finish")


def _down_block(hid, w_down, x1, g_post, g_next, tm, tk, dec=None):
    m, d = x1.shape
    kdim = hid.shape[1]
    n_i, n_k = m // tm, kdim // tk
    emit_next = g_next is not None
    row = pl.BlockSpec((tm, d), lambda i, k, *_: (i, 0))
    const = pl.BlockSpec((1, d), lambda i, k, *_: (0, 0))
    x1_spec = pl.BlockSpec((tm, d), lambda i, k, *_: (i, 0), pipeline_mode=pl.Buffered(1))
    in_specs = [pl.BlockSpec((tm, tk), lambda i, k, *_: (i, k)),
                pl.BlockSpec((tk, d), lambda i, k, *_: (k, 0)), x1_spec, const]
    args = [hid, w_down, x1, g_post]
    out_specs = [row]
    out_shape = [jax.ShapeDtypeStruct((m, d), F32)]
    scratch = [pltpu.VMEM((tm, d), F32)]
    if emit_next:
        in_specs.append(const)
        args.append(g_next)
        out_specs.append(row)
        out_shape.append(jax.ShapeDtypeStruct((m, d), BF16))
    n_prefetch = 0
    if dec is not None:
        assert dec.n_items == n_i * n_k
        dec_specs, dec_args, dec_out_spec, dec_out_shape, dec_scratch = dec.operands(lambda i, k: i * n_k + k)
        in_specs += dec_specs
        args = [dec.page_table] + args + dec_args
        out_specs.append(dec_out_spec)
        out_shape.append(dec_out_shape)
        scratch += dec_scratch
        n_prefetch = 1
    grid_spec = pltpu.PrefetchScalarGridSpec(
        num_scalar_prefetch=n_prefetch, grid=(n_i, n_k), in_specs=in_specs, out_specs=out_specs,
        scratch_shapes=scratch)
    out = pl.pallas_call(
        functools.partial(_down_kernel, emit_next=emit_next, dec=None if dec is None else dec.static),
        grid_spec=grid_spec,
        out_shape=out_shape,
        compiler_params=_params(("arbitrary", "arbitrary"), 60),
        name="down_norm_residual",
    )(*args)
    return out[0], (out[1] if emit_next else None), (out[-1] if dec is not None else None)


def _prefix_kernel(zf_ref, bf_ref, logf_ref, cq_ref, ct_ref, c_sc, *, t, n_heads):
    lf = _log_sigmoid(zf_ref[0] + bf_ref[...])
    logf_ref[0] = lf[:, 0:n_heads]
    r_i = lax.broadcasted_iota(jnp.int32, (CHUNK, CHUNK), 0)
    c_i = lax.broadcasted_iota(jnp.int32, (CHUNK, CHUNK), 1)
    tri = _ones_where(r_i >= c_i)
    carry = jnp.zeros((1, 128), F32)
    for blk in range(t // CHUNK):
        cb = _dot3_left(tri, lf[blk * CHUNK:(blk + 1) * CHUNK, :]) + carry
        c_sc[blk * CHUNK:(blk + 1) * CHUNK, :] = cb * LOG2E
        carry = cb[CHUNK - 1:CHUNK, :]
    c = c_sc[...]
    ct_ref[0] = c.T[0:n_heads, :]
    for h in range(n_heads):
        cq_ref[0, h] = c[:, h:h + 1]


def _fox_prefix(z3, bf_pad, n_heads, f_blk):
    b, t, _ = z3.shape
    return pl.pallas_call(
        functools.partial(_prefix_kernel, t=t, n_heads=n_heads),
        grid=(b,),
        in_specs=[pl.BlockSpec((1, t, 128), lambda i: (i, 0, f_blk)), pl.BlockSpec((1, 128), lambda i: (0, 0))],
        out_specs=[
            pl.BlockSpec((1, t, n_heads), lambda i: (i, 0, 0)),
            pl.BlockSpec((1, n_heads, t, 1), lambda i: (i, 0, 0, 0)),
            pl.BlockSpec((1, n_heads, t), lambda i: (i, 0, 0)),
        ],
        out_shape=[
            jax.ShapeDtypeStruct((b, t, n_heads), F32),
            jax.ShapeDtypeStruct((b, n_heads, t, 1), F32),
            jax.ShapeDtypeStruct((b, n_heads, t), F32),
        ],
        scratch_shapes=[pltpu.VMEM((t, 128), F32)],
        compiler_params=_params(("arbitrary",), 48),
        name="fox_prefix",
    )(z3, bf_pad)


def _fox_attn_kernel(qi_ref, ki_ref, q_ref, k_ref, v_ref, cq_ref, ck_ref, o_ref, m_sc, l_sc, acc_sc, cq_sc,
                     *, tq, n_heads):
    step = pl.program_id(1)
    qi = qi_ref[step]
    ki = ki_ref[step]
    n_sub = tq // 128

    @pl.when(ki == 0)
    def _():
        m_sc[...] = jnp.full_like(m_sc, -jnp.inf)
        l_sc[...] = jnp.zeros_like(l_sc)
        acc_sc[...] = jnp.zeros_like(acc_sc)
        for h in range(n_heads):
            cq_sc[h] = jnp.broadcast_to(cq_ref[0, h], (tq, 128))

    def block(diagonal):
        if diagonal:
            keep = (lax.broadcasted_iota(jnp.int32, (tq, tq), 0) >= lax.broadcasted_iota(jnp.int32, (tq, tq), 1))
        head_cols = lambda h: slice(h * HEAD_DIM, (h + 1) * HEAD_DIM)
        qk = lambda h: _dot_nt(q_ref[0, :, head_cols(h)], k_ref[0, :, head_cols(h)])
        qk_next = qk(0)
        for h in range(n_heads):
            hs = head_cols(h)
            s = qk_next - ck_ref[0, h:h + 1, :]
            if h + 1 < n_heads:
                qk_next = qk(h + 1)
            if diagonal:
                s = jnp.where(keep, s, -jnp.inf)
            subs = [s[:, j * 128:(j + 1) * 128] for j in range(n_sub)]
            mc = subs[0]
            for x in subs[1:]:
                mc = jnp.maximum(mc, x)
            cq = cq_sc[h]
            m_prev = m_sc[h]
            m_new = jnp.maximum(m_prev, jnp.max(mc, axis=1, keepdims=True) + cq)
            alpha = jnp.exp2(m_prev - m_new)
            shift = m_new - cq
            ps = [jnp.exp2(x - shift) for x in subs]
            lsum = ps[0]
            for x in ps[1:]:
                lsum = lsum + x
            l_sc[h] = alpha * l_sc[h] + jnp.sum(lsum, axis=1, keepdims=True)
            p = jnp.concatenate(ps, axis=1).astype(BF16)
            acc_sc[:, hs] = alpha * acc_sc[:, hs] + _dot(p, v_ref[0, :, hs])
            m_sc[h] = m_new

    @pl.when(ki < qi)
    def _():
        block(False)

    @pl.when(ki == qi)
    def _():
        block(True)
        for h in range(n_heads):
            hs = slice(h * HEAD_DIM, (h + 1) * HEAD_DIM)
            o_ref[0, :, hs] = (acc_sc[:, hs] / l_sc[h]).astype(o_ref.dtype)


def _fox_attn_prompt(q, k, v, cq, ct, n_heads, tq):
    b, t, d_a = q.shape
    nq = t // tq
    pairs = [(i, j) for i in range(nq) for j in range(i + 1)]
    qi_tab = jnp.asarray([pr[0] for pr in pairs], jnp.int32)
    ki_tab = jnp.asarray([pr[1] for pr in pairs], jnp.int32)
    q_blk = pl.BlockSpec((1, tq, d_a), lambda bi, s, qt, kt: (bi, qt[s], 0))
    kv_blk = pl.BlockSpec((1, tq, d_a), lambda bi, s, qt, kt: (bi, kt[s], 0))
    grid_spec = pltpu.PrefetchScalarGridSpec(
        num_scalar_prefetch=2,
        grid=(b, len(pairs)),
        in_specs=[
            q_blk, kv_blk, kv_blk,
            pl.BlockSpec((1, n_heads, tq, 1), lambda bi, s, qt, kt: (bi, 0, qt[s], 0)),
            pl.BlockSpec((1, n_heads, tq), lambda bi, s, qt, kt: (bi, 0, kt[s])),
        ],
        out_specs=q_blk,
        scratch_shapes=[
            pltpu.VMEM((n_heads, tq, 128), F32),
            pltpu.VMEM((n_heads, tq, 128), F32),
            pltpu.VMEM((tq, d_a), F32),
            pltpu.VMEM((n_heads, tq, 128), F32),
        ],
    )
    return pl.pallas_call(
        functools.partial(_fox_attn_kernel, tq=tq, n_heads=n_heads),
        grid_spec=grid_spec,
        out_shape=jax.ShapeDtypeStruct((b, t, d_a), BF16),
        compiler_params=_params(("arbitrary", "arbitrary"), 48),
        name="fox_attn_prompt",
    )(qi_tab, ki_tab, q, k, v, cq, ct)


def _layernorm_silu(y, g, b):
    mu = jnp.mean(y, axis=-1, keepdims=True)
    yc = y - mu
    yn = yc * lax.rsqrt(jnp.mean(yc * yc, axis=-1, keepdims=True) + EPS) * g + b
    return yn * _sigmoid(yn)


def _mixer_kernel(ub_ref, vb_ref, ac_ref, gc_ref, gv_ref, ws_ref, bst_ref, cw_ref, cb_ref, lg_ref, lb_ref,
                  sgu_ref, conv_ref, tail_ref, g_sc, *, tm, n_heads_b, conv_w, halo):
    ti = pl.program_id(1)

    @pl.when(ti == 0)
    def _():
        g_sc[0:halo, :] = jnp.zeros((halo, g_sc.shape[1]), F32)

    u = _gelu(ub_ref[0])
    vn = _rms(_gelu(vb_ref[0]), gv_ref[...])
    r_i = lax.broadcasted_iota(jnp.int32, (CHUNK, CHUNK), 0)
    c_i = lax.broadcasted_iota(jnp.int32, (CHUNK, CHUNK), 1)
    for h in range(n_heads_b):
        hs = slice(h * HEAD_DIM, (h + 1) * HEAD_DIM)
        w_h = jnp.where(r_i >= c_i, ws_ref[h], 0.0).astype(BF16)
        bias_h = bst_ref[:, h:h + 1]
        for c in range(tm // CHUNK):
            rs = slice(c * CHUNK, (c + 1) * CHUNK)
            mix = _dot(w_h, vn[rs, hs].astype(BF16)) + bias_h
            sgu_ref[0, rs, hs] = (u[rs, hs] * mix).astype(sgu_ref.dtype)

    g_sc[halo:halo + tm, :] = ac_ref[0] * _sigmoid(gc_ref[0])
    base = halo - (conv_w - 1)
    n_rows = halo + tm
    window = g_sc[...]
    rotated = {0: window}
    y = cb_ref[...]
    for k in range(conv_w):
        a, r = divmod(base + k, 8)
        if r not in rotated:
            rotated[r] = pltpu.roll(window, n_rows - r, 0)
        y = y + rotated[r][8 * a:8 * a + tm, :] * cw_ref[k:k + 1, :]
    conv_ref[0] = _layernorm_silu(y, lg_ref[...], lb_ref[...]).astype(conv_ref.dtype)

    @pl.when(ti == pl.num_programs(1) - 1)
    def _():
        tail_ref[0] = g_sc[halo + tm - (conv_w - 1):halo + tm, :]

    g_sc[0:halo, :] = g_sc[tm:tm + halo, :]


def _mixer_prompt(z3, g_v, w_s, b_s_t, conv_w, conv_b, ln_g, ln_b, tm):
    b, t, _ = z3.shape
    n_heads_b = w_s.shape[0]
    d_b = n_heads_b * HEAD_DIM
    d_c = conv_w.shape[1]
    kw = conv_w.shape[0]
    halo = 32
    zcol = lambda c: pl.BlockSpec((1, tm, d_b), lambda bi, ti: (bi, ti, c))
    const = lambda shape: pl.BlockSpec(shape, lambda bi, ti: (0,) * len(shape))
    return pl.pallas_call(
        functools.partial(_mixer_kernel, tm=tm, n_heads_b=n_heads_b, conv_w=kw, halo=halo),
        grid=(b, t // tm),
        in_specs=[zcol(0), zcol(1), zcol(2), zcol(3),
                  const((1, d_b)), const((n_heads_b, CHUNK, CHUNK)), const((CHUNK, n_heads_b)),
                  const((kw, d_c)), const((1, d_c)), const((1, d_c)), const((1, d_c))],
        out_specs=[
            pl.BlockSpec((1, tm, d_b), lambda bi, ti: (bi, ti, 0)),
            pl.BlockSpec((1, tm, d_c), lambda bi, ti: (bi, ti, 0)),
            pl.BlockSpec((1, kw - 1, d_c), lambda bi, ti: (bi, 0, 0)),
        ],
        out_shape=[
            jax.ShapeDtypeStruct((b, t, d_b), BF16),
            jax.ShapeDtypeStruct((b, t, d_c), BF16),
            jax.ShapeDtypeStruct((b, kw - 1, d_c), F32),
        ],
        scratch_shapes=[pltpu.VMEM((halo + tm, d_c), F32)],
        compiler_params=_params(("arbitrary", "arbitrary"), 32),
        name="mixer_prompt",
    )(z3, z3, z3, z3, g_v, w_s, b_s_t, conv_w, conv_b, ln_g, ln_b)


def _mixer_sample_kernel(zs_ref, zf_ref, bf_ref, gv_ref, w0_ref, b0_ref, st_ref, cw_ref, cb_ref, lg_ref, lb_ref,
                         sgu_ref, conv_ref, state_ref, vn_ref, logf_ref, y_sc, *, d_b, conv_w):
    n = zs_ref.shape[0]
    u = _gelu(zs_ref[:, 0:d_b])
    vn = _rms(_gelu(zs_ref[:, d_b:2 * d_b]), gv_ref[...])
    vn_ref[...] = vn
    sgu_ref[...] = (u * (w0_ref[...] * vn + b0_ref[...])).astype(sgu_ref.dtype)
    glu = zs_ref[:, 2 * d_b:3 * d_b] * _sigmoid(zs_ref[:, 3 * d_b:4 * d_b])
    kw = conv_w - 1
    for bi in range(n):
        g_new = glu[bi:bi + 1, :]
        y_sc[bi:bi + 1, :] = (jnp.sum(st_ref[bi] * cw_ref[0:kw, :], axis=0, keepdims=True)
                              + g_new * cw_ref[kw:kw + 1, :])
        state_ref[bi, 0:kw - 1, :] = st_ref[bi, 1:kw, :]
        state_ref[bi, kw - 1:kw, :] = g_new
    conv_ref[...] = _layernorm_silu(y_sc[...] + cb_ref[...], lg_ref[...], lb_ref[...]).astype(conv_ref.dtype)
    logf_ref[...] = _log_sigmoid(zf_ref[...] + bf_ref[...])


def _mixer_sample(zs, zf, bf_pad, g_v, w0_row, b0_row, state, conv_w, conv_b, ln_g, ln_b):
    n = zs.shape[0]
    d_b = g_v.shape[1]
    kw, d_c = conv_w.shape
    full = lambda shape: pl.BlockSpec(shape, lambda i: (0,) * len(shape))
    args = (zs, zf, bf_pad, g_v, w0_row, b0_row, state, conv_w, conv_b, ln_g, ln_b)
    out_shape = [
        jax.ShapeDtypeStruct((n, d_b), BF16),
        jax.ShapeDtypeStruct((n, d_c), BF16),
        jax.ShapeDtypeStruct((n, kw - 1, d_c), F32),
        jax.ShapeDtypeStruct((n, d_b), F32),
        jax.ShapeDtypeStruct((n, 128), F32),
    ]
    return pl.pallas_call(
        functools.partial(_mixer_sample_kernel, d_b=d_b, conv_w=kw),
        grid=(1,),
        in_specs=[full(a.shape) for a in args],
        out_specs=[full(s.shape) for s in out_shape],
        out_shape=out_shape,
        scratch_shapes=[pltpu.VMEM((n, d_c), F32)],
        compiler_params=_params(("arbitrary",), 32),
        name="mixer_sample",
    )(*args)


def _decode_step(phase, p, dec, in_refs, o_ref, state, carry=None):
    n_grp, n_heads, scale = dec.n_grp, dec.n_heads, dec.scale
    zs_ref, cn_ref = in_refs[0], in_refs[1]
    refs = in_refs[2:]
    k_refs, v_refs, lf_refs = refs[:n_grp], refs[n_grp:2 * n_grp], refs[2 * n_grp:3 * n_grp]
    cn_sc, m_sc, l_sc, acc_sc, tail_sc = state
    rows_pp = k_refs[0].shape[2]
    n_flat = rows_pp // 128
    lane = lax.broadcasted_iota(jnp.int32, (n_heads, 128), 1)
    sub = lax.broadcasted_iota(jnp.int32, (n_heads, 128), 0)
    cls_mask = n_heads - 1
    own = sub == (lane & cls_mask)
    diag = sub == lane
    r_i = lax.broadcasted_iota(jnp.int32, (128, 128), 0)
    c_i = lax.broadcasted_iota(jnp.int32, (128, 128), 1)
    same = (r_i & cls_mask) == (c_i & cls_mask)

    if phase == "init":
        @pl.when(p == 0)
        def _():
            spread = _ones_where((r_i < n_heads) & ((c_i & cls_mask) == r_i))
            cn_sc[...] = _dot3(jnp.broadcast_to(cn_ref[0], (n_heads, 128)), spread)
            m_sc[...] = jnp.full_like(m_sc, -jnp.inf)
            l_sc[...] = jnp.zeros_like(l_sc)
            acc_sc[...] = jnp.zeros_like(acc_sc)
            tail_sc[...] = jnp.zeros_like(tail_sc)
        return

    if phase == "finish":
        @pl.when(p == dec.n_steps - 1)
        def _():
            m_col = jnp.max(jnp.where(diag, m_sc[...], -jnp.inf), axis=1, keepdims=True)
            l_row = jnp.sum(l_sc[...], axis=0, keepdims=True)
            l_col = jnp.sum(jnp.where(own, jnp.broadcast_to(l_row, (n_heads, 128)), 0.0), axis=1, keepdims=True)
            cn_col = jnp.sum(jnp.where(diag, cn_sc[...], 0.0), axis=1, keepdims=True)
            q_new = zs_ref[0, 0:n_heads, :]
            k_new = zs_ref[0, n_heads:2 * n_heads, :]
            v_new = zs_ref[0, 2 * n_heads:3 * n_heads, :]
            s_new = jnp.sum(q_new * k_new, axis=1, keepdims=True) * scale + cn_col - cn_col
            m_f = jnp.maximum(m_col, s_new)
            a_f = jnp.exp(m_col - m_f)
            p_new = jnp.exp(s_new - m_f)
            o_ref[0] = (a_f * acc_sc[...] + p_new * v_new) / (a_f * l_col + p_new)
        return

    if phase == "values":
        return _decode_values(dec, carry, v_refs, state, own, diag, n_flat)

    lf = jnp.concatenate([lf_refs[g][0, 0] for g in range(n_grp)], axis=0)
    n_rows = n_grp * n_flat
    row_tot = _dot3(lf, _ones_where(same))
    within = _dot3(lf, _ones_where(same & (r_i > c_i)))
    rr = lax.broadcasted_iota(jnp.int32, (n_rows, n_rows), 0)
    cc = lax.broadcasted_iota(jnp.int32, (n_rows, n_rows), 1)
    suffix = within + _dot3_left(_ones_where(cc > rr), row_tot) + tail_sc[0:1, :]

    q8 = zs_ref[0, 0:n_heads, :].astype(BF16)
    scores = []
    for g in range(n_grp):
        s_t = _dot_nt(q8, k_refs[g][0, 0].astype(BF16))
        flat = [jnp.sum(jnp.where(own, s_t[:, a * 128:(a + 1) * 128], 0.0), axis=0, keepdims=True)
                for a in range(n_flat)]
        scores.append(jnp.concatenate(flat, axis=0) * scale
                      + suffix[g * n_flat:(g + 1) * n_flat, :] + cn_sc[...])
    return scores, row_tot


def _decode_values(dec, carry, v_refs, state, own, diag, n_flat):
    n_grp, n_heads = dec.n_grp, dec.n_heads
    cn_sc, m_sc, l_sc, acc_sc, tail_sc = state
    scores, row_tot = carry
    m_step = scores[0]
    for x in scores[1:]:
        m_step = jnp.maximum(m_step, x)
    shift = 1
    while shift < n_flat:
        m_step = jnp.maximum(m_step, pltpu.roll(m_step, shift, 0))
        shift *= 2
    shift = n_heads
    while shift < 128:
        m_step = jnp.maximum(m_step, pltpu.roll(m_step, shift, 1))
        shift *= 2
    m_prev = m_sc[...]
    m_new = jnp.maximum(m_prev, m_step)
    alpha = jnp.exp(m_prev - m_new)
    probs = [jnp.exp(x - m_new) for x in scores]
    l_step = probs[0]
    for x in probs[1:]:
        l_step = l_step + x
    l_sc[...] = alpha * l_sc[...] + l_step
    pv = jnp.zeros((n_heads, HEAD_DIM), F32)
    for g in range(n_grp):
        blocks = [jnp.where(own, jnp.broadcast_to(probs[g][a:a + 1, :], (n_heads, 128)), 0.0)
                  for a in range(n_flat)]
        pv += _dot(jnp.concatenate(blocks, axis=1).astype(BF16), v_refs[g][0, 0].astype(BF16))
    alpha_col = jnp.sum(jnp.where(diag, alpha, 0.0), axis=1, keepdims=True)
    acc_sc[...] = alpha_col * acc_sc[...] + pv
    m_sc[...] = m_new
    tail_sc[...] += jnp.sum(row_tot, axis=0, keepdims=True)


class _DecodeStatic(NamedTuple):
    n_grp: int
    n_steps: int
    n_items: int
    n_heads: int
    scale: float


class _DecodePlan:
    def __init__(self, page_table, zs3, cn, cache_k, cache_v, cache_lf, layer, n_heads, b0, nb, n_grp):
        n_pages = page_table.shape[1]
        assert n_pages % n_grp == 0 and cache_k.shape[2] == 128 * n_heads and cache_lf.shape[2] == n_heads
        assert n_heads & (n_heads - 1) == 0 and n_heads <= 8
        self.page_table, self.layer, self.b0, self.nb, self.n_pages = page_table, layer, b0, nb, n_pages
        self.arrays = (zs3, cn, cache_k, cache_v, cache_lf)
        n_steps = n_pages // n_grp
        self.static = _DecodeStatic(n_grp, n_steps, nb * n_steps, n_heads, HEAD_DIM ** -0.5)
        self.n_items = nb * n_steps

    def operands(self, step_of):
        zs3, cn, cache_k, cache_v, cache_lf = self.arrays
        st, b0, layer, n_pages = self.static, self.b0, self.layer, self.n_pages
        n_heads, n_grp, n_steps = st.n_heads, st.n_grp, st.n_steps
        rows_pp = cache_k.shape[2]
        seq = lambda g0, g1: step_of(g0, g1) // n_steps

        def page_spec(block, g):
            def index(g0, g1, pt):
                s = step_of(g0, g1)
                return (pt[b0 + s // n_steps, n_pages - (s % n_steps + 1) * n_grp + g], layer, 0, 0)
            return pl.BlockSpec(block, index)

        kv_specs = [page_spec((1, 1, rows_pp, HEAD_DIM), g) for g in range(n_grp)]
        lf_specs = [page_spec((1, 1, n_heads, 128), g) for g in range(n_grp)]
        in_specs = [pl.BlockSpec((1, 3 * n_heads, HEAD_DIM), lambda g0, g1, pt: (b0 + seq(g0, g1), 0, 0)),
                    pl.BlockSpec((1, 1, 128), lambda g0, g1, pt: (b0 + seq(g0, g1), 0, 0))]
        in_specs += kv_specs + kv_specs + lf_specs
        args = [zs3, cn] + [cache_k] * n_grp + [cache_v] * n_grp + [cache_lf] * n_grp
        out_spec = pl.BlockSpec((1, n_heads, HEAD_DIM), lambda g0, g1, pt: (seq(g0, g1), 0, 0))
        out_shape = jax.ShapeDtypeStruct((self.nb, n_heads, HEAD_DIM), F32)
        scratch = [pltpu.VMEM((n_heads, 128), F32)] * 3 + [pltpu.VMEM((n_heads, HEAD_DIM), F32),
                                                           pltpu.VMEM((n_heads, 128), F32)]
        return in_specs, args, out_spec, out_shape, scratch


def kernel(x_prompt, x_sample, cache_k, cache_v, cache_logf, state_conv, page_table, w_in, b_f, g_v, w_s, b_s,
           conv_w, conv_b, ln_c_g, ln_c_b, w_o, g_pre_mix, g_post_mix, g_pre_mlp, g_post_mlp, w_up, w_down):
    bp, tp, d = x_prompt.shape
    bs, ds, _ = x_sample.shape
    assert ds == 1, "the sample path handles exactly one new token per sequence"
    depth = w_in.shape[0]
    n_pool, _, page, n_heads, _ = cache_k.shape
    d_a = n_heads * HEAD_DIM
    n_heads_b = w_s.shape[1]
    d_b = n_heads_b * HEAD_DIM
    d_c = conv_w.shape[2]
    off_f = 3 * d_a
    assert w_in.shape[2] == off_f + n_heads + 2 * d_b + 2 * d_c and d_b == d_c and page == CHUNK
    mp = bp * tp

    n_rest = 2 * d_b + 2 * d_c
    w_in_t = jnp.swapaxes(w_in, 1, 2)
    d_ff = w_up.shape[2]
    bf_pad = jnp.pad(b_f, ((0, 0), (0, 128 - n_heads)))
    row2 = lambda a, l: a[l][None, :]
    proj = functools.partial(_ws_matmul, transposed=True)

    n_pages = page_table.shape[1]
    assert bs % 2 == 0
    grp_up = (bs // 2) * n_pages // ((d_ff // UP_TILE[1]) * (mp // UP_TILE[0]))
    grp_down = (bs // 2) * n_pages // ((mp // DOWN_TILE[0]) * (d_ff // DOWN_TILE[1]))

    cache_k2 = cache_k.reshape(n_pool, depth, page * n_heads, HEAD_DIM)
    cache_v2 = cache_v.reshape(n_pool, depth, page * n_heads, HEAD_DIM)
    cache_lf = cache_logf.reshape(n_pool, depth, page * n_heads // 128, 128)

    xp = x_prompt.reshape(mp, d)
    xs = x_sample.reshape(bs, d)
    xpn = None
    xsn = _norm_bf16(xs, row2(g_pre_mix, 0), bs)
    fp_l, cp_l, ks_l, vs_l, fs_l, cs_l, us_l = ([] for _ in range(7))
    k_pages = v_pages = None
    for l in range(depth):
        g_next = row2(g_pre_mix, l + 1) if l + 1 < depth else None
        if xpn is None:
            q, zf, zs_q, zs_f, xpn = proj(xp, w_in_t, l, 0, d_a, 1024, d_a, "q", "proj_q", f_row=off_f, xs=xsn,
                                          norm_g=row2(g_pre_mix, l))
        else:
            q, zf, zs_q, zs_f = proj(xpn, w_in_t, l, 0, d_a, 1024, d_a, "q", "proj_q", f_row=off_f, xs=xsn)
        kb, k_pages, zs_k = proj(xpn, w_in_t, l, d_a, d_a, 1024, d_a, "kv", "proj_k", xs=xsn,
                                 pages=(k_pages, bp, tp, depth, n_heads, page))
        vb, v_pages, zs_v = proj(xpn, w_in_t, l, 2 * d_a, d_a, 1024, d_a, "kv", "proj_v", xs=xsn,
                                 pages=(v_pages, bp, tp, depth, n_heads, page))
        z_rest, w_o_l, zs_rest = proj(xpn, w_in_t, l, off_f + n_heads, n_rest, 1024, 1024, "f32", "proj_rest",
                                      side_cast=w_o, xs=xsn)
        z3 = z_rest.reshape(bp, tp, n_rest)
        logf, cq, ct = _fox_prefix(zf.reshape(bp, tp, 128), bf_pad[l][None, :], n_heads, 0)
        attn = _fox_attn_prompt(q.reshape(bp, tp, d_a), kb.reshape(bp, tp, d_a), vb.reshape(bp, tp, d_a),
                                cq, ct, n_heads, 512)
        fp_l.append(logf)
        w0_row = jnp.repeat(w_s[l][:, 0, 0], HEAD_DIM)[None, :]
        b0_row = jnp.repeat(b_s[l][:, 0], HEAD_DIM)[None, :]
        sgu_s, conv_s, state_s, vn_s, logf_s = _mixer_sample(
            zs_rest, zs_f, bf_pad[l][None, :], row2(g_v, l), w0_row, b0_row, state_conv[l], conv_w[l],
            row2(conv_b, l), row2(ln_c_g, l), row2(ln_c_b, l))
        zs3 = jnp.concatenate([zs_q, zs_k, zs_v], axis=1).reshape(bs, 3 * n_heads, HEAD_DIM)
        plan = lambda b0, n_grp: _DecodePlan(page_table, zs3, logf_s[:, None, :], cache_k2, cache_v2, cache_lf,
                                             l, n_heads, b0, bs // 2, n_grp)
        sgu, conv, conv_tail = _mixer_prompt(z3, row2(g_v, l), w_s[l], b_s[l].T, conv_w[l], row2(conv_b, l),
                                             row2(ln_c_g, l), row2(ln_c_b, l), MIXER_TM)
        cp_l.append(conv_tail)
        x1, x1n = _wo_block(attn.reshape(mp, d_a), sgu.reshape(mp, d_b), conv.reshape(mp, d_c), w_o_l, xp,
                            row2(g_post_mix, l), row2(g_pre_mlp, l), 512)
        hid, w_down_l, attn_s0 = _ws_matmul(x1n, w_up, l, 0, d_ff, UP_TILE[0], UP_TILE[1], "relu2", "mlp_up",
                                            side_cast=w_down, dec=plan(0, grp_up))
        xp, xpn, attn_s1 = _down_block(hid, w_down_l, x1, row2(g_post_mlp, l), g_next, DOWN_TILE[0],
                                       DOWN_TILE[1], dec=plan(bs // 2, grp_down))
        attn_s = jnp.concatenate([attn_s0, attn_s1], axis=0).reshape(bs, d_a).astype(BF16)
        x1s, x1sn = _wo_block(attn_s, sgu_s, conv_s, w_o_l, xs, row2(g_post_mix, l), row2(g_pre_mlp, l), bs)
        hid_s = _ws_matmul(x1sn, w_up, l, 0, d_ff, bs, 1024, "relu2", "mlp_up_sample")
        xs, xsn, _ = _down_block(hid_s, w_down_l, x1s, row2(g_post_mlp, l), g_next, bs, 2048)
        ks_l.append(zs_k)
        vs_l.append(zs_v)
        fs_l.append(logf_s[:, :n_heads])
        cs_l.append(state_s)
        us_l.append(vn_s)

    npp = tp // page
    kv_out = lambda pages: pages.reshape(bp, npp, depth, page, n_heads, HEAD_DIM)
    logf_prompt = jnp.stack(fp_l, axis=1).reshape(bp, depth, npp, page, n_heads).swapaxes(1, 2)
    return (xp.reshape(bp, tp, d), xs.reshape(bs, ds, d),
            kv_out(k_pages), kv_out(v_pages), logf_prompt,
            jnp.stack(cp_l, axis=0),
            jnp.stack(ks_l, axis=1).reshape(bs, depth, ds, n_heads, HEAD_DIM),
            jnp.stack(vs_l, axis=1).reshape(bs, depth, ds, n_heads, HEAD_DIM),
            jnp.stack(fs_l, axis=1).reshape(bs, depth, ds, n_heads),
            jnp.stack(cs_l, axis=0),
            jnp.stack(us_l, axis=0).reshape(depth, bs, ds, d_b))
```

```python
import functools
from typing import NamedTuple

import numpy as np
import jax
import jax.numpy as jnp
from jax import lax
from jax.experimental import pallas as pl
from jax.experimental.pallas import tpu as pltpu

F32 = jnp.float32
BF16 = jnp.bfloat16

HEAD_DIM = 128
CHUNK = 128
EPS = 1e-6
MIB = 1024 * 1024
GELU_C = float(np.sqrt(2.0 / np.pi))
LOG2E = float(np.log2(np.e))

UP_TILE = (1024, 1024)
DOWN_TILE = (512, 2048)
MIXER_TM = 512


def _params(semantics, vmem_mib):
    return pltpu.CompilerParams(dimension_semantics=semantics, vmem_limit_bytes=vmem_mib * MIB)


def _rms(x, g):
    return x * lax.rsqrt(jnp.mean(x * x, axis=-1, keepdims=True) + EPS) * g


def _gelu(x):
    return x * (0.5 * (1.0 + jnp.tanh(GELU_C * (x + 0.044715 * (x * x * x)))))


def _sigmoid(x):
    return 1.0 / (1.0 + jnp.exp(-x))


def _log_sigmoid(x):
    return jnp.minimum(x, 0.0) - jnp.log1p(jnp.exp(-jnp.abs(x)))


def _split3(x):
    hi = x.astype(BF16)
    r = x - hi.astype(F32)
    mid = r.astype(BF16)
    lo = (r - mid.astype(F32)).astype(BF16)
    return hi, mid, lo


def _dot(a, b):
    return jnp.dot(a, b, preferred_element_type=F32)


def _dot_nt(a, b):
    return lax.dot_general(a, b, (((1,), (1,)), ((), ())), preferred_element_type=F32)


def _dot3(x, w):
    hi, mid, lo = _split3(x)
    return (_dot(hi, w) + _dot(mid, w)) + _dot(lo, w)


def _dot3_left(w, x):
    hi, mid, lo = _split3(x)
    return (_dot(w, hi) + _dot(w, mid)) + _dot(w, lo)


def _ones_where(cond):
    return jnp.where(cond, 1.0, 0.0).astype(BF16)


def _norm_kernel(x_ref, g_ref, o_ref):
    o_ref[...] = _rms(x_ref[...], g_ref[...]).astype(o_ref.dtype)


def _norm_bf16(x, g, tm):
    m, d = x.shape
    return pl.pallas_call(
        _norm_kernel,
        grid=(m // tm,),
        in_specs=[pl.BlockSpec((tm, d), lambda i: (i, 0)), pl.BlockSpec((1, d), lambda i: (0, 0))],
        out_specs=pl.BlockSpec((tm, d), lambda i: (i, 0)),
        out_shape=jax.ShapeDtypeStruct((m, d), BF16),
        compiler_params=_params(("arbitrary",), 32),
        name="pre_norm",
    )(x, g)


def _ws_kernel(*refs, mode, transposed, with_f, has_prev, kv_slot, side_cast, with_xs, norm_in, dec, n_heads,
               page, q_scale):
    refs = list(refs)
    if dec is not None:
        refs = refs[1:]
    x_ref, w_ref = refs[0], refs[1]
    pos = 2
    if with_f:
        wf_ref = refs[pos]
        pos += 1
    if has_prev:
        pos += 1
    if side_cast:
        side_in = refs[pos]
        pos += 1
    if with_xs:
        xs_ref = refs[pos]
        pos += 1
    if norm_in:
        gn_ref = refs[pos]
        pos += 1
    if dec is not None:
        dec_in = refs[pos:pos + 2 + 3 * dec.n_grp]
        pos += 2 + 3 * dec.n_grp
    n_out = 2 if (mode == "kv" or with_f) else 1
    outs = refs[pos:pos + n_out]
    pos += n_out
    if side_cast:
        refs[pos][...] = side_in[...].astype(BF16)
        pos += 1
    if with_xs:
        xs_outs = refs[pos:pos + (2 if with_f else 1)]
        pos += len(xs_outs)
    if norm_in:
        xn_out = refs[pos]
        pos += 1
    if dec is not None:
        dec_out = refs[pos]
        pos += 1
    w_sc = refs[pos]
    pos += 1
    if with_f:
        wf_sc = refs[pos]
        pos += 1
    dec_state = refs[pos:]

    @pl.when(pl.program_id(1) == 0)
    def _():
        w_sc[...] = (w_ref[0] if transposed else w_ref[...]).astype(BF16)
        if with_f:
            pad = jnp.zeros((wf_sc.shape[0] - wf_ref.shape[1], wf_sc.shape[1]), F32)
            wf_sc[...] = jnp.concatenate([wf_ref[0], pad], axis=0).astype(BF16)
        if with_xs:
            xs = xs_ref[...]
            xs_outs[0][...] = _dot_nt(xs, w_sc[...]) if transposed else _dot(xs, w_sc[...])
            if with_f:
                xs_outs[1][...] = _dot_nt(xs, wf_sc[...])

    if dec is not None:
        dec_p = lax.rem(pl.program_id(0) * pl.num_programs(1) + pl.program_id(1), dec.n_steps)
        run_decode = functools.partial(_decode_step, p=dec_p, dec=dec, in_refs=dec_in, o_ref=dec_out,
                                       state=dec_state)
        run_decode("init")
        dec_carry = run_decode("scores")
    if norm_in:
        x = _rms(x_ref[...], gn_ref[...]).astype(BF16)
        xn_out[...] = x
    else:
        x = x_ref[...]
    acc = _dot_nt(x, w_sc[...]) if transposed else _dot(x, w_sc[...])
    if with_f:
        outs[1][...] = _dot_nt(x, wf_sc[...])
    if mode == "f32":
        outs[0][...] = acc
    elif mode == "q":
        outs[0][...] = (acc * q_scale).astype(BF16)
    elif mode == "relu2":
        h = jnp.maximum(acc, 0.0)
        outs[0][...] = (h * h).astype(BF16)
    else:
        outs[0][...] = acc.astype(BF16)
        dst_ref = outs[1]
        slot = () if has_prev else (kv_slot,)
        for pg in range(acc.shape[0] // page):
            for h in range(n_heads):
                dst_ref[(0, pg) + slot + (pl.ds(h, page, stride=n_heads), slice(None))] = (
                    acc[pg * page:(pg + 1) * page, h * HEAD_DIM:(h + 1) * HEAD_DIM])
        if not has_prev:
            for other in range(dst_ref.shape[2]):
                if other != kv_slot:
                    dst_ref[0, :, other] = jnp.zeros(dst_ref.shape[1:2] + dst_ref.shape[3:], F32)
    if dec is not None:
        run_decode("values", carry=dec_carry)
        run_decode("finish")


def _ws_matmul(x, w, layer, col0, n_cols, tm, tn, mode, name, *, transposed=False, f_row=None, pages=None,
               side_cast=None, xs=None, norm_g=None, dec=None):
    m, k = x.shape
    assert n_cols % tn == 0 and (f_row is None or (n_cols == tn and transposed))
    n_j, n_i = n_cols // tn, m // tm
    if transposed:
        w_spec = pl.BlockSpec((pl.Element(1), pl.Element(tn), pl.Element(k)),
                              lambda j, i, *_: (layer, pl.multiple_of(col0 + j * tn, 8), 0))
    else:
        assert col0 % tn == 0
        w_spec = pl.BlockSpec((None, k, tn), lambda j, i, *_: (layer, 0, col0 // tn + j))
    in_specs = [pl.BlockSpec((tm, k), lambda j, i, *_: (i, 0)), w_spec]
    args = [x, w]
    out_specs = [pl.BlockSpec((tm, tn), lambda j, i, *_: (i, j))]
    out_shape = [jax.ShapeDtypeStruct((m, n_cols), F32 if mode == "f32" else BF16)]
    scratch = [pltpu.VMEM((tn, k) if transposed else (k, tn), BF16)]
    if f_row is not None:
        in_specs.append(pl.BlockSpec((pl.Element(1), pl.Element(8), pl.Element(k)),
                                     lambda j, i, *_: (layer, f_row, 0)))
        args.append(w)
        out_specs.append(pl.BlockSpec((tm, 128), lambda j, i, *_: (i, 0)))
        out_shape.append(jax.ShapeDtypeStruct((m, 128), F32))
        scratch.append(pltpu.VMEM((128, k), BF16))
    aliases = {}
    n_heads = page = 0
    has_prev = False
    if mode == "kv":
        prev, bsz, t, depth, n_heads, page = pages
        assert n_cols == tn == n_heads * HEAD_DIM and t % tm == 0 and tm % page == 0
        tps = t // tm
        if prev is None:
            out_specs.append(pl.BlockSpec((1, tm // page, depth, page * n_heads, HEAD_DIM),
                                          lambda j, i, *_: (i // tps, i % tps, 0, 0, 0)))
        else:
            out_specs.append(pl.BlockSpec((1, tm // page, None, page * n_heads, HEAD_DIM),
                                          lambda j, i, *_: (i // tps, i % tps, layer, 0, 0)))
        out_shape.append(jax.ShapeDtypeStruct((bsz, t // page, depth, page * n_heads, HEAD_DIM), F32))
        if prev is not None:
            has_prev = True
            in_specs.append(pl.BlockSpec(memory_space=pl.ANY))
            args.append(prev)
            aliases = {2: 1}
    if side_cast is not None:
        _, side_r, side_c = side_cast.shape
        rows = side_r // (n_j * n_i)
        assert rows * n_j * n_i == side_r and rows % 16 == 0
        in_specs.append(pl.BlockSpec((None, rows, side_c), lambda j, i, *_: (layer, j * n_i + i, 0)))
        args.append(side_cast)
        out_specs.append(pl.BlockSpec((rows, side_c), lambda j, i, *_: (j * n_i + i, 0)))
        out_shape.append(jax.ShapeDtypeStruct((side_r, side_c), BF16))
    if xs is not None:
        ns = xs.shape[0]
        in_specs.append(pl.BlockSpec((ns, k), lambda j, i, *_: (0, 0)))
        args.append(xs)
        out_specs.append(pl.BlockSpec((ns, tn), lambda j, i, *_: (0, j)))
        out_shape.append(jax.ShapeDtypeStruct((ns, n_cols), F32))
        if f_row is not None:
            out_specs.append(pl.BlockSpec((ns, 128), lambda j, i, *_: (0, 0)))
            out_shape.append(jax.ShapeDtypeStruct((ns, 128), F32))
    if norm_g is not None:
        assert n_j == 1
        in_specs.append(pl.BlockSpec((1, k), lambda j, i, *_: (0, 0)))
        args.append(norm_g)
        out_specs.append(pl.BlockSpec((tm, k), lambda j, i, *_: (i, 0)))
        out_shape.append(jax.ShapeDtypeStruct((m, k), BF16))
    n_prefetch = 0
    if dec is not None:
        assert dec.n_items == n_j * n_i
        dec_specs, dec_args, dec_out_spec, dec_out_shape, dec_scratch = dec.operands(lambda j, i: j * n_i + i)
        in_specs += dec_specs
        args = [dec.page_table] + args + dec_args
        out_specs.append(dec_out_spec)
        out_shape.append(dec_out_shape)
        scratch += dec_scratch
        n_prefetch = 1
    grid_spec = pltpu.PrefetchScalarGridSpec(
        num_scalar_prefetch=n_prefetch, grid=(n_j, n_i), in_specs=in_specs, out_specs=out_specs,
        scratch_shapes=scratch)
    out = pl.pallas_call(
        functools.partial(_ws_kernel, mode=mode, transposed=transposed, with_f=f_row is not None,
                          has_prev=has_prev, kv_slot=layer, side_cast=side_cast is not None, with_xs=xs is not None,
                          norm_in=norm_g is not None,
                          dec=None if dec is None else dec.static, n_heads=n_heads, page=page,
                          q_scale=HEAD_DIM ** -0.5 * LOG2E),
        grid_spec=grid_spec,
        out_shape=out_shape,
        input_output_aliases=aliases,
        compiler_params=_params(("arbitrary", "arbitrary"), 58),
        name=name,
    )(*args)
    return out if len(out) > 1 else out[0]


def _wo_kernel(a_ref, s_ref, c_ref, w_ref, x_ref, g1_ref, g2_ref, x1_ref, xn_ref, *, d_a, d_b):
    n_split = 2 if a_ref.shape[0] % 32 == 0 else 1
    rows_per = a_ref.shape[0] // n_split
    for part in range(n_split):
        rows = slice(part * rows_per, (part + 1) * rows_per)
        mixed = _dot(a_ref[rows, :], w_ref[0:d_a, :])
        mixed += _dot(s_ref[rows, :], w_ref[d_a:d_a + d_b, :])
        mixed += _dot(c_ref[rows, :], w_ref[d_a + d_b:, :])
        x1 = x_ref[rows, :] + _rms(mixed, g1_ref[...])
        x1_ref[rows, :] = x1
        xn_ref[rows, :] = _rms(x1, g2_ref[...]).astype(xn_ref.dtype)


def _wo_block(attn, sgu, conv, w_o, x, g_post, g_pre_mlp, tm):
    m, d = x.shape
    d_a, d_b, d_c = attn.shape[1], sgu.shape[1], conv.shape[1]
    row = lambda width: pl.BlockSpec((tm, width), lambda i: (i, 0))
    const = lambda shape: pl.BlockSpec(shape, lambda i: (0, 0))
    w_spec = const((d, d))
    return pl.pallas_call(
        functools.partial(_wo_kernel, d_a=d_a, d_b=d_b),
        grid=(m // tm,),
        in_specs=[row(d_a), row(d_b), row(d_c), w_spec, row(d), const((1, d)), const((1, d))],
        out_specs=[row(d), row(d)],
        out_shape=[jax.ShapeDtypeStruct((m, d), F32), jax.ShapeDtypeStruct((m, d), BF16)],
        compiler_params=_params(("arbitrary",), 48),
        name="wo_norm_residual",
    )(attn, sgu, conv, w_o, x, g_post, g_pre_mlp)


def _down_kernel(*refs, emit_next, dec):
    refs = list(refs)
    if dec is not None:
        refs = refs[1:]
    h_ref, w_ref, x1_ref, g_ref = refs[:4]
    pos = 4
    if emit_next:
        gn_ref = refs[pos]
        pos += 1
    if dec is not None:
        dec_in = refs[pos:pos + 2 + 3 * dec.n_grp]
        pos += 2 + 3 * dec.n_grp
    x2_ref = refs[pos]
    pos += 1
    if emit_next:
        xn_ref = refs[pos]
        pos += 1
    if dec is not None:
        dec_out = refs[pos]
        pos += 1
    acc_ref = refs[pos]
    dec_state = refs[pos + 1:]
    k = pl.program_id(1)

    @pl.when(k == 0)
    def _():
        acc_ref[...] = jnp.zeros_like(acc_ref)

    if dec is not None:
        dec_p = lax.rem(pl.program_id(0) * pl.num_programs(1) + k, dec.n_steps)
        run_decode = functools.partial(_decode_step, p=dec_p, dec=dec, in_refs=dec_in, o_ref=dec_out,
                                       state=dec_state)
        run_decode("init")
        dec_carry = run_decode("scores")
    acc_ref[...] += _dot(h_ref[...], w_ref[...])
    if dec is not None:
        run_decode("values", carry=dec_carry)

    @pl.when(k == pl.num_programs(1) - 1)
    def _():
        x2 = x1_ref[...] + _rms(acc_ref[...], g_ref[...])
        x2_ref[...] = x2
        if emit_next:
            xn_ref[...] = _rms(x2, gn_ref[...]).astype(xn_ref.dtype)

    if dec is not None:
        run_decode("finish")


def _down_block(hid, w_down, x1, g_post, g_next, tm, tk, dec=None):
    m, d = x1.shape
    kdim = hid.shape[1]
    n_i, n_k = m // tm, kdim // tk
    emit_next = g_next is not None
    row = pl.BlockSpec((tm, d), lambda i, k, *_: (i, 0))
    const = pl.BlockSpec((1, d), lambda i, k, *_: (0, 0))
    x1_spec = pl.BlockSpec((tm, d), lambda i, k, *_: (i, 0), pipeline_mode=pl.Buffered(1))
    in_specs = [pl.BlockSpec((tm, tk), lambda i, k, *_: (i, k)),
                pl.BlockSpec((tk, d), lambda i, k, *_: (k, 0)), x1_spec, const]
    args = [hid, w_down, x1, g_post]
    out_specs = [row]
    out_shape = [jax.ShapeDtypeStruct((m, d), F32)]
    scratch = [pltpu.VMEM((tm, d), F32)]
    if emit_next:
        in_specs.append(const)
        args.append(g_next)
        out_specs.append(row)
        out_shape.append(jax.ShapeDtypeStruct((m, d), BF16))
    n_prefetch = 0
    if dec is not None:
        assert dec.n_items == n_i * n_k
        dec_specs, dec_args, dec_out_spec, dec_out_shape, dec_scratch = dec.operands(lambda i, k: i * n_k + k)
        in_specs += dec_specs
        args = [dec.page_table] + args + dec_args
        out_specs.append(dec_out_spec)
        out_shape.append(dec_out_shape)
        scratch += dec_scratch
        n_prefetch = 1
    grid_spec = pltpu.PrefetchScalarGridSpec(
        num_scalar_prefetch=n_prefetch, grid=(n_i, n_k), in_specs=in_specs, out_specs=out_specs,
        scratch_shapes=scratch)
    out = pl.pallas_call(
        functools.partial(_down_kernel, emit_next=emit_next, dec=None if dec is None else dec.static),
        grid_spec=grid_spec,
        out_shape=out_shape,
        compiler_params=_params(("arbitrary", "arbitrary"), 60),
        name="down_norm_residual",
    )(*args)
    return out[0], (out[1] if emit_next else None), (out[-1] if dec is not None else None)


def _prefix_kernel(zf_ref, bf_ref, logf_ref, cq_ref, ct_ref, c_sc, *, t, n_heads):
    lf = _log_sigmoid(zf_ref[0] + bf_ref[...])
    logf_ref[0] = lf[:, 0:n_heads]
    r_i = lax.broadcasted_iota(jnp.int32, (CHUNK, CHUNK), 0)
    c_i = lax.broadcasted_iota(jnp.int32, (CHUNK, CHUNK), 1)
    tri = _ones_where(r_i >= c_i)
    carry = jnp.zeros((1, 128), F32)
    for blk in range(t // CHUNK):
        cb = _dot3_left(tri, lf[blk * CHUNK:(blk + 1) * CHUNK, :]) + carry
        c_sc[blk * CHUNK:(blk + 1) * CHUNK, :] = cb * LOG2E
        carry = cb[CHUNK - 1:CHUNK, :]
    c = c_sc[...]
    ct_ref[0] = c.T[0:n_heads, :]
    for h in range(n_heads):
        cq_ref[0, h] = c[:, h:h + 1]


def _fox_prefix(z3, bf_pad, n_heads, f_blk):
    b, t, _ = z3.shape
    return pl.pallas_call(
        functools.partial(_prefix_kernel, t=t, n_heads=n_heads),
        grid=(b,),
        in_specs=[pl.BlockSpec((1, t, 128), lambda i: (i, 0, f_blk)), pl.BlockSpec((1, 128), lambda i: (0, 0))],
        out_specs=[
            pl.BlockSpec((1, t, n_heads), lambda i: (i, 0, 0)),
            pl.BlockSpec((1, n_heads, t, 1), lambda i: (i, 0, 0, 0)),
            pl.BlockSpec((1, n_heads, t), lambda i: (i, 0, 0)),
        ],
        out_shape=[
            jax.ShapeDtypeStruct((b, t, n_heads), F32),
            jax.ShapeDtypeStruct((b, n_heads, t, 1), F32),
            jax.ShapeDtypeStruct((b, n_heads, t), F32),
        ],
        scratch_shapes=[pltpu.VMEM((t, 128), F32)],
        compiler_params=_params(("arbitrary",), 48),
        name="fox_prefix",
    )(z3, bf_pad)


def _fox_attn_kernel(qi_ref, ki_ref, q_ref, k_ref, v_ref, cq_ref, ck_ref, o_ref, m_sc, l_sc, acc_sc, cq_sc,
                     *, tq, n_heads):
    step = pl.program_id(1)
    qi = qi_ref[step]
    ki = ki_ref[step]
    n_sub = tq // 128

    @pl.when(ki == 0)
    def _():
        m_sc[...] = jnp.full_like(m_sc, -jnp.inf)
        l_sc[...] = jnp.zeros_like(l_sc)
        acc_sc[...] = jnp.zeros_like(acc_sc)
        for h in range(n_heads):
            cq_sc[h] = jnp.broadcast_to(cq_ref[0, h], (tq, 128))

    def block(diagonal):
        if diagonal:
            keep = (lax.broadcasted_iota(jnp.int32, (tq, tq), 0) >= lax.broadcasted_iota(jnp.int32, (tq, tq), 1))
        head_cols = lambda h: slice(h * HEAD_DIM, (h + 1) * HEAD_DIM)
        qk = lambda h: _dot_nt(q_ref[0, :, head_cols(h)], k_ref[0, :, head_cols(h)])
        qk_next = qk(0)
        for h in range(n_heads):
            hs = head_cols(h)
            s = qk_next - ck_ref[0, h:h + 1, :]
            if h + 1 < n_heads:
                qk_next = qk(h + 1)
            if diagonal:
                s = jnp.where(keep, s, -jnp.inf)
            subs = [s[:, j * 128:(j + 1) * 128] for j in range(n_sub)]
            mc = subs[0]
            for x in subs[1:]:
                mc = jnp.maximum(mc, x)
            cq = cq_sc[h]
            m_prev = m_sc[h]
            m_new = jnp.maximum(m_prev, jnp.max(mc, axis=1, keepdims=True) + cq)
            alpha = jnp.exp2(m_prev - m_new)
            shift = m_new - cq
            ps = [jnp.exp2(x - shift) for x in subs]
            lsum = ps[0]
            for x in ps[1:]:
                lsum = lsum + x
            l_sc[h] = alpha * l_sc[h] + jnp.sum(lsum, axis=1, keepdims=True)
            p = jnp.concatenate(ps, axis=1).astype(BF16)
            acc_sc[:, hs] = alpha * acc_sc[:, hs] + _dot(p, v_ref[0, :, hs])
            m_sc[h] = m_new

    @pl.when(ki < qi)
    def _():
        block(False)

    @pl.when(ki == qi)
    def _():
        block(True)
        for h in range(n_heads):
            hs = slice(h * HEAD_DIM, (h + 1) * HEAD_DIM)
            o_ref[0, :, hs] = (acc_sc[:, hs] / l_sc[h]).astype(o_ref.dtype)


def _fox_attn_prompt(q, k, v, cq, ct, n_heads, tq):
    b, t, d_a = q.shape
    nq = t // tq
    pairs = [(i, j) for i in range(nq) for j in range(i + 1)]
    qi_tab = jnp.asarray([pr[0] for pr in pairs], jnp.int32)
    ki_tab = jnp.asarray([pr[1] for pr in pairs], jnp.int32)
    q_blk = pl.BlockSpec((1, tq, d_a), lambda bi, s, qt, kt: (bi, qt[s], 0))
    kv_blk = pl.BlockSpec((1, tq, d_a), lambda bi, s, qt, kt: (bi, kt[s], 0))
    grid_spec = pltpu.PrefetchScalarGridSpec(
        num_scalar_prefetch=2,
        grid=(b, len(pairs)),
        in_specs=[
            q_blk, kv_blk, kv_blk,
            pl.BlockSpec((1, n_heads, tq, 1), lambda bi, s, qt, kt: (bi, 0, qt[s], 0)),
            pl.BlockSpec((1, n_heads, tq), lambda bi, s, qt, kt: (bi, 0, kt[s])),
        ],
        out_specs=q_blk,
        scratch_shapes=[
            pltpu.VMEM((n_heads, tq, 128), F32),
            pltpu.VMEM((n_heads, tq, 128), F32),
            pltpu.VMEM((tq, d_a), F32),
            pltpu.VMEM((n_heads, tq, 128), F32),
        ],
    )
    return pl.pallas_call(
        functools.partial(_fox_attn_kernel, tq=tq, n_heads=n_heads),
        grid_spec=grid_spec,
        out_shape=jax.ShapeDtypeStruct((b, t, d_a), BF16),
        compiler_params=_params(("arbitrary", "arbitrary"), 48),
        name="fox_attn_prompt",
    )(qi_tab, ki_tab, q, k, v, cq, ct)


def _layernorm_silu(y, g, b):
    mu = jnp.mean(y, axis=-1, keepdims=True)
    yc = y - mu
    yn = yc * lax.rsqrt(jnp.mean(yc * yc, axis=-1, keepdims=True) + EPS) * g + b
    return yn * _sigmoid(yn)


def _mixer_kernel(ub_ref, vb_ref, ac_ref, gc_ref, gv_ref, ws_ref, bst_ref, cw_ref, cb_ref, lg_ref, lb_ref,
                  sgu_ref, conv_ref, tail_ref, g_sc, *, tm, n_heads_b, conv_w, halo):
    ti = pl.program_id(1)

    @pl.when(ti == 0)
    def _():
        g_sc[0:halo, :] = jnp.zeros((halo, g_sc.shape[1]), F32)

    u = _gelu(ub_ref[0])
    vn = _rms(_gelu(vb_ref[0]), gv_ref[...])
    r_i = lax.broadcasted_iota(jnp.int32, (CHUNK, CHUNK), 0)
    c_i = lax.broadcasted_iota(jnp.int32, (CHUNK, CHUNK), 1)
    for h in range(n_heads_b):
        hs = slice(h * HEAD_DIM, (h + 1) * HEAD_DIM)
        w_h = jnp.where(r_i >= c_i, ws_ref[h], 0.0).astype(BF16)
        bias_h = bst_ref[:, h:h + 1]
        for c in range(tm // CHUNK):
            rs = slice(c * CHUNK, (c + 1) * CHUNK)
            mix = _dot(w_h, vn[rs, hs].astype(BF16)) + bias_h
            sgu_ref[0, rs, hs] = (u[rs, hs] * mix).astype(sgu_ref.dtype)

    g_sc[halo:halo + tm, :] = ac_ref[0] * _sigmoid(gc_ref[0])
    base = halo - (conv_w - 1)
    n_rows = halo + tm
    window = g_sc[...]
    rotated = {0: window}
    y = cb_ref[...]
    for k in range(conv_w):
        a, r = divmod(base + k, 8)
        if r not in rotated:
            rotated[r] = pltpu.roll(window, n_rows - r, 0)
        y = y + rotated[r][8 * a:8 * a + tm, :] * cw_ref[k:k + 1, :]
    conv_ref[0] = _layernorm_silu(y, lg_ref[...], lb_ref[...]).astype(conv_ref.dtype)

    @pl.when(ti == pl.num_programs(1) - 1)
    def _():
        tail_ref[0] = g_sc[halo + tm - (conv_w - 1):halo + tm, :]

    g_sc[0:halo, :] = g_sc[tm:tm + halo, :]


def _mixer_prompt(z3, g_v, w_s, b_s_t, conv_w, conv_b, ln_g, ln_b, tm):
    b, t, _ = z3.shape
    n_heads_b = w_s.shape[0]
    d_b = n_heads_b * HEAD_DIM
    d_c = conv_w.shape[1]
    kw = conv_w.shape[0]
    halo = 32
    zcol = lambda c: pl.BlockSpec((1, tm, d_b), lambda bi, ti: (bi, ti, c))
    const = lambda shape: pl.BlockSpec(shape, lambda bi, ti: (0,) * len(shape))
    return pl.pallas_call(
        functools.partial(_mixer_kernel, tm=tm, n_heads_b=n_heads_b, conv_w=kw, halo=halo),
        grid=(b, t // tm),
        in_specs=[zcol(0), zcol(1), zcol(2), zcol(3),
                  const((1, d_b)), const((n_heads_b, CHUNK, CHUNK)), const((CHUNK, n_heads_b)),
                  const((kw, d_c)), const((1, d_c)), const((1, d_c)), const((1, d_c))],
        out_specs=[
            pl.BlockSpec((1, tm, d_b), lambda bi, ti: (bi, ti, 0)),
            pl.BlockSpec((1, tm, d_c), lambda bi, ti: (bi, ti, 0)),
            pl.BlockSpec((1, kw - 1, d_c), lambda bi, ti: (bi, 0, 0)),
        ],
        out_shape=[
            jax.ShapeDtypeStruct((b, t, d_b), BF16),
            jax.ShapeDtypeStruct((b, t, d_c), BF16),
            jax.ShapeDtypeStruct((b, kw - 1, d_c), F32),
        ],
        scratch_shapes=[pltpu.VMEM((halo + tm, d_c), F32)],
        compiler_params=_params(("arbitrary", "arbitrary"), 32),
        name="mixer_prompt",
    )(z3, z3, z3, z3, g_v, w_s, b_s_t, conv_w, conv_b, ln_g, ln_b)


def _mixer_sample_kernel(zs_ref, zf_ref, bf_ref, gv_ref, w0_ref, b0_ref, st_ref, cw_ref, cb_ref, lg_ref, lb_ref,
                         sgu_ref, conv_ref, state_ref, vn_ref, logf_ref, y_sc, *, d_b, conv_w):
    n = zs_ref.shape[0]
    u = _gelu(zs_ref[:, 0:d_b])
    vn = _rms(_gelu(zs_ref[:, d_b:2 * d_b]), gv_ref[...])
    vn_ref[...] = vn
    sgu_ref[...] = (u * (w0_ref[...] * vn + b0_ref[...])).astype(sgu_ref.dtype)
    glu = zs_ref[:, 2 * d_b:3 * d_b] * _sigmoid(zs_ref[:, 3 * d_b:4 * d_b])
    kw = conv_w - 1
    for bi in range(n):
        g_new = glu[bi:bi + 1, :]
        y_sc[bi:bi + 1, :] = (jnp.sum(st_ref[bi] * cw_ref[0:kw, :], axis=0, keepdims=True)
                              + g_new * cw_ref[kw:kw + 1, :])
        state_ref[bi, 0:kw - 1, :] = st_ref[bi, 1:kw, :]
        state_ref[bi, kw - 1:kw, :] = g_new
    conv_ref[...] = _layernorm_silu(y_sc[...] + cb_ref[...], lg_ref[...], lb_ref[...]).astype(conv_ref.dtype)
    logf_ref[...] = _log_sigmoid(zf_ref[...] + bf_ref[...])


def _mixer_sample(zs, zf, bf_pad, g_v, w0_row, b0_row, state, conv_w, conv_b, ln_g, ln_b):
    n = zs.shape[0]
    d_b = g_v.shape[1]
    kw, d_c = conv_w.shape
    full = lambda shape: pl.BlockSpec(shape, lambda i: (0,) * len(shape))
    args = (zs, zf, bf_pad, g_v, w0_row, b0_row, state, conv_w, conv_b, ln_g, ln_b)
    out_shape = [
        jax.ShapeDtypeStruct((n, d_b), BF16),
        jax.ShapeDtypeStruct((n, d_c), BF16),
        jax.ShapeDtypeStruct((n, kw - 1, d_c), F32),
        jax.ShapeDtypeStruct((n, d_b), F32),
        jax.ShapeDtypeStruct((n, 128), F32),
    ]
    return pl.pallas_call(
        functools.partial(_mixer_sample_kernel, d_b=d_b, conv_w=kw),
        grid=(1,),
        in_specs=[full(a.shape) for a in args],
        out_specs=[full(s.shape) for s in out_shape],
        out_shape=out_shape,
        scratch_shapes=[pltpu.VMEM((n, d_c), F32)],
        compiler_params=_params(("arbitrary",), 32),
        name="mixer_sample",
    )(*args)


def _decode_step(phase, p, dec, in_refs, o_ref, state, carry=None):
    n_grp, n_heads, scale = dec.n_grp, dec.n_heads, dec.scale
    zs_ref, cn_ref = in_refs[0], in_refs[1]
    refs = in_refs[2:]
    k_refs, v_refs, lf_refs = refs[:n_grp], refs[n_grp:2 * n_grp], refs[2 * n_grp:3 * n_grp]
    cn_sc, m_sc, l_sc, acc_sc, tail_sc = state
    rows_pp = k_refs[0].shape[2]
    n_flat = rows_pp // 128
    lane = lax.broadcasted_iota(jnp.int32, (n_heads, 128), 1)
    sub = lax.broadcasted_iota(jnp.int32, (n_heads, 128), 0)
    cls_mask = n_heads - 1
    own = sub == (lane & cls_mask)
    diag = sub == lane
    r_i = lax.broadcasted_iota(jnp.int32, (128, 128), 0)
    c_i = lax.broadcasted_iota(jnp.int32, (128, 128), 1)
    same = (r_i & cls_mask) == (c_i & cls_mask)

    if phase == "init":
        @pl.when(p == 0)
        def _():
            spread = _ones_where((r_i < n_heads) & ((c_i & cls_mask) == r_i))
            cn_sc[...] = _dot3(jnp.broadcast_to(cn_ref[0], (n_heads, 128)), spread)
            m_sc[...] = jnp.full_like(m_sc, -jnp.inf)
            l_sc[...] = jnp.zeros_like(l_sc)
            acc_sc[...] = jnp.zeros_like(acc_sc)
            tail_sc[...] = jnp.zeros_like(tail_sc)
        return

    if phase == "finish":
        @pl.when(p == dec.n_steps - 1)
        def _():
            m_col = jnp.max(jnp.where(diag, m_sc[...], -jnp.inf), axis=1, keepdims=True)
            l_row = jnp.sum(l_sc[...], axis=0, keepdims=True)
            l_col = jnp.sum(jnp.where(own, jnp.broadcast_to(l_row, (n_heads, 128)), 0.0), axis=1, keepdims=True)
            cn_col = jnp.sum(jnp.where(diag, cn_sc[...], 0.0), axis=1, keepdims=True)
            q_new = zs_ref[0, 0:n_heads, :]
            k_new = zs_ref[0, n_heads:2 * n_heads, :]
            v_new = zs_ref[0, 2 * n_heads:3 * n_heads, :]
            s_new = jnp.sum(q_new * k_new, axis=1, keepdims=True) * scale + cn_col - cn_col
            m_f = jnp.maximum(m_col, s_new)
            a_f = jnp.exp(m_col - m_f)
            p_new = jnp.exp(s_new - m_f)
            o_ref[0] = (a_f * acc_sc[...] + p_new * v_new) / (a_f * l_col + p_new)
        return

    if phase == "values":
        return _decode_values(dec, carry, v_refs, state, own, diag, n_flat)

    lf = jnp.concatenate([lf_refs[g][0, 0] for g in range(n_grp)], axis=0)
    n_rows = n_grp * n_flat
    row_tot = _dot3(lf, _ones_where(same))
    within = _dot3(lf, _ones_where(same & (r_i > c_i)))
    rr = lax.broadcasted_iota(jnp.int32, (n_rows, n_rows), 0)
    cc = lax.broadcasted_iota(jnp.int32, (n_rows, n_rows), 1)
    suffix = within + _dot3_left(_ones_where(cc > rr), row_tot) + tail_sc[0:1, :]

    q8 = zs_ref[0, 0:n_heads, :].astype(BF16)
    scores = []
    for g in range(n_grp):
        s_t = _dot_nt(q8, k_refs[g][0, 0].astype(BF16))
        flat = [jnp.sum(jnp.where(own, s_t[:, a * 128:(a + 1) * 128], 0.0), axis=0, keepdims=True)
                for a in range(n_flat)]
        scores.append(jnp.concatenate(flat, axis=0) * scale
                      + suffix[g * n_flat:(g + 1) * n_flat, :] + cn_sc[...])
    return scores, row_tot


def _decode_values(dec, carry, v_refs, state, own, diag, n_flat):
    n_grp, n_heads = dec.n_grp, dec.n_heads
    cn_sc, m_sc, l_sc, acc_sc, tail_sc = state
    scores, row_tot = carry
    m_step = scores[0]
    for x in scores[1:]:
        m_step = jnp.maximum(m_step, x)
    shift = 1
    while shift < n_flat:
        m_step = jnp.maximum(m_step, pltpu.roll(m_step, shift, 0))
        shift *= 2
    shift = n_heads
    while shift < 128:
        m_step = jnp.maximum(m_step, pltpu.roll(m_step, shift, 1))
        shift *= 2
    m_prev = m_sc[...]
    m_new = jnp.maximum(m_prev, m_step)
    alpha = jnp.exp(m_prev - m_new)
    probs = [jnp.exp(x - m_new) for x in scores]
    l_step = probs[0]
    for x in probs[1:]:
        l_step = l_step + x
    l_sc[...] = alpha * l_sc[...] + l_step
    pv = jnp.zeros((n_heads, HEAD_DIM), F32)
    for g in range(n_grp):
        blocks = [jnp.where(own, jnp.broadcast_to(probs[g][a:a + 1, :], (n_heads, 128)), 0.0)
                  for a in range(n_flat)]
        pv += _dot(jnp.concatenate(blocks, axis=1).astype(BF16), v_refs[g][0, 0].astype(BF16))
    alpha_col = jnp.sum(jnp.where(diag, alpha, 0.0), axis=1, keepdims=True)
    acc_sc[...] = alpha_col * acc_sc[...] + pv
    m_sc[...] = m_new
    tail_sc[...] += jnp.sum(row_tot, axis=0, keepdims=True)


class _DecodeStatic(NamedTuple):
    n_grp: int
    n_steps: int
    n_items: int
    n_heads: int
    scale: float


class _DecodePlan:
    def __init__(self, page_table, zs3, cn, cache_k, cache_v, cache_lf, layer, n_heads, b0, nb, n_grp):
        n_pages = page_table.shape[1]
        assert n_pages % n_grp == 0 and cache_k.shape[2] == 128 * n_heads and cache_lf.shape[2] == n_heads
        assert n_heads & (n_heads - 1) == 0 and n_heads <= 8
        self.page_table, self.layer, self.b0, self.nb, self.n_pages = page_table, layer, b0, nb, n_pages
        self.arrays = (zs3, cn, cache_k, cache_v, cache_lf)
        n_steps = n_pages // n_grp
        self.static = _DecodeStatic(n_grp, n_steps, nb * n_steps, n_heads, HEAD_DIM ** -0.5)
        self.n_items = nb * n_steps

    def operands(self, step_of):
        zs3, cn, cache_k, cache_v, cache_lf = self.arrays
        st, b0, layer, n_pages = self.static, self.b0, self.layer, self.n_pages
        n_heads, n_grp, n_steps = st.n_heads, st.n_grp, st.n_steps
        rows_pp = cache_k.shape[2]
        seq = lambda g0, g1: step_of(g0, g1) // n_steps

        def page_spec(block, g):
            def index(g0, g1, pt):
                s = step_of(g0, g1)
                return (pt[b0 + s // n_steps, n_pages - (s % n_steps + 1) * n_grp + g], layer, 0, 0)
            return pl.BlockSpec(block, index)

        kv_specs = [page_spec((1, 1, rows_pp, HEAD_DIM), g) for g in range(n_grp)]
        lf_specs = [page_spec((1, 1, n_heads, 128), g) for g in range(n_grp)]
        in_specs = [pl.BlockSpec((1, 3 * n_heads, HEAD_DIM), lambda g0, g1, pt: (b0 + seq(g0, g1), 0, 0)),
                    pl.BlockSpec((1, 1, 128), lambda g0, g1, pt: (b0 + seq(g0, g1), 0, 0))]
        in_specs += kv_specs + kv_specs + lf_specs
        args = [zs3, cn] + [cache_k] * n_grp + [cache_v] * n_grp + [cache_lf] * n_grp
        out_spec = pl.BlockSpec((1, n_heads, HEAD_DIM), lambda g0, g1, pt: (seq(g0, g1), 0, 0))
        out_shape = jax.ShapeDtypeStruct((self.nb, n_heads, HEAD_DIM), F32)
        scratch = [pltpu.VMEM((n_heads, 128), F32)] * 3 + [pltpu.VMEM((n_heads, HEAD_DIM), F32),
                                                           pltpu.VMEM((n_heads, 128), F32)]
        return in_specs, args, out_spec, out_shape, scratch


def kernel(x_prompt, x_sample, cache_k, cache_v, cache_logf, state_conv, page_table, w_in, b_f, g_v, w_s, b_s,
           conv_w, conv_b, ln_c_g, ln_c_b, w_o, g_pre_mix, g_post_mix, g_pre_mlp, g_post_mlp, w_up, w_down):
    bp, tp, d = x_prompt.shape
    bs, ds, _ = x_sample.shape
    assert ds == 1, "the sample path handles exactly one new token per sequence"
    depth = w_in.shape[0]
    n_pool, _, page, n_heads, _ = cache_k.shape
    d_a = n_heads * HEAD_DIM
    n_heads_b = w_s.shape[1]
    d_b = n_heads_b * HEAD_DIM
    d_c = conv_w.shape[2]
    off_f = 3 * d_a
    assert w_in.shape[2] == off_f + n_heads + 2 * d_b + 2 * d_c and d_b == d_c and page == CHUNK
    mp = bp * tp

    n_rest = 2 * d_b + 2 * d_c
    w_in_t = jnp.swapaxes(w_in, 1, 2)
    d_ff = w_up.shape[2]
    bf_pad = jnp.pad(b_f, ((0, 0), (0, 128 - n_heads)))
    row2 = lambda a, l: a[l][None, :]
    proj = functools.partial(_ws_matmul, transposed=True)

    n_pages = page_table.shape[1]
    assert bs % 2 == 0
    grp_up = (bs // 2) * n_pages // ((d_ff // UP_TILE[1]) * (mp // UP_TILE[0]))
    grp_down = (bs // 2) * n_pages // ((mp // DOWN_TILE[0]) * (d_ff // DOWN_TILE[1]))

    cache_k2 = cache_k.reshape(n_pool, depth, page * n_heads, HEAD_DIM)
    cache_v2 = cache_v.reshape(n_pool, depth, page * n_heads, HEAD_DIM)
    cache_lf = cache_logf.reshape(n_pool, depth, page * n_heads // 128, 128)

    xp = x_prompt.reshape(mp, d)
    xs = x_sample.reshape(bs, d)
    xpn = None
    xsn = _norm_bf16(xs, row2(g_pre_mix, 0), bs)
    fp_l, cp_l, ks_l, vs_l, fs_l, cs_l, us_l = ([] for _ in range(7))
    k_pages = v_pages = None
    for l in range(depth):
        g_next = row2(g_pre_mix, l + 1) if l + 1 < depth else None
        if xpn is None:
            q, zf, zs_q, zs_f, xpn = proj(xp, w_in_t, l, 0, d_a, 1024, d_a, "q", "proj_q", f_row=off_f, xs=xsn,
                                          norm_g=row2(g_pre_mix, l))
        else:
            q, zf, zs_q, zs_f = proj(xpn, w_in_t, l, 0, d_a, 1024, d_a, "q", "proj_q", f_row=off_f, xs=xsn)
        kb, k_pages, zs_k = proj(xpn, w_in_t, l, d_a, d_a, 1024, d_a, "kv", "proj_k", xs=xsn,
                                 pages=(k_pages, bp, tp, depth, n_heads, page))
        vb, v_pages, zs_v = proj(xpn, w_in_t, l, 2 * d_a, d_a, 1024, d_a, "kv", "proj_v", xs=xsn,
                                 pages=(v_pages, bp, tp, depth, n_heads, page))
        z_rest, w_o_l, zs_rest = proj(xpn, w_in_t, l, off_f + n_heads, n_rest, 1024, 1024, "f32", "proj_rest",
                                      side_cast=w_o, xs=xsn)
        z3 = z_rest.reshape(bp, tp, n_rest)
        logf, cq, ct = _fox_prefix(zf.reshape(bp, tp, 128), bf_pad[l][None, :], n_heads, 0)
        attn = _fox_attn_prompt(q.reshape(bp, tp, d_a), kb.reshape(bp, tp, d_a), vb.reshape(bp, tp, d_a),
                                cq, ct, n_heads, 512)
        fp_l.append(logf)
        w0_row = jnp.repeat(w_s[l][:, 0, 0], HEAD_DIM)[None, :]
        b0_row = jnp.repeat(b_s[l][:, 0], HEAD_DIM)[None, :]
        sgu_s, conv_s, state_s, vn_s, logf_s = _mixer_sample(
            zs_rest, zs_f, bf_pad[l][None, :], row2(g_v, l), w0_row, b0_row, state_conv[l], conv_w[l],
            row2(conv_b, l), row2(ln_c_g, l), row2(ln_c_b, l))
        zs3 = jnp.concatenate([zs_q, zs_k, zs_v], axis=1).reshape(bs, 3 * n_heads, HEAD_DIM)
        plan = lambda b0, n_grp: _DecodePlan(page_table, zs3, logf_s[:, None, :], cache_k2, cache_v2, cache_lf,
                                             l, n_heads, b0, bs // 2, n_grp)
        sgu, conv, conv_tail = _mixer_prompt(z3, row2(g_v, l), w_s[l], b_s[l].T, conv_w[l], row2(conv_b, l),
                                             row2(ln_c_g, l), row2(ln_c_b, l), MIXER_TM)
        cp_l.append(conv_tail)
        x1, x1n = _wo_block(attn.reshape(mp, d_a), sgu.reshape(mp, d_b), conv.reshape(mp, d_c), w_o_l, xp,
                            row2(g_post_mix, l), row2(g_pre_mlp, l), 512)
        hid, w_down_l, attn_s0 = _ws_matmul(x1n, w_up, l, 0, d_ff, UP_TILE[0], UP_TILE[1], "relu2", "mlp_up",
                                            side_cast=w_down, dec=plan(0, grp_up))
        xp, xpn, attn_s1 = _down_block(hid, w_down_l, x1, row2(g_post_mlp, l), g_next, DOWN_TILE[0],
                                       DOWN_TILE[1], dec=plan(bs // 2, grp_down))
        attn_s = jnp.concatenate([attn_s0, attn_s1], axis=0).reshape(bs, d_a).astype(BF16)
        x1s, x1sn = _wo_block(attn_s, sgu_s, conv_s, w_o_l, xs, row2(g_post_mix, l), row2(g_pre_mlp, l), bs)
        hid_s = _ws_matmul(x1sn, w_up, l, 0, d_ff, bs, 1024, "relu2", "mlp_up_sample")
        xs, xsn, _ = _down_block(hid_s, w_down_l, x1s, row2(g_post_mlp, l), g_next, bs, 2048)
        ks_l.append(zs_k)
        vs_l.append(zs_v)
        fs_l.append(logf_s[:, :n_heads])
        cs_l.append(state_s)
        us_l.append(vn_s)

    npp = tp // page
    kv_out = lambda pages: pages.reshape(bp, npp, depth, page, n_heads, HEAD_DIM)
    logf_prompt = jnp.stack(fp_l, axis=1).reshape(bp, depth, npp, page, n_heads).swapaxes(1, 2)
    return (xp.reshape(bp, tp, d), xs.reshape(bs, ds, d),
            kv_out(k_pages), kv_out(v_pages), logf_prompt,
            jnp.stack(cp_l, axis=0),
            jnp.stack(ks_l, axis=1).reshape(bs, depth, ds, n_heads, HEAD_DIM),
            jnp.stack(vs_l, axis=1).reshape(bs, depth, ds, n_heads, HEAD_DIM),
            jnp.stack(fs_l, axis=1).reshape(bs, depth, ds, n_heads),
            jnp.stack(cs_l, axis=0),
            jnp.stack(us_l, axis=0).reshape(depth, bs, ds, d_b))
```
